```python
import math
import jax, jax.numpy as jnp
from jax import lax
import numpy as np

D_MODEL = 2048
BATCH = 4
SEQ = 2048
DEPTH = 4
DEC_BATCH = 128
DEC_SEQ = 1
PAST_LEN = 16384
PAGE_SIZE = 128

N_MIXERS = 2
N_SG_LAYERS = (DEPTH + 1) // 2
N_SSM_LAYERS = DEPTH // 2
MIX_WIDTH = D_MODEL
XA_HEADS = 4
XA_HEAD_DIM = D_MODEL // 16
XA_WIDTH = XA_HEADS * XA_HEAD_DIM
N_MEM = 256
TOK_WIDTH = MIX_WIDTH - XA_WIDTH
CHUNK = 128
SG_WIDTH = TOK_WIDTH
SG_GROUP_DIM = 128
SG_GROUPS = SG_WIDTH // SG_GROUP_DIM
SSM_WIDTH = TOK_WIDTH
SSM_GROUP_DIM = 16
SSM_GROUPS = SSM_WIDTH // SSM_GROUP_DIM
SSM_STATE = 64
DT_MIN = 1e-3
DT_MAX = 1e-1
D_FF = ((8 * D_MODEL // 3 + 127) // 128) * 128
CONV_W = 3
EPS = 1e-6

kernel_name = "hybrid_sgmlp_s5_memxattn_convffn_step"


def rmsnorm(x, g):
    x32 = x.astype(jnp.float32)
    y = x32 * lax.rsqrt(jnp.mean(x32 * x32, axis=-1, keepdims=True) + EPS)
    return y.astype(x.dtype) * g


def mem_kv(mem, g, w):
    k, v = jnp.split(rmsnorm(mem, g) @ w, 2, axis=-1)
    b, m = mem.shape[0], mem.shape[1]
    return (k.reshape(b, m, XA_HEADS, XA_HEAD_DIM), v.reshape(b, m, XA_HEADS, XA_HEAD_DIM))


def cross_attend(q, k, v):
    b, l = q.shape[0], q.shape[1]
    qh = q.reshape(b, l, XA_HEADS, XA_HEAD_DIM)
    s = jnp.einsum('blhd,bmhd->bhlm', qh, k).astype(jnp.float32) * (XA_HEAD_DIM ** -0.5)
    p = jax.nn.softmax(s, axis=-1).astype(v.dtype)
    o = jnp.einsum('bhlm,bmhd->blhd', p, v)
    return o.reshape(b, l, XA_WIDTH)


def spatial_gate(v, w_s, b_s):
    b, l, w = v.shape
    pad = (-l) % CHUNK
    vp = jnp.pad(v, ((0, 0), (0, pad), (0, 0)))
    n = vp.shape[1] // CHUNK
    vc = vp.reshape(b, n, CHUNK, SG_GROUPS, SG_GROUP_DIM)
    mask = jnp.tril(jnp.ones((CHUNK, CHUNK), dtype=bool))
    wm = jnp.where(mask, w_s, 0)
    s = jnp.einsum('gts,bnsgd->bntgd', wm, vc) + b_s.T[:, :, None]
    return s.reshape(b, n * CHUNK, w)[:, :l]


def sg_mixer(h, w_in, w_out, g_v, w_s, b_s, mk, mv):
    z = h @ w_in
    uv, q = z[..., :2 * SG_WIDTH], z[..., 2 * SG_WIDTH:]
    u, v = jnp.split(jax.nn.gelu(uv), 2, axis=-1)
    v = rmsnorm(v, g_v)
    tok = u * spatial_gate(v, w_s, b_s)
    out = jnp.concatenate([tok, cross_attend(q, mk, mv)], axis=-1) @ w_out
    return out, v


def s5_scan(u, s_re, s_im, lam_re, lam_im, log_dt, b_re, b_im, c_re, c_im, d):
    f = jnp.float32
    dtype = u.dtype
    bsz, l = u.shape[0], u.shape[1]
    lam_re, lam_im = lam_re.astype(f), lam_im.astype(f)
    dt = jnp.exp(log_dt.astype(f))[:, None]
    ar, ai = lam_re * dt, lam_im * dt
    mag = jnp.exp(ar)
    lb_re, lb_im = mag * jnp.cos(ai), mag * jnp.sin(ai)
    nr, ni = lb_re - 1.0, lb_im
    den = lam_re * lam_re + lam_im * lam_im
    k_re = (nr * lam_re + ni * lam_im) / den
    k_im = (ni * lam_re - nr * lam_im) / den
    ug = u.astype(f).reshape(bsz, l, SSM_GROUPS, SSM_GROUP_DIM)
    bu_re = jnp.einsum('blhc,hpc->blhp', ug, b_re.astype(f))
    bu_im = jnp.einsum('blhc,hpc->blhp', ug, b_im.astype(f))
    x_re = k_re * bu_re - k_im * bu_im
    x_im = k_re * bu_im + k_im * bu_re
    a_re = jnp.broadcast_to(lb_re, x_re.shape)
    a_im = jnp.broadcast_to(lb_im, x_im.shape)

    def combine(e1, e2):
        a1r, a1i, b1r, b1i = e1
        a2r, a2i, b2r, b2i = e2
        return (a2r * a1r - a2i * a1i, a2r * a1i + a2i * a1r,
                a2r * b1r - a2i * b1i + b2r, a2r * b1i + a2i * b1r + b2i)

    _, _, h_re, h_im = lax.associative_scan(combine, (a_re, a_im, x_re, x_im), axis=1)
    steps = jnp.arange(1, l + 1, dtype=f)[:, None, None]
    pmag = jnp.exp(ar * steps)
    p_re, p_im = pmag * jnp.cos(ai * steps), pmag * jnp.sin(ai * steps)
    s0r, s0i = s_re.astype(f)[:, None], s_im.astype(f)[:, None]
    h_re = h_re + p_re * s0r - p_im * s0i
    h_im = h_im + p_re * s0i + p_im * s0r
    y = (jnp.einsum('blhp,hcp->blhc', h_re, c_re.astype(f))
         - jnp.einsum('blhp,hcp->blhc', h_im, c_im.astype(f)))
    y = y.reshape(bsz, l, SSM_WIDTH) + d.astype(f) * u.astype(f)
    return y.astype(dtype), h_re[:, -1].astype(s_re.dtype), h_im[:, -1].astype(s_im.dtype)


def ssm_mixer(h, w_in, w_out, lam_re, lam_im, log_dt, b_re, b_im, c_re, c_im, d,
              w_glu, b_glu, mk, mv, s_re, s_im):
    z = h @ w_in
    u, q = z[..., :SSM_WIDTH], z[..., SSM_WIDTH:]
    y, n_re, n_im = s5_scan(u, s_re, s_im, lam_re, lam_im, log_dt, b_re, b_im, c_re, c_im, d)
    y = jax.nn.gelu(y)
    y = y * jax.nn.sigmoid(y @ w_glu + b_glu)
    out = jnp.concatenate([y, cross_attend(q, mk, mv)], axis=-1) @ w_out
    return out, n_re, n_im


def conv_ffn(h, w_up, conv_w, conv_b, w_down, prev):
    a, g = jnp.split(h @ w_up, 2, axis=-1)
    l = a.shape[1]
    full = jnp.concatenate([prev.astype(a.dtype), a], axis=1)
    c = conv_b
    for j in range(CONV_W):
        c = c + conv_w[j] * full[:, j:j + l]
    y = jax.nn.silu(c) * g
    return y @ w_down, full[:, l:]


def trunk(x, mem_k, mem_v, ssm_re, ssm_im, conv_prev, p):
    sg_v, s_re, s_im, conv_new = [], [], [], []
    for i in range(DEPTH):
        j = i // N_MIXERS
        h = rmsnorm(x, p['g_mix'][i])
        if i % N_MIXERS == 0:
            out, v = sg_mixer(h, p['sg_w_in'][j], p['sg_w_out'][j], p['sg_g_v'][j],
                              p['sg_w_s'][j], p['sg_b_s'][j], mem_k[i], mem_v[i])
            sg_v.append(v)
        else:
            out, r, im = ssm_mixer(h, p['ssm_w_in'][j], p['ssm_w_out'][j], p['ssm_lam_re'][j],
                                   p['ssm_lam_im'][j], p['ssm_log_dt'][j], p['ssm_b_re'][j],
                                   p['ssm_b_im'][j], p['ssm_c_re'][j], p['ssm_c_im'][j],
                                   p['ssm_d'][j], p['ssm_w_glu'][j], p['ssm_b_glu'][j],
                                   mem_k[i], mem_v[i], ssm_re[j], ssm_im[j])
            s_re.append(r)
            s_im.append(im)
        x = x + out
        h = rmsnorm(x, p['g_ffn'][i])
        out, c = conv_ffn(h, p['ffn_w_up'][i], p['ffn_conv_w'][i], p['ffn_conv_b'][i],
                          p['ffn_w_down'][i], conv_prev[i])
        conv_new.append(c)
        x = x + out
    return (rmsnorm(x, p['g_final']), jnp.stack(sg_v), jnp.stack(s_re), jnp.stack(s_im),
            jnp.stack(conv_new))


def setup_inputs(seed: int = 0) -> dict:
    key = jax.random.key(seed)
    ks = iter(jax.random.split(key, 48))
    f = jnp.float32

    def nrm(shape, scale):
        return jax.random.normal(next(ks), shape, f) * scale

    def gain(shape):
        return 1.0 + nrm(shape, 0.01)

    lam_im = (jnp.broadcast_to(jnp.pi * jnp.arange(SSM_STATE, dtype=f), (N_SSM_LAYERS, SSM_GROUPS, SSM_STATE))
              + nrm((N_SSM_LAYERS, SSM_GROUPS, SSM_STATE), 0.01))
    return {
        'x_prompt': nrm((BATCH, SEQ, D_MODEL), 1.0),
        'x_sample': nrm((DEC_BATCH, DEC_SEQ, D_MODEL), 1.0),
        'mem_prompt': nrm((BATCH, N_MEM, D_MODEL), 1.0),
        'cache_mem_k': nrm((DEPTH, DEC_BATCH, N_MEM, XA_HEADS, XA_HEAD_DIM), 1.0),
        'cache_mem_v': nrm((DEPTH, DEC_BATCH, N_MEM, XA_HEADS, XA_HEAD_DIM), 1.0),
        'state_ssm_re': nrm((N_SSM_LAYERS, DEC_BATCH, SSM_GROUPS, SSM_STATE), 0.1),
        'state_ssm_im': nrm((N_SSM_LAYERS, DEC_BATCH, SSM_GROUPS, SSM_STATE), 0.1),
        'state_conv': nrm((DEPTH, DEC_BATCH, CONV_W - 1, D_FF), 1.0),
        'g_mix': gain((DEPTH, D_MODEL)),
        'g_ffn': gain((DEPTH, D_MODEL)),
        'g_mem': gain((DEPTH, D_MODEL)),
        'g_final': gain((D_MODEL,)),
        'w_mem_kv': nrm((DEPTH, D_MODEL, 2 * XA_WIDTH), D_MODEL ** -0.5),
        'sg_w_in': nrm((N_SG_LAYERS, D_MODEL, 2 * SG_WIDTH + XA_WIDTH), D_MODEL ** -0.5),
        'sg_w_out': nrm((N_SG_LAYERS, SG_WIDTH + XA_WIDTH, D_MODEL), (SG_WIDTH + XA_WIDTH) ** -0.5),
        'sg_g_v': gain((N_SG_LAYERS, SG_WIDTH)),
        'sg_w_s': nrm((N_SG_LAYERS, SG_GROUPS, CHUNK, CHUNK), CHUNK ** -0.5),
        'sg_b_s': gain((N_SG_LAYERS, SG_GROUPS, CHUNK)),
        'ssm_w_in': nrm((N_SSM_LAYERS, D_MODEL, SSM_WIDTH + XA_WIDTH), D_MODEL ** -0.5),
        'ssm_w_out': nrm((N_SSM_LAYERS, SSM_WIDTH + XA_WIDTH, D_MODEL), (SSM_WIDTH + XA_WIDTH) ** -0.5),
        'ssm_lam_re': -0.5 + nrm((N_SSM_LAYERS, SSM_GROUPS, SSM_STATE), 0.01),
        'ssm_lam_im': lam_im,
        'ssm_log_dt': jax.random.uniform(next(ks), (N_SSM_LAYERS, SSM_GROUPS), f,
                                         math.log(DT_MIN), math.log(DT_MAX)),
        'ssm_b_re': nrm((N_SSM_LAYERS, SSM_GROUPS, SSM_STATE, SSM_GROUP_DIM), (2 * SSM_GROUP_DIM) ** -0.5),
        'ssm_b_im': nrm((N_SSM_LAYERS, SSM_GROUPS, SSM_STATE, SSM_GROUP_DIM), (2 * SSM_GROUP_DIM) ** -0.5),
        'ssm_c_re': nrm((N_SSM_LAYERS, SSM_GROUPS, SSM_GROUP_DIM, SSM_STATE), SSM_STATE ** -0.5),
        'ssm_c_im': nrm((N_SSM_LAYERS, SSM_GROUPS, SSM_GROUP_DIM, SSM_STATE), SSM_STATE ** -0.5),
        'ssm_d': nrm((N_SSM_LAYERS, SSM_WIDTH), 1.0),
        'ssm_w_glu': nrm((N_SSM_LAYERS, SSM_WIDTH, SSM_WIDTH), SSM_WIDTH ** -0.5),
        'ssm_b_glu': nrm((N_SSM_LAYERS, SSM_WIDTH), 0.01),
        'ffn_w_up': nrm((DEPTH, D_MODEL, 2 * D_FF), D_MODEL ** -0.5),
        'ffn_conv_w': nrm((DEPTH, CONV_W, D_FF), CONV_W ** -0.5),
        'ffn_conv_b': nrm((DEPTH, D_FF), 0.01),
        'ffn_w_down': nrm((DEPTH, D_FF, D_MODEL), D_FF ** -0.5),
    }


def reference(x_prompt, x_sample, mem_prompt, cache_mem_k, cache_mem_v, state_ssm_re, state_ssm_im,
              state_conv, g_mix, g_ffn, g_mem, g_final, w_mem_kv, sg_w_in, sg_w_out, sg_g_v, sg_w_s,
              sg_b_s, ssm_w_in, ssm_w_out, ssm_lam_re, ssm_lam_im, ssm_log_dt, ssm_b_re, ssm_b_im,
              ssm_c_re, ssm_c_im, ssm_d, ssm_w_glu, ssm_b_glu, ffn_w_up, ffn_conv_w, ffn_conv_b,
              ffn_w_down):
    p = {'g_mix': g_mix, 'g_ffn': g_ffn, 'g_final': g_final,
         'sg_w_in': sg_w_in, 'sg_w_out': sg_w_out, 'sg_g_v': sg_g_v, 'sg_w_s': sg_w_s, 'sg_b_s': sg_b_s,
         'ssm_w_in': ssm_w_in, 'ssm_w_out': ssm_w_out, 'ssm_lam_re': ssm_lam_re, 'ssm_lam_im': ssm_lam_im,
         'ssm_log_dt': ssm_log_dt, 'ssm_b_re': ssm_b_re, 'ssm_b_im': ssm_b_im, 'ssm_c_re': ssm_c_re,
         'ssm_c_im': ssm_c_im, 'ssm_d': ssm_d, 'ssm_w_glu': ssm_w_glu, 'ssm_b_glu': ssm_b_glu,
         'ffn_w_up': ffn_w_up, 'ffn_conv_w': ffn_conv_w, 'ffn_conv_b': ffn_conv_b, 'ffn_w_down': ffn_w_down}
    bsz = x_prompt.shape[0]
    dt = x_prompt.dtype
    kv = [mem_kv(mem_prompt, g_mem[i], w_mem_kv[i]) for i in range(DEPTH)]
    mem_k_prompt = jnp.stack([k for k, _ in kv])
    mem_v_prompt = jnp.stack([v for _, v in kv])
    zero_re = jnp.zeros((N_SSM_LAYERS, bsz, SSM_GROUPS, SSM_STATE), dt)
    zero_im = jnp.zeros((N_SSM_LAYERS, bsz, SSM_GROUPS, SSM_STATE), dt)
    zero_conv = jnp.zeros((DEPTH, bsz, CONV_W - 1, D_FF), dt)
    y_prompt, _, ssm_re_prompt, ssm_im_prompt, conv_prompt = trunk(
        x_prompt, mem_k_prompt, mem_v_prompt, zero_re, zero_im, zero_conv, p)
    y_sample, sg_v_sample, ssm_re_sample, ssm_im_sample, conv_sample = trunk(
        x_sample, cache_mem_k, cache_mem_v, state_ssm_re, state_ssm_im, state_conv, p)
    return (y_prompt, y_sample, mem_k_prompt, mem_v_prompt, ssm_re_prompt, ssm_im_prompt, conv_prompt,
            ssm_re_sample, ssm_im_sample, conv_sample, sg_v_sample)
```

```python
import functools
import math

import jax
import jax.numpy as jnp
from jax import lax
from jax.experimental import pallas as pl
from jax.experimental.pallas import tpu as pltpu

F32 = jnp.float32
BF16 = jnp.bfloat16

D_MODEL = 2048
BATCH = 4
SEQ = 2048
DEPTH = 4
DEC_BATCH = 128
N_MEM = 256
XA_HEADS = 4
XA_HEAD_DIM = 128
XA_WIDTH = 512
TOK_WIDTH = 1536
CHUNK = 128
SG_GROUPS = 12
SSM_GROUPS = 96
SSM_GROUP_DIM = 16
SSM_STATE = 64
D_FF = 5504
EPS = 1e-6

N_TILES = 8
P_ROWS = 1024
S_ROWS = 16
TILE = P_ROWS + S_ROWS
S_BLK = P_ROWS // S_ROWS

LANE = 128
FF_BLOCKS = D_FF // LANE
FF_PAD = (FF_BLOCKS + 1) * LANE
TK = 512

S5_CB = 256
S5_NCB = TOK_WIDTH // S5_CB
S5_ST = (S5_CB // SSM_GROUP_DIM) * SSM_STATE
S5_SLABS = S5_ST // LANE
S5_TQ = 512
S5_PITCH = S5_TQ + 8
V7X_VMEM_LIMIT = 56 * 1024 * 1024


def _cparams(n_axes, vmem_mb=None):
    kw = dict(dimension_semantics=("arbitrary",) * n_axes)
    if vmem_mb is not None:
        kw["vmem_limit_bytes"] = min(int(vmem_mb * 1024 * 1024), V7X_VMEM_LIMIT)
    return pltpu.CompilerParams(**kw)


def _rms(x, g):
    return x * lax.rsqrt(jnp.mean(x * x, axis=-1, keepdims=True) + EPS) * g


def _prep_kernel(xp_ref, xs_ref, g_ref, x_ref, h_ref):
    xp = xp_ref[0]
    xs = xs_ref[0]
    g = g_ref[...]
    x_ref[0, :P_ROWS] = xp
    x_ref[0, P_ROWS:] = xs
    h_ref[0, :P_ROWS] = _rms(xp, g).astype(BF16)
    h_ref[0, P_ROWS:] = _rms(xs, g).astype(BF16)


def _prep(x_prompt, x_sample, g):
    xp = x_prompt.reshape(N_TILES, P_ROWS, D_MODEL)
    xs = x_sample.reshape(N_TILES, S_ROWS, D_MODEL)
    return pl.pallas_call(
        _prep_kernel,
        grid=(N_TILES,),
        in_specs=[
            pl.BlockSpec((1, P_ROWS, D_MODEL), lambda j: (j, 0, 0)),
            pl.BlockSpec((1, S_ROWS, D_MODEL), lambda j: (j, 0, 0)),
            pl.BlockSpec((1, D_MODEL), lambda j: (0, 0)),
        ],
        out_specs=[
            pl.BlockSpec((1, TILE, D_MODEL), lambda j: (j, 0, 0)),
            pl.BlockSpec((1, TILE, D_MODEL), lambda j: (j, 0, 0)),
        ],
        out_shape=[
            jax.ShapeDtypeStruct((N_TILES, TILE, D_MODEL), F32),
            jax.ShapeDtypeStruct((N_TILES, TILE, D_MODEL), BF16),
        ],
        compiler_params=_cparams(1, 52),
        name="prep",
    )(xp, xs, g.reshape(1, D_MODEL))


def _mem_kv_kernel(m_ref, g_ref, w_ref, o_ref):
    h = _rms(m_ref[...], g_ref[0]).astype(BF16)
    o_ref[0, 0] = jnp.dot(h, w_ref[0].astype(BF16), preferred_element_type=F32)


def _mem_kv(mem_prompt, g_mem, w_mem_kv):
    rows = BATCH * N_MEM
    mem = mem_prompt.reshape(rows, D_MODEL)
    return pl.pallas_call(
        _mem_kv_kernel,
        grid=(DEPTH, 2),
        in_specs=[
            pl.BlockSpec((rows, D_MODEL), lambda i, n: (0, 0)),
            pl.BlockSpec((1, 1, D_MODEL), lambda i, n: (i, 0, 0)),
            pl.BlockSpec((1, D_MODEL, XA_WIDTH), lambda i, n: (i, 0, n)),
        ],
        out_specs=pl.BlockSpec((1, 1, rows, XA_WIDTH), lambda i, n: (i, n, 0, 0)),
        out_shape=jax.ShapeDtypeStruct((DEPTH, 2, rows, XA_WIDTH), F32),
        compiler_params=_cparams(2, 40),
        name="mem_kv",
    )(mem, g_mem.reshape(DEPTH, 1, D_MODEL), w_mem_kv)


def _in_proj_kernel(h_ref, w_ref, o_ref, wbf_ref, *, n_act):
    n = pl.program_id(0)

    @pl.when(pl.program_id(1) == 0)
    def _():
        wbf_ref[...] = w_ref[...].astype(BF16)

    acc = jnp.dot(h_ref[0], wbf_ref[...], preferred_element_type=F32)
    if n_act == 0:
        o_ref[0] = acc
    else:
        @pl.when(n < n_act)
        def _():
            o_ref[0] = jax.nn.gelu(acc)

        @pl.when(n >= n_act)
        def _():
            o_ref[0] = acc


def _in_proj(h, w, n_act):
    width = w.shape[1]
    tn = 512
    return pl.pallas_call(
        functools.partial(_in_proj_kernel, n_act=n_act),
        grid=(width // tn, N_TILES),
        in_specs=[
            pl.BlockSpec((1, TILE, D_MODEL), lambda n, m: (m, 0, 0)),
            pl.BlockSpec((D_MODEL, tn), lambda n, m: (0, n)),
        ],
        out_specs=pl.BlockSpec((1, TILE, tn), lambda n, m: (m, 0, n)),
        out_shape=jax.ShapeDtypeStruct((N_TILES, TILE, width), F32),
        scratch_shapes=[pltpu.VMEM((D_MODEL, tn), BF16)],
        compiler_params=_cparams(2, 40),
        name="in_proj",
    )(h, w)


def _sg_gate_kernel(u_ref, v_ref, gv_ref, ws_ref, bias_ref, coef_ref, tok_ref, sgv_ref, vn_ref):
    vn_ref[...] = _rms(v_ref[0], gv_ref[...])
    row = lax.broadcasted_iota(jnp.int32, (CHUNK, CHUNK), 0)
    col = lax.broadcasted_iota(jnp.int32, (CHUNK, CHUNK), 1)
    causal = col <= row
    wms = [jnp.where(causal, ws_ref[g], 0.0).astype(BF16) for g in range(SG_GROUPS)]

    def chunk(c, carry):
        r0 = pl.multiple_of(c * CHUNK, CHUNK)
        for g in range(SG_GROUPS):
            cols = slice(g * LANE, (g + 1) * LANE)
            blk = vn_ref[pl.ds(r0, CHUNK), cols].astype(BF16)
            s = jnp.dot(wms[g], blk, preferred_element_type=F32) + bias_ref[:, cols]
            tok_ref[0, pl.ds(r0, CHUNK), cols] = (u_ref[0, pl.ds(r0, CHUNK), cols] * s).astype(BF16)
        return carry

    lax.fori_loop(0, P_ROWS // CHUNK, chunk, 0)
    vs = vn_ref[P_ROWS:, :]
    sgv_ref[0] = vs
    s = coef_ref[...] * vs + bias_ref[0:1, :]
    tok_ref[0, P_ROWS:, :] = (u_ref[0, P_ROWS:, :] * s).astype(BF16)


def _sg_gate(z, g_v, w_s, b_s):
    bias = jnp.repeat(b_s.T, LANE, axis=1)
    coef = jnp.repeat(w_s[:, 0, 0], LANE).reshape(1, TOK_WIDTH)
    return pl.pallas_call(
        _sg_gate_kernel,
        grid=(N_TILES,),
        in_specs=[
            pl.BlockSpec((1, TILE, TOK_WIDTH), lambda j: (j, 0, 0)),
            pl.BlockSpec((1, TILE, TOK_WIDTH), lambda j: (j, 0, 1)),
            pl.BlockSpec((1, TOK_WIDTH), lambda j: (0, 0)),
            pl.BlockSpec((SG_GROUPS, CHUNK, CHUNK), lambda j: (0, 0, 0)),
            pl.BlockSpec((CHUNK, TOK_WIDTH), lambda j: (0, 0)),
            pl.BlockSpec((1, TOK_WIDTH), lambda j: (0, 0)),
        ],
        out_specs=[
            pl.BlockSpec((1, TILE, TOK_WIDTH), lambda j: (j, 0, 0)),
            pl.BlockSpec((1, S_ROWS, TOK_WIDTH), lambda j: (j, 0, 0)),
        ],
        out_shape=[
            jax.ShapeDtypeStruct((N_TILES, TILE, TOK_WIDTH), BF16),
            jax.ShapeDtypeStruct((N_TILES, S_ROWS, TOK_WIDTH), F32),
        ],
        scratch_shapes=[pltpu.VMEM((TILE, TOK_WIDTH), F32)],
        compiler_params=_cparams(1, 48),
        name="sg_gate",
    )(z, z, g_v.reshape(1, TOK_WIDTH), w_s, bias, coef)


def _xattn_kernel(q_ref, mk_ref, mv_ref, ck_ref, cv_ref, o_ref):
    scale = XA_HEAD_DIM ** -0.5
    for h in range(XA_HEADS):
        cols = slice(h * XA_HEAD_DIM, (h + 1) * XA_HEAD_DIM)
        qh = q_ref[0, :P_ROWS, cols].astype(BF16)
        kh = mk_ref[:, cols].astype(BF16)
        vh = mv_ref[:, cols].astype(BF16)
        s = lax.dot_general(qh, kh, (((1,), (1,)), ((), ())), preferred_element_type=F32) * scale
        s = s - jnp.max(s, axis=-1, keepdims=True)
        e = jnp.exp(s)
        p = (e / jnp.sum(e, axis=-1, keepdims=True)).astype(BF16)
        o_ref[0, :P_ROWS, cols] = jnp.dot(p, vh, preferred_element_type=F32).astype(BF16)
        qs = q_ref[0, P_ROWS:, cols]
        ks = ck_ref[:, :, cols]
        vs = cv_ref[:, :, cols]
        ss = jnp.sum(ks * qs[:, None, :], axis=-1, keepdims=True) * scale
        ss = ss - jnp.max(ss, axis=1, keepdims=True)
        es = jnp.exp(ss)
        ps = es / jnp.sum(es, axis=1, keepdims=True)
        o_ref[0, P_ROWS:, cols] = jnp.sum(ps * vs, axis=1).astype(BF16)


def _xattn(z, q_blk, memkv, layer, cache_k, cache_v):
    ck = cache_k.reshape(DEPTH, DEC_BATCH, N_MEM, XA_WIDTH)
    cv = cache_v.reshape(DEPTH, DEC_BATCH, N_MEM, XA_WIDTH)
    return pl.pallas_call(
        _xattn_kernel,
        grid=(N_TILES,),
        in_specs=[
            pl.BlockSpec((1, TILE, XA_WIDTH), lambda j: (j, 0, q_blk)),
            pl.BlockSpec((None, None, N_MEM, XA_WIDTH), lambda j: (layer, 0, j // 2, 0)),
            pl.BlockSpec((None, None, N_MEM, XA_WIDTH), lambda j: (layer, 1, j // 2, 0)),
            pl.BlockSpec((None, S_ROWS, N_MEM, XA_WIDTH), lambda j: (layer, j, 0, 0)),
            pl.BlockSpec((None, S_ROWS, N_MEM, XA_WIDTH), lambda j: (layer, j, 0, 0)),
        ],
        out_specs=pl.BlockSpec((1, TILE, XA_WIDTH), lambda j: (j, 0, 0)),
        out_shape=jax.ShapeDtypeStruct((N_TILES, TILE, XA_WIDTH), BF16),
        compiler_params=_cparams(1, 52),
        name="xattn",
    )(z, memkv, memkv, ck, cv)


def _s5_kernel(u_ref, bm_ref, cm_ref, lbr_ref, lbi_ref, d_ref, y_ref, hre_ref, him_ref, xs_ref, hs_ref):
    tq = pl.program_id(1)

    @pl.when(tq == 0)
    def _():
        hs_ref[...] = jnp.zeros_like(hs_ref)

    bm = bm_ref[0]
    for b in range(BATCH):
        x = jnp.dot(u_ref[b, 0].astype(BF16), bm, preferred_element_type=F32)
        for s in range(2 * S5_SLABS):
            xs_ref[s, b * S5_PITCH:b * S5_PITCH + S5_TQ, :] = x[:, s * LANE:(s + 1) * LANE]

    lbr = [lbr_ref[0, p:p + 1, :] for p in range(S5_SLABS)]
    lbi = [lbi_ref[0, p:p + 1, :] for p in range(S5_SLABS)]

    def step(t, carry):
        new = []
        for p in range(S5_SLABS):
            hr, hi = carry[2 * p], carry[2 * p + 1]
            xr = xs_ref[p, pl.ds(t, BATCH, stride=S5_PITCH), :]
            xi = xs_ref[S5_SLABS + p, pl.ds(t, BATCH, stride=S5_PITCH), :]
            nr = lbr[p] * hr - lbi[p] * hi + xr
            ni = lbr[p] * hi + lbi[p] * hr + xi
            xs_ref[p, pl.ds(t, BATCH, stride=S5_PITCH), :] = nr
            xs_ref[S5_SLABS + p, pl.ds(t, BATCH, stride=S5_PITCH), :] = ni
            new += [nr, ni]
        return tuple(new)

    init = []
    for p in range(S5_SLABS):
        init += [hs_ref[p, :BATCH, :], hs_ref[S5_SLABS + p, :BATCH, :]]
    fin = lax.fori_loop(0, S5_TQ, step, tuple(init), unroll=2)
    for p in range(S5_SLABS):
        hs_ref[p, :BATCH, :] = fin[2 * p]
        hs_ref[S5_SLABS + p, :BATCH, :] = fin[2 * p + 1]
        hre_ref[:, p * LANE:(p + 1) * LANE] = fin[2 * p]
        him_ref[:, p * LANE:(p + 1) * LANE] = fin[2 * p + 1]

    cm = cm_ref[0]
    d = d_ref[...]
    for b in range(BATCH):
        hcat = jnp.concatenate(
            [xs_ref[s, b * S5_PITCH:b * S5_PITCH + S5_TQ, :].astype(BF16) for s in range(2 * S5_SLABS)], axis=1)
        y = jnp.dot(hcat, cm, preferred_element_type=F32) + d * u_ref[b, 0]
        y_ref[b, 0] = jax.nn.gelu(y)


def _s5_prompt(z, prm):
    z4 = z.reshape(BATCH, 2, TILE, D_MODEL)
    blk = (BATCH, 1, S5_TQ, S5_CB)
    y, hre, him = pl.pallas_call(
        _s5_kernel,
        grid=(S5_NCB, SEQ // S5_TQ),
        in_specs=[
            pl.BlockSpec(blk, lambda c, t: (0, t // 2, t % 2, c)),
            pl.BlockSpec((1, S5_CB, 2 * S5_ST), lambda c, t: (c, 0, 0)),
            pl.BlockSpec((1, 2 * S5_ST, S5_CB), lambda c, t: (c, 0, 0)),
            pl.BlockSpec((1, S5_SLABS, LANE), lambda c, t: (c, 0, 0)),
            pl.BlockSpec((1, S5_SLABS, LANE), lambda c, t: (c, 0, 0)),
            pl.BlockSpec((1, S5_CB), lambda c, t: (0, c)),
        ],
        out_specs=[
            pl.BlockSpec(blk, lambda c, t: (0, t // 2, t % 2, c)),
            pl.BlockSpec((BATCH, S5_ST), lambda c, t: (0, c)),
            pl.BlockSpec((BATCH, S5_ST), lambda c, t: (0, c)),
        ],
        out_shape=[
            jax.ShapeDtypeStruct((BATCH, 2, TILE, TOK_WIDTH), F32),
            jax.ShapeDtypeStruct((BATCH, SSM_GROUPS * SSM_STATE), F32),
            jax.ShapeDtypeStruct((BATCH, SSM_GROUPS * SSM_STATE), F32),
        ],
        scratch_shapes=[
            pltpu.VMEM((2 * S5_SLABS, BATCH * S5_PITCH, LANE), F32),
            pltpu.VMEM((2 * S5_SLABS, 8, LANE), F32),
        ],
        compiler_params=_cparams(2, 48),
        name="s5_prompt",
    )(z4, prm["bm_hi"], prm["cm"], prm["lbr"].reshape(S5_NCB, S5_SLABS, LANE),
      prm["lbi"].reshape(S5_NCB, S5_SLABS, LANE), prm["d"])
    return y.reshape(N_TILES, TILE, TOK_WIDTH), hre, him


def _s5_sample_kernel(u_ref, sre_ref, sim_ref, bh_ref, bl_ref, cm_ref, lbr_ref, lbi_ref, d_ref, yin_ref,
                      y_ref, nre_ref, nim_ref):
    del yin_ref
    u = u_ref[0]
    uh = u.astype(BF16)
    ul = (u - uh.astype(F32)).astype(BF16)
    bh, bl = bh_ref[0], bl_ref[0]
    x = (jnp.dot(ul, bh, preferred_element_type=F32) + jnp.dot(uh, bl, preferred_element_type=F32)
         + jnp.dot(uh, bh, preferred_element_type=F32))
    lbr, lbi = lbr_ref[...], lbi_ref[...]
    sr, si = sre_ref[...], sim_ref[...]
    nr = lbr * sr - lbi * si + x[:, :S5_ST]
    ni = lbr * si + lbi * sr + x[:, S5_ST:]
    nre_ref[...] = nr
    nim_ref[...] = ni
    hcat = jnp.concatenate([nr.astype(BF16), ni.astype(BF16)], axis=1)
    y = jnp.dot(hcat, cm_ref[0], preferred_element_type=F32) + d_ref[...] * u
    y_ref[0] = jax.nn.gelu(y)


def _s5_sample(z, y_all, s_re, s_im, prm):
    nst = SSM_GROUPS * SSM_STATE
    y, nre, nim = pl.pallas_call(
        _s5_sample_kernel,
        grid=(S5_NCB, N_TILES),
        in_specs=[
            pl.BlockSpec((1, S_ROWS, S5_CB), lambda c, j: (j, S_BLK, c)),
            pl.BlockSpec((S_ROWS, S5_ST), lambda c, j: (j, c)),
            pl.BlockSpec((S_ROWS, S5_ST), lambda c, j: (j, c)),
            pl.BlockSpec((1, S5_CB, 2 * S5_ST), lambda c, j: (c, 0, 0)),
            pl.BlockSpec((1, S5_CB, 2 * S5_ST), lambda c, j: (c, 0, 0)),
            pl.BlockSpec((1, 2 * S5_ST, S5_CB), lambda c, j: (c, 0, 0)),
            pl.BlockSpec((1, S5_ST), lambda c, j: (0, c)),
            pl.BlockSpec((1, S5_ST), lambda c, j: (0, c)),
            pl.BlockSpec((1, S5_CB), lambda c, j: (0, c)),
            pl.BlockSpec(memory_space=pl.ANY),
        ],
        out_specs=[
            pl.BlockSpec((1, S_ROWS, S5_CB), lambda c, j: (j, S_BLK, c)),
            pl.BlockSpec((S_ROWS, S5_ST), lambda c, j: (j, c)),
            pl.BlockSpec((S_ROWS, S5_ST), lambda c, j: (j, c)),
        ],
        out_shape=[
            jax.ShapeDtypeStruct((N_TILES, TILE, TOK_WIDTH), F32),
            jax.ShapeDtypeStruct((DEC_BATCH, nst), F32),
            jax.ShapeDtypeStruct((DEC_BATCH, nst), F32),
        ],
        input_output_aliases={9: 0},
        compiler_params=_cparams(2, 32),
        name="s5_sample",
    )(z, s_re.reshape(DEC_BATCH, nst), s_im.reshape(DEC_BATCH, nst), prm["bm_hi"], prm["bm_lo"], prm["cm"],
      prm["lbr"].reshape(1, nst), prm["lbi"].reshape(1, nst), prm["d"], y_all)
    return y, nre, nim


def _s5_params(lam_re, lam_im, log_dt, b_re, b_im, c_re, c_im, d):
    dt = jnp.exp(log_dt)[:, None]
    ar, ai = lam_re * dt, lam_im * dt
    mag = jnp.exp(ar)
    lb_re, lb_im = mag * jnp.cos(ai), mag * jnp.sin(ai)
    nr, ni = lb_re - 1.0, lb_im
    den = lam_re * lam_re + lam_im * lam_im
    k_re = (nr * lam_re + ni * lam_im) / den
    k_im = (ni * lam_re - nr * lam_im) / den
    bb_re = k_re[..., None] * b_re - k_im[..., None] * b_im
    bb_im = k_re[..., None] * b_im + k_im[..., None] * b_re
    gpb = S5_CB // SSM_GROUP_DIM
    eye = jnp.eye(gpb, dtype=F32)

    def blockdiag_in(m):
        m = m.reshape(S5_NCB, gpb, SSM_STATE, SSM_GROUP_DIM)
        return jnp.einsum("ngpc,gh->ngchp", m, eye).reshape(S5_NCB, S5_CB, S5_ST)

    def blockdiag_out(m):
        m = m.reshape(S5_NCB, gpb, SSM_GROUP_DIM, SSM_STATE)
        return jnp.einsum("ngcp,gh->ngphc", m, eye).reshape(S5_NCB, S5_ST, S5_CB)

    bm = jnp.concatenate([blockdiag_in(bb_re), blockdiag_in(bb_im)], axis=2)
    bm_hi = bm.astype(BF16)
    bm_lo = (bm - bm_hi.astype(F32)).astype(BF16)
    cm = jnp.concatenate([blockdiag_out(c_re), -blockdiag_out(c_im)], axis=1).astype(BF16)
    return dict(bm_hi=bm_hi, bm_lo=bm_lo, cm=cm, lbr=lb_re, lbi=lb_im, d=d.reshape(1, TOK_WIDTH))


def _glu_kernel(y_ref, w_ref, b_ref, o_ref, wbf_ref, *, tn):
    n = pl.program_id(0)

    @pl.when(pl.program_id(1) == 0)
    def _():
        wbf_ref[...] = w_ref[...].astype(BF16)

    gate = jnp.dot(y_ref[0].astype(BF16), wbf_ref[...], preferred_element_type=F32) + b_ref[...]
    for c in range(TOK_WIDTH // tn):
        @pl.when(n == c)
        def _(c=c):
            o_ref[0] = (y_ref[0, :, c * tn:(c + 1) * tn] * jax.nn.sigmoid(gate)).astype(BF16)


def _glu(y, w, b):
    tn = 512
    return pl.pallas_call(
        functools.partial(_glu_kernel, tn=tn),
        grid=(TOK_WIDTH // tn, N_TILES),
        in_specs=[
            pl.BlockSpec((1, TILE, TOK_WIDTH), lambda n, m: (m, 0, 0)),
            pl.BlockSpec((TOK_WIDTH, tn), lambda n, m: (0, n)),
            pl.BlockSpec((1, tn), lambda n, m: (0, n)),
        ],
        out_specs=pl.BlockSpec((1, TILE, tn), lambda n, m: (m, 0, n)),
        out_shape=jax.ShapeDtypeStruct((N_TILES, TILE, TOK_WIDTH), BF16),
        scratch_shapes=[pltpu.VMEM((TOK_WIDTH, tn), BF16)],
        compiler_params=_cparams(2, 40),
        name="glu",
    )(y, w, b.reshape(1, TOK_WIDTH))


def _proj_kernel(*refs, part_steps, k_total, final):
    n_parts = len(part_steps)
    parts = refs[:n_parts]
    w_ref, x_ref, g_ref = refs[n_parts:n_parts + 3]
    outs = refs[n_parts + 3:]
    if final:
        yp_ref, ys_ref, acc_ref = outs
        acc = acc_ref
    else:
        xn_ref, h_ref = outs
        acc = xn_ref.at[0]
    k = pl.program_id(1)
    n_k = sum(part_steps)
    n_xc = D_MODEL // TK

    @pl.when(k == 0)
    def _():
        acc[...] = jnp.zeros(acc.shape, F32)

    w = w_ref[...]
    if k_total % TK:
        rows = lax.broadcasted_iota(jnp.int32, w.shape, 0)
        w = jnp.where(rows < k_total - k * TK, w, 0.0)
    wbf = w.astype(BF16)
    start = 0
    for p_ref, cnt in zip(parts, part_steps):
        @pl.when((k >= start) & (k < start + cnt))
        def _(p_ref=p_ref):
            acc[...] += jnp.dot(p_ref[0], wbf, preferred_element_type=F32)
        start += cnt
    for c in range(n_xc):
        @pl.when(k == c)
        def _(c=c):
            acc[:, c * TK:(c + 1) * TK] += x_ref[0]

    @pl.when(k == n_k - 1)
    def _():
        h = _rms(acc[...], g_ref[...])
        if final:
            yp_ref[0] = h[:P_ROWS]
            ys_ref[0] = h[P_ROWS:]
        else:
            h_ref[0] = h.astype(BF16)


def _proj(parts, w, x, g, final=False):
    k_total = w.shape[0]
    part_steps = tuple(p.shape[-1] // TK for p in parts)
    n_k = sum(part_steps)
    assert n_k == pl.cdiv(k_total, TK) and n_k >= D_MODEL // TK
    starts = [sum(part_steps[:i]) for i in range(len(parts))]
    in_specs = []
    for s0, cnt in zip(starts, part_steps):
        in_specs.append(pl.BlockSpec(
            (1, TILE, TK), lambda m, k, s0=s0, cnt=cnt: (m, 0, jnp.clip(k - s0, 0, cnt - 1))))
    in_specs += [
        pl.BlockSpec((TK, D_MODEL), lambda m, k: (k, 0)),
        pl.BlockSpec((1, TILE, TK), lambda m, k: (m, 0, jnp.minimum(k, D_MODEL // TK - 1))),
        pl.BlockSpec((1, D_MODEL), lambda m, k: (0, 0)),
    ]
    if final:
        out_specs = [
            pl.BlockSpec((1, P_ROWS, D_MODEL), lambda m, k: (m, 0, 0)),
            pl.BlockSpec((1, S_ROWS, D_MODEL), lambda m, k: (m, 0, 0)),
        ]
        out_shape = [
            jax.ShapeDtypeStruct((N_TILES, P_ROWS, D_MODEL), F32),
            jax.ShapeDtypeStruct((N_TILES, S_ROWS, D_MODEL), F32),
        ]
        scratch = [pltpu.VMEM((TILE, D_MODEL), F32)]
    else:
        out_specs = [
            pl.BlockSpec((1, TILE, D_MODEL), lambda m, k: (m, 0, 0)),
            pl.BlockSpec((1, TILE, D_MODEL), lambda m, k: (m, 0, 0)),
        ]
        out_shape = [
            jax.ShapeDtypeStruct((N_TILES, TILE, D_MODEL), F32),
            jax.ShapeDtypeStruct((N_TILES, TILE, D_MODEL), BF16),
        ]
        scratch = []
    return pl.pallas_call(
        functools.partial(_proj_kernel, part_steps=part_steps, k_total=k_total, final=final),
        grid=(N_TILES, n_k),
        in_specs=in_specs,
        out_specs=out_specs,
        out_shape=out_shape,
        scratch_shapes=scratch,
        compiler_params=_cparams(2, 52),
        name="proj_final" if final else "proj",
    )(*parts, w, x, g.reshape(1, D_MODEL))


def _ffn_up_kernel(h_ref, wa_ref, wg_ref, cw_ref, cb_ref, p0_ref, p1_ref, y_ref, cp_ref, cs0_ref, cs1_ref,
                   wbf_ref):
    f = pl.program_id(0)

    @pl.when(f == FF_BLOCKS)
    def _():
        y_ref[...] = jnp.zeros(y_ref.shape, BF16)

    @pl.when(f < FF_BLOCKS)
    def _():
        wbf_ref[:, :LANE] = wa_ref[...].astype(BF16)
        wbf_ref[:, LANE:] = wg_ref[...].astype(BF16)
        w0, w1, w2 = cw_ref[0:1, :], cw_ref[1:2, :], cw_ref[2:3, :]
        cb = cb_ref[...]
        row = lax.broadcasted_iota(jnp.int32, (P_ROWS, LANE), 0)
        tail = None
        for j in range(N_TILES):
            r = jnp.dot(h_ref[j], wbf_ref[...], preferred_element_type=F32)
            a, g = r[:, :LANE], r[:, LANE:]
            ap = a[:P_ROWS]
            if j % 2 == 0:
                m1 = jnp.zeros((1, LANE), F32)
                m2 = jnp.zeros((1, LANE), F32)
            else:
                m2, m1 = tail[0:1], tail[1:2]
            a1 = jnp.where(row >= 1, pltpu.roll(ap, 1, 0), m1)
            a2 = jnp.where(row >= 2, pltpu.roll(ap, 2, 0), jnp.where(row == 0, m2, m1))
            c = cb + w0 * a2 + w1 * a1 + w2 * ap
            y_ref[j, :P_ROWS, :] = (jax.nn.silu(c) * g[:P_ROWS]).astype(BF16)
            tail = ap[P_ROWS - 2:]
            if j % 2 == 1:
                cp_ref[j // 2] = tail
            a_s = a[P_ROWS:]
            q0 = p0_ref[j * S_ROWS:(j + 1) * S_ROWS, :]
            q1 = p1_ref[j * S_ROWS:(j + 1) * S_ROWS, :]
            cs = cb + w0 * q0 + w1 * q1 + w2 * a_s
            y_ref[j, P_ROWS:, :] = (jax.nn.silu(cs) * g[P_ROWS:]).astype(BF16)
            cs0_ref[j * S_ROWS:(j + 1) * S_ROWS, :] = q1
            cs1_ref[j * S_ROWS:(j + 1) * S_ROWS, :] = a_s


def _ffn_up(h, w_up, conv_w, conv_b, prev):
    prev2 = prev.reshape(DEC_BATCH, 2 * D_FF)
    last = FF_BLOCKS - 1
    fc = lambda f: jnp.minimum(f, last)
    return pl.pallas_call(
        _ffn_up_kernel,
        grid=(FF_BLOCKS + 1,),
        in_specs=[
            pl.BlockSpec((N_TILES, TILE, D_MODEL), lambda f: (0, 0, 0), pipeline_mode=pl.Buffered(1)),
            pl.BlockSpec((D_MODEL, LANE), lambda f: (0, fc(f))),
            pl.BlockSpec((D_MODEL, LANE), lambda f: (0, FF_BLOCKS + fc(f))),
            pl.BlockSpec((3, LANE), lambda f: (0, fc(f))),
            pl.BlockSpec((1, LANE), lambda f: (0, fc(f))),
            pl.BlockSpec((DEC_BATCH, LANE), lambda f: (0, fc(f))),
            pl.BlockSpec((DEC_BATCH, LANE), lambda f: (0, FF_BLOCKS + fc(f))),
        ],
        out_specs=[
            pl.BlockSpec((N_TILES, TILE, LANE), lambda f: (0, 0, f)),
            pl.BlockSpec((BATCH, 2, LANE), lambda f: (0, 0, fc(f))),
            pl.BlockSpec((DEC_BATCH, LANE), lambda f: (0, fc(f))),
            pl.BlockSpec((DEC_BATCH, LANE), lambda f: (0, fc(f))),
        ],
        out_shape=[
            jax.ShapeDtypeStruct((N_TILES, TILE, FF_PAD), BF16),
            jax.ShapeDtypeStruct((BATCH, 2, D_FF), F32),
            jax.ShapeDtypeStruct((DEC_BATCH, D_FF), F32),
            jax.ShapeDtypeStruct((DEC_BATCH, D_FF), F32),
        ],
        scratch_shapes=[pltpu.VMEM((D_MODEL, 2 * LANE), BF16)],
        compiler_params=_cparams(1, 56),
        name="ffn_up",
    )(h, w_up, w_up, conv_w, conv_b.reshape(1, D_FF), prev2, prev2)


def kernel(x_prompt, x_sample, mem_prompt, cache_mem_k, cache_mem_v, state_ssm_re, state_ssm_im, state_conv,
           g_mix, g_ffn, g_mem, g_final, w_mem_kv, sg_w_in, sg_w_out, sg_g_v, sg_w_s, sg_b_s, ssm_w_in,
           ssm_w_out, ssm_lam_re, ssm_lam_im, ssm_log_dt, ssm_b_re, ssm_b_im, ssm_c_re, ssm_c_im, ssm_d,
           ssm_w_glu, ssm_b_glu, ffn_w_up, ffn_conv_w, ffn_conv_b, ffn_w_down):
    memkv = _mem_kv(mem_prompt, g_mem, w_mem_kv)
    x, h = _prep(x_prompt, x_sample, g_mix[0])
    sg_v, re_p, im_p, re_s, im_s, conv_p, conv_s = [], [], [], [], [], [], []
    y_prompt = y_sample = None
    for i in range(DEPTH):
        j = i // 2
        if i % 2 == 0:
            z = _in_proj(h, sg_w_in[j], n_act=2 * TOK_WIDTH // 512)
            tok, v = _sg_gate(z, sg_g_v[j], sg_w_s[j], sg_b_s[j])
            sg_v.append(v.reshape(DEC_BATCH, 1, TOK_WIDTH))
            xa = _xattn(z, 2 * TOK_WIDTH // XA_WIDTH, memkv, i, cache_mem_k, cache_mem_v)
            x, h = _proj([tok, xa], sg_w_out[j], x, g_ffn[i])
        else:
            prm = _s5_params(ssm_lam_re[j], ssm_lam_im[j], ssm_log_dt[j], ssm_b_re[j], ssm_b_im[j],
                             ssm_c_re[j], ssm_c_im[j], ssm_d[j])
            z = _in_proj(h, ssm_w_in[j], n_act=0)
            y, hre, him = _s5_prompt(z, prm)
            y, nre, nim = _s5_sample(z, y, state_ssm_re[j], state_ssm_im[j], prm)
            re_p.append(hre.reshape(BATCH, SSM_GROUPS, SSM_STATE))
            im_p.append(him.reshape(BATCH, SSM_GROUPS, SSM_STATE))
            re_s.append(nre.reshape(DEC_BATCH, SSM_GROUPS, SSM_STATE))
            im_s.append(nim.reshape(DEC_BATCH, SSM_GROUPS, SSM_STATE))
            yg = _glu(y, ssm_w_glu[j], ssm_b_glu[j])
            xa = _xattn(z, TOK_WIDTH // XA_WIDTH, memkv, i, cache_mem_k, cache_mem_v)
            x, h = _proj([yg, xa], ssm_w_out[j], x, g_ffn[i])
        yf, cp, cs0, cs1 = _ffn_up(h, ffn_w_up[i], ffn_conv_w[i], ffn_conv_b[i], state_conv[i])
        conv_p.append(cp)
        conv_s.append(jnp.stack([cs0, cs1], axis=1))
        if i + 1 < DEPTH:
            x, h = _proj([yf], ffn_w_down[i], x, g_mix[i + 1])
        else:
            y_prompt, y_sample = _proj([yf], ffn_w_down[i], x, g_final, final=True)
    mem_k = memkv[:, 0].reshape(DEPTH, BATCH, N_MEM, XA_HEADS, XA_HEAD_DIM)
    mem_v = memkv[:, 1].reshape(DEPTH, BATCH, N_MEM, XA_HEADS, XA_HEAD_DIM)
    return (y_prompt.reshape(BATCH, SEQ, D_MODEL), y_sample.reshape(DEC_BATCH, 1, D_MODEL), mem_k, mem_v,
            jnp.stack(re_p), jnp.stack(im_p), jnp.stack(conv_p),
            jnp.stack(re_s), jnp.stack(im_s), jnp.stack(conv_s), jnp.stack(sg_v))
```

```python
import functools
import math

import jax
import jax.numpy as jnp
from jax import lax
from jax.experimental import pallas as pl
from jax.experimental.pallas import tpu as pltpu

F32 = jnp.float32
BF16 = jnp.bfloat16

D_MODEL = 2048
BATCH = 4
SEQ = 2048
DEPTH = 4
DEC_BATCH = 128
N_MEM = 256
XA_HEADS = 4
XA_HEAD_DIM = 128
XA_WIDTH = 512
TOK_WIDTH = 1536
CHUNK = 128
SG_GROUPS = 12
SSM_GROUPS = 96
SSM_GROUP_DIM = 16
SSM_STATE = 64
D_FF = 5504
EPS = 1e-6

N_TILES = 8
P_ROWS = 1024
S_ROWS = 16
TILE = P_ROWS + S_ROWS
S_BLK = P_ROWS // S_ROWS

LANE = 128
FF_BLOCKS = D_FF // LANE
FF_PAD = (FF_BLOCKS + 1) * LANE
TK = 512

S5_CB = 256
S5_NCB = TOK_WIDTH // S5_CB
S5_ST = (S5_CB // SSM_GROUP_DIM) * SSM_STATE
S5_SLABS = S5_ST // LANE
S5_TQ = 512
S5_PITCH = S5_TQ + 8
V7X_VMEM_LIMIT = 56 * 1024 * 1024


def _cparams(n_axes, vmem_mb=None):
    kw = dict(dimension_semantics=("arbitrary",) * n_axes)
    if vmem_mb is not None:
        kw["vmem_limit_bytes"] = min(int(vmem_mb * 1024 * 1024), V7X_VMEM_LIMIT)
    return pltpu.CompilerParams(**kw)


def _rms(x, g):
    return x * lax.rsqrt(jnp.mean(x * x, axis=-1, keepdims=True) + EPS) * g


def _prep_kernel(xp_ref, xs_ref, g_ref, x_ref, h_ref):
    xp = xp_ref[0]
    xs = xs_ref[0]
    g = g_ref[...]
    x_ref[0, :P_ROWS] = xp
    x_ref[0, P_ROWS:] = xs
    h_ref[0, :P_ROWS] = _rms(xp, g).astype(BF16)
    h_ref[0, P_ROWS:] = _rms(xs, g).astype(BF16)


def _prep(x_prompt, x_sample, g):
    xp = x_prompt.reshape(N_TILES, P_ROWS, D_MODEL)
    xs = x_sample.reshape(N_TILES, S_ROWS, D_MODEL)
    return pl.pallas_call(
        _prep_kernel,
        grid=(N_TILES,),
        in_specs=[
            pl.BlockSpec((1, P_ROWS, D_MODEL), lambda j: (j, 0, 0)),
            pl.BlockSpec((1, S_ROWS, D_MODEL), lambda j: (j, 0, 0)),
            pl.BlockSpec((1, D_MODEL), lambda j: (0, 0)),
        ],
        out_specs=[
            pl.BlockSpec((1, TILE, D_MODEL), lambda j: (j, 0, 0)),
            pl.BlockSpec((1, TILE, D_MODEL), lambda j: (j, 0, 0)),
        ],
        out_shape=[
            jax.ShapeDtypeStruct((N_TILES, TILE, D_MODEL), F32),
            jax.ShapeDtypeStruct((N_TILES, TILE, D_MODEL), BF16),
        ],
        compiler_params=_cparams(1, 52),
        name="prep",
    )(xp, xs, g.reshape(1, D_MODEL))


def _mem_kv_kernel(m_ref, g_ref, w_ref, k_ref, v_ref):
    h = _rms(m_ref[...], g_ref[0]).astype(BF16)
    r = jnp.dot(h, w_ref[0].astype(BF16), preferred_element_type=F32)

    @pl.when(pl.program_id(1) == 0)
    def _():
        k_ref[0] = r

    @pl.when(pl.program_id(1) == 1)
    def _():
        v_ref[0] = r


def _mem_kv(mem_prompt, g_mem, w_mem_kv):
    rows = BATCH * N_MEM
    mem = mem_prompt.reshape(rows, D_MODEL)
    return pl.pallas_call(
        _mem_kv_kernel,
        grid=(DEPTH, 2),
        in_specs=[
            pl.BlockSpec((rows, D_MODEL), lambda i, n: (0, 0)),
            pl.BlockSpec((1, 1, D_MODEL), lambda i, n: (i, 0, 0)),
            pl.BlockSpec((1, D_MODEL, XA_WIDTH), lambda i, n: (i, 0, n)),
        ],
        out_specs=[
            pl.BlockSpec((1, rows, XA_WIDTH), lambda i, n: (i, 0, 0)),
            pl.BlockSpec((1, rows, XA_WIDTH), lambda i, n: (i, 0, 0)),
        ],
        out_shape=[
            jax.ShapeDtypeStruct((DEPTH, rows, XA_WIDTH), F32),
            jax.ShapeDtypeStruct((DEPTH, rows, XA_WIDTH), F32),
        ],
        compiler_params=_cparams(2, 40),
        name="mem_kv",
    )(mem, g_mem.reshape(DEPTH, 1, D_MODEL), w_mem_kv)


def _in_proj_kernel(h_ref, w_ref, o_ref, wbf_ref, *, n_act):
    n = pl.program_id(0)

    @pl.when(pl.program_id(1) == 0)
    def _():
        wbf_ref[...] = w_ref[...].astype(BF16)

    acc = jnp.dot(h_ref[0], wbf_ref[...], preferred_element_type=F32)
    if n_act == 0:
        o_ref[0] = acc
    else:
        @pl.when(n < n_act)
        def _():
            o_ref[0] = jax.nn.gelu(acc)

        @pl.when(n >= n_act)
        def _():
            o_ref[0] = acc


def _in_proj(h, w, layer, n_act):
    width = w.shape[2]
    tn = 512
    return pl.pallas_call(
        functools.partial(_in_proj_kernel, n_act=n_act),
        grid=(width // tn, N_TILES),
        in_specs=[
            pl.BlockSpec((1, TILE, D_MODEL), lambda n, m: (m, 0, 0)),
            pl.BlockSpec((None, D_MODEL, tn), lambda n, m: (layer, 0, n)),
        ],
        out_specs=pl.BlockSpec((1, TILE, tn), lambda n, m: (m, 0, n)),
        out_shape=jax.ShapeDtypeStruct((N_TILES, TILE, width), F32),
        scratch_shapes=[pltpu.VMEM((D_MODEL, tn), BF16)],
        compiler_params=_cparams(2, 40),
        name="in_proj",
    )(h, w)


def _sg_gate_kernel(u_ref, v_ref, gv_ref, ws_ref, bias_ref, coef_ref, tok_ref, sgv_ref, vn_ref):
    vn_ref[...] = _rms(v_ref[0], gv_ref[...])
    row = lax.broadcasted_iota(jnp.int32, (CHUNK, CHUNK), 0)
    col = lax.broadcasted_iota(jnp.int32, (CHUNK, CHUNK), 1)
    causal = col <= row
    wms = [jnp.where(causal, ws_ref[g], 0.0).astype(BF16) for g in range(SG_GROUPS)]

    def chunk(c, carry):
        r0 = pl.multiple_of(c * CHUNK, CHUNK)
        for g in range(SG_GROUPS):
            cols = slice(g * LANE, (g + 1) * LANE)
            blk = vn_ref[pl.ds(r0, CHUNK), cols].astype(BF16)
            s = jnp.dot(wms[g], blk, preferred_element_type=F32) + bias_ref[:, cols]
            tok_ref[0, pl.ds(r0, CHUNK), cols] = (u_ref[0, pl.ds(r0, CHUNK), cols] * s).astype(BF16)
        return carry

    lax.fori_loop(0, P_ROWS // CHUNK, chunk, 0)
    vs = vn_ref[P_ROWS:, :]
    sgv_ref[0] = vs
    s = coef_ref[...] * vs + bias_ref[0:1, :]
    tok_ref[0, P_ROWS:, :] = (u_ref[0, P_ROWS:, :] * s).astype(BF16)


def _sg_gate(z, g_v, w_s, b_s, layer):
    bias = jnp.repeat(b_s.T, LANE, axis=1)
    coef = jnp.repeat(w_s[layer, :, 0, 0], LANE).reshape(1, TOK_WIDTH)
    return pl.pallas_call(
        _sg_gate_kernel,
        grid=(N_TILES,),
        in_specs=[
            pl.BlockSpec((1, TILE, TOK_WIDTH), lambda j: (j, 0, 0)),
            pl.BlockSpec((1, TILE, TOK_WIDTH), lambda j: (j, 0, 1)),
            pl.BlockSpec((1, TOK_WIDTH), lambda j: (0, 0)),
            pl.BlockSpec((None, SG_GROUPS, CHUNK, CHUNK), lambda j: (layer, 0, 0, 0)),
            pl.BlockSpec((CHUNK, TOK_WIDTH), lambda j: (0, 0)),
            pl.BlockSpec((1, TOK_WIDTH), lambda j: (0, 0)),
        ],
        out_specs=[
            pl.BlockSpec((1, TILE, TOK_WIDTH), lambda j: (j, 0, 0)),
            pl.BlockSpec((1, S_ROWS, TOK_WIDTH), lambda j: (j, 0, 0)),
        ],
        out_shape=[
            jax.ShapeDtypeStruct((N_TILES, TILE, TOK_WIDTH), BF16),
            jax.ShapeDtypeStruct((N_TILES, S_ROWS, TOK_WIDTH), F32),
        ],
        scratch_shapes=[pltpu.VMEM((TILE, TOK_WIDTH), F32)],
        compiler_params=_cparams(1, 48),
        name="sg_gate",
    )(z, z, g_v.reshape(1, TOK_WIDTH), w_s, bias, coef)


XA_SUB = 8


def _xattn_kernel(q_ref, mk_ref, mv_ref, ck_ref, cv_ref, o_ref, os_ref):
    scale = XA_HEAD_DIM ** -0.5
    sub = pl.program_id(1)

    @pl.when(sub == 0)
    def _():
        for h in range(XA_HEADS):
            cols = slice(h * XA_HEAD_DIM, (h + 1) * XA_HEAD_DIM)
            qh = q_ref[0, :P_ROWS, cols].astype(BF16)
            kh = mk_ref[:, cols].astype(BF16)
            vh = mv_ref[:, cols].astype(BF16)
            s = lax.dot_general(qh, kh, (((1,), (1,)), ((), ())), preferred_element_type=F32) * scale
            s = s - jnp.max(s, axis=-1, keepdims=True)
            e = jnp.exp(s)
            p = (e / jnp.sum(e, axis=-1, keepdims=True)).astype(BF16)
            o_ref[0, :P_ROWS, cols] = jnp.dot(p, vh, preferred_element_type=F32).astype(BF16)

    r0 = pl.multiple_of(P_ROWS + sub * XA_SUB, XA_SUB)
    o0 = pl.multiple_of(sub * XA_SUB, XA_SUB)
    for h in range(XA_HEADS):
        cols = slice(h * XA_HEAD_DIM, (h + 1) * XA_HEAD_DIM)
        qs = q_ref[0, pl.ds(r0, XA_SUB), cols]
        ks = ck_ref[:, :, h, :]
        vs = cv_ref[:, :, h, :]
        ss = jnp.sum(ks * qs[:, None, :], axis=-1, keepdims=True) * scale
        ss = ss - jnp.max(ss, axis=1, keepdims=True)
        es = jnp.exp(ss)
        ps = es / jnp.sum(es, axis=1, keepdims=True)
        os_ref[pl.ds(o0, XA_SUB), cols] = jnp.sum(ps * vs, axis=1)

    @pl.when(sub == S_ROWS // XA_SUB - 1)
    def _():
        o_ref[0, P_ROWS:, :] = os_ref[...].astype(BF16)


def _xattn(z, q_blk, mem_k, mem_v, layer, cache_k, cache_v):
    n_sub = S_ROWS // XA_SUB
    cache_blk = (None, XA_SUB, N_MEM, XA_HEADS, XA_HEAD_DIM)
    return pl.pallas_call(
        _xattn_kernel,
        grid=(N_TILES, n_sub),
        in_specs=[
            pl.BlockSpec((1, TILE, XA_WIDTH), lambda j, s: (j, 0, q_blk)),
            pl.BlockSpec((None, N_MEM, XA_WIDTH), lambda j, s: (layer, j // 2, 0)),
            pl.BlockSpec((None, N_MEM, XA_WIDTH), lambda j, s: (layer, j // 2, 0)),
            pl.BlockSpec(cache_blk, lambda j, s: (layer, j * n_sub + s, 0, 0, 0)),
            pl.BlockSpec(cache_blk, lambda j, s: (layer, j * n_sub + s, 0, 0, 0)),
        ],
        out_specs=pl.BlockSpec((1, TILE, XA_WIDTH), lambda j, s: (j, 0, 0)),
        out_shape=jax.ShapeDtypeStruct((N_TILES, TILE, XA_WIDTH), BF16),
        scratch_shapes=[pltpu.VMEM((S_ROWS, XA_WIDTH), F32)],
        compiler_params=_cparams(2, 52),
        name="xattn",
    )(z, mem_k, mem_v, cache_k, cache_v)


def _s5_kernel(u_ref, bm_ref, cm_ref, lbr_ref, lbi_ref, d_ref, y_ref, hre_ref, him_ref, xs_ref, hs_ref):
    tq = pl.program_id(1)

    @pl.when(tq == 0)
    def _():
        hs_ref[...] = jnp.zeros_like(hs_ref)

    bm = bm_ref[0]
    for b in range(BATCH):
        x = jnp.dot(u_ref[b, 0].astype(BF16), bm, preferred_element_type=F32)
        for s in range(2 * S5_SLABS):
            xs_ref[s, b * S5_PITCH:b * S5_PITCH + S5_TQ, :] = x[:, s * LANE:(s + 1) * LANE]

    lbr = [lbr_ref[0, p:p + 1, :] for p in range(S5_SLABS)]
    lbi = [lbi_ref[0, p:p + 1, :] for p in range(S5_SLABS)]

    def step(t, carry):
        new = []
        for p in range(S5_SLABS):
            hr, hi = carry[2 * p], carry[2 * p + 1]
            xr = xs_ref[p, pl.ds(t, BATCH, stride=S5_PITCH), :]
            xi = xs_ref[S5_SLABS + p, pl.ds(t, BATCH, stride=S5_PITCH), :]
            nr = lbr[p] * hr - lbi[p] * hi + xr
            ni = lbr[p] * hi + lbi[p] * hr + xi
            xs_ref[p, pl.ds(t, BATCH, stride=S5_PITCH), :] = nr
            xs_ref[S5_SLABS + p, pl.ds(t, BATCH, stride=S5_PITCH), :] = ni
            new += [nr, ni]
        return tuple(new)

    init = []
    for p in range(S5_SLABS):
        init += [hs_ref[p, :BATCH, :], hs_ref[S5_SLABS + p, :BATCH, :]]
    fin = lax.fori_loop(0, S5_TQ, step, tuple(init), unroll=2)
    for p in range(S5_SLABS):
        hs_ref[p, :BATCH, :] = fin[2 * p]
        hs_ref[S5_SLABS + p, :BATCH, :] = fin[2 * p + 1]
        hre_ref[:, p * LANE:(p + 1) * LANE] = fin[2 * p]
        him_ref[:, p * LANE:(p + 1) * LANE] = fin[2 * p + 1]

    cm = cm_ref[0]
    d = d_ref[...]
    for b in range(BATCH):
        hcat = jnp.concatenate(
            [xs_ref[s, b * S5_PITCH:b * S5_PITCH + S5_TQ, :].astype(BF16) for s in range(2 * S5_SLABS)], axis=1)
        y = jnp.dot(hcat, cm, preferred_element_type=F32) + d * u_ref[b, 0]
        y_ref[b, 0] = jax.nn.gelu(y)


def _s5_prompt(z, prm):
    z4 = z.reshape(BATCH, 2, TILE, D_MODEL)
    blk = (BATCH, 1, S5_TQ, S5_CB)
    y, hre, him = pl.pallas_call(
        _s5_kernel,
        grid=(S5_NCB, SEQ // S5_TQ),
        in_specs=[
            pl.BlockSpec(blk, lambda c, t: (0, t // 2, t % 2, c)),
            pl.BlockSpec((1, S5_CB, 2 * S5_ST), lambda c, t: (c, 0, 0)),
            pl.BlockSpec((1, 2 * S5_ST, S5_CB), lambda c, t: (c, 0, 0)),
            pl.BlockSpec((1, S5_SLABS, LANE), lambda c, t: (c, 0, 0)),
            pl.BlockSpec((1, S5_SLABS, LANE), lambda c, t: (c, 0, 0)),
            pl.BlockSpec((1, S5_CB), lambda c, t: (0, c)),
        ],
        out_specs=[
            pl.BlockSpec(blk, lambda c, t: (0, t // 2, t % 2, c)),
            pl.BlockSpec((BATCH, S5_ST), lambda c, t: (0, c)),
            pl.BlockSpec((BATCH, S5_ST), lambda c, t: (0, c)),
        ],
        out_shape=[
            jax.ShapeDtypeStruct((BATCH, 2, TILE, TOK_WIDTH), F32),
            jax.ShapeDtypeStruct((BATCH, SSM_GROUPS * SSM_STATE), F32),
            jax.ShapeDtypeStruct((BATCH, SSM_GROUPS * SSM_STATE), F32),
        ],
        scratch_shapes=[
            pltpu.VMEM((2 * S5_SLABS, BATCH * S5_PITCH, LANE), F32),
            pltpu.VMEM((2 * S5_SLABS, 8, LANE), F32),
        ],
        compiler_params=_cparams(2, 48),
        name="s5_prompt",
    )(z4, prm["bm_hi"], prm["cm"], prm["lbr"].reshape(S5_NCB, S5_SLABS, LANE),
      prm["lbi"].reshape(S5_NCB, S5_SLABS, LANE), prm["d"])
    return y.reshape(N_TILES, TILE, TOK_WIDTH), hre, him


def _s5_sample_kernel(u_ref, sre_ref, sim_ref, bh_ref, bl_ref, cm_ref, lbr_ref, lbi_ref, d_ref, yin_ref,
                      y_ref, nre_ref, nim_ref):
    del yin_ref
    u = u_ref[0]
    uh = u.astype(BF16)
    ul = (u - uh.astype(F32)).astype(BF16)
    bh, bl = bh_ref[0], bl_ref[0]
    x = (jnp.dot(ul, bh, preferred_element_type=F32) + jnp.dot(uh, bl, preferred_element_type=F32)
         + jnp.dot(uh, bh, preferred_element_type=F32))
    lbr, lbi = lbr_ref[...], lbi_ref[...]
    sr, si = sre_ref[...], sim_ref[...]
    nr = lbr * sr - lbi * si + x[:, :S5_ST]
    ni = lbr * si + lbi * sr + x[:, S5_ST:]
    nre_ref[...] = nr
    nim_ref[...] = ni
    hcat = jnp.concatenate([nr.astype(BF16), ni.astype(BF16)], axis=1)
    y = jnp.dot(hcat, cm_ref[0], preferred_element_type=F32) + d_ref[...] * u
    y_ref[0] = jax.nn.gelu(y)


def _s5_sample(z, y_all, s_re, s_im, prm):
    nst = SSM_GROUPS * SSM_STATE
    y, nre, nim = pl.pallas_call(
        _s5_sample_kernel,
        grid=(S5_NCB, N_TILES),
        in_specs=[
            pl.BlockSpec((1, S_ROWS, S5_CB), lambda c, j: (j, S_BLK, c)),
            pl.BlockSpec((S_ROWS, S5_ST), lambda c, j: (j, c)),
            pl.BlockSpec((S_ROWS, S5_ST), lambda c, j: (j, c)),
            pl.BlockSpec((1, S5_CB, 2 * S5_ST), lambda c, j: (c, 0, 0)),
            pl.BlockSpec((1, S5_CB, 2 * S5_ST), lambda c, j: (c, 0, 0)),
            pl.BlockSpec((1, 2 * S5_ST, S5_CB), lambda c, j: (c, 0, 0)),
            pl.BlockSpec((1, S5_ST), lambda c, j: (0, c)),
            pl.BlockSpec((1, S5_ST), lambda c, j: (0, c)),
            pl.BlockSpec((1, S5_CB), lambda c, j: (0, c)),
            pl.BlockSpec(memory_space=pl.ANY),
        ],
        out_specs=[
            pl.BlockSpec((1, S_ROWS, S5_CB), lambda c, j: (j, S_BLK, c)),
            pl.BlockSpec((S_ROWS, S5_ST), lambda c, j: (j, c)),
            pl.BlockSpec((S_ROWS, S5_ST), lambda c, j: (j, c)),
        ],
        out_shape=[
            jax.ShapeDtypeStruct((N_TILES, TILE, TOK_WIDTH), F32),
            jax.ShapeDtypeStruct((DEC_BATCH, nst), F32),
            jax.ShapeDtypeStruct((DEC_BATCH, nst), F32),
        ],
        input_output_aliases={9: 0},
        compiler_params=_cparams(2, 32),
        name="s5_sample",
    )(z, s_re.reshape(DEC_BATCH, nst), s_im.reshape(DEC_BATCH, nst), prm["bm_hi"], prm["bm_lo"], prm["cm"],
      prm["lbr"].reshape(1, nst), prm["lbi"].reshape(1, nst), prm["d"], y_all)
    return y, nre, nim


def _s5_params(lam_re, lam_im, log_dt, b_re, b_im, c_re, c_im, d):
    dt = jnp.exp(log_dt)[:, None]
    ar, ai = lam_re * dt, lam_im * dt
    mag = jnp.exp(ar)
    lb_re, lb_im = mag * jnp.cos(ai), mag * jnp.sin(ai)
    nr, ni = lb_re - 1.0, lb_im
    den = lam_re * lam_re + lam_im * lam_im
    k_re = (nr * lam_re + ni * lam_im) / den
    k_im = (ni * lam_re - nr * lam_im) / den
    bb_re = k_re[..., None] * b_re - k_im[..., None] * b_im
    bb_im = k_re[..., None] * b_im + k_im[..., None] * b_re
    gpb = S5_CB // SSM_GROUP_DIM
    eye = jnp.eye(gpb, dtype=F32)

    def blockdiag_in(m):
        m = m.reshape(S5_NCB, gpb, SSM_STATE, SSM_GROUP_DIM)
        return jnp.einsum("ngpc,gh->ngchp", m, eye).reshape(S5_NCB, S5_CB, S5_ST)

    def blockdiag_out(m):
        m = m.reshape(S5_NCB, gpb, SSM_GROUP_DIM, SSM_STATE)
        return jnp.einsum("ngcp,gh->ngphc", m, eye).reshape(S5_NCB, S5_ST, S5_CB)

    bm = jnp.concatenate([blockdiag_in(bb_re), blockdiag_in(bb_im)], axis=2)
    bm_hi = bm.astype(BF16)
    bm_lo = (bm - bm_hi.astype(F32)).astype(BF16)
    cm = jnp.concatenate([blockdiag_out(c_re), -blockdiag_out(c_im)], axis=1).astype(BF16)
    return dict(bm_hi=bm_hi, bm_lo=bm_lo, cm=cm, lbr=lb_re, lbi=lb_im, d=d.reshape(1, TOK_WIDTH))


def _glu_kernel(y_ref, w_ref, b_ref, o_ref, wbf_ref, *, tn):
    n = pl.program_id(0)

    @pl.when(pl.program_id(1) == 0)
    def _():
        wbf_ref[...] = w_ref[...].astype(BF16)

    gate = jnp.dot(y_ref[0].astype(BF16), wbf_ref[...], preferred_element_type=F32) + b_ref[...]
    for c in range(TOK_WIDTH // tn):
        @pl.when(n == c)
        def _(c=c):
            o_ref[0] = (y_ref[0, :, c * tn:(c + 1) * tn] * jax.nn.sigmoid(gate)).astype(BF16)


def _glu(y, w, b, layer):
    tn = 512
    return pl.pallas_call(
        functools.partial(_glu_kernel, tn=tn),
        grid=(TOK_WIDTH // tn, N_TILES),
        in_specs=[
            pl.BlockSpec((1, TILE, TOK_WIDTH), lambda n, m: (m, 0, 0)),
            pl.BlockSpec((None, TOK_WIDTH, tn), lambda n, m: (layer, 0, n)),
            pl.BlockSpec((1, tn), lambda n, m: (0, n)),
        ],
        out_specs=pl.BlockSpec((1, TILE, tn), lambda n, m: (m, 0, n)),
        out_shape=jax.ShapeDtypeStruct((N_TILES, TILE, TOK_WIDTH), BF16),
        scratch_shapes=[pltpu.VMEM((TOK_WIDTH, tn), BF16)],
        compiler_params=_cparams(2, 40),
        name="glu",
    )(y, w, b.reshape(1, TOK_WIDTH))


def _proj_kernel(*refs, part_steps, k_total, final):
    n_parts = len(part_steps)
    parts = refs[:n_parts]
    w_ref, x_ref, g_ref = refs[n_parts:n_parts + 3]
    outs = refs[n_parts + 3:]
    if final:
        yp_ref, ys_ref, acc_ref = outs
        acc = acc_ref
    else:
        xn_ref, h_ref = outs
        acc = xn_ref.at[0]
    k = pl.program_id(1)
    n_k = sum(part_steps)
    n_xc = D_MODEL // TK

    @pl.when(k == 0)
    def _():
        acc[...] = jnp.zeros(acc.shape, F32)

    w = w_ref[...]
    if k_total % TK:
        rows = lax.broadcasted_iota(jnp.int32, w.shape, 0)
        w = jnp.where(rows < k_total - k * TK, w, 0.0)
    wbf = w.astype(BF16)
    start = 0
    for p_ref, cnt in zip(parts, part_steps):
        @pl.when((k >= start) & (k < start + cnt))
        def _(p_ref=p_ref):
            acc[...] += jnp.dot(p_ref[0], wbf, preferred_element_type=F32)
        start += cnt
    for c in range(n_xc):
        @pl.when(k == c)
        def _(c=c):
            acc[:, c * TK:(c + 1) * TK] += x_ref[0]

    @pl.when(k == n_k - 1)
    def _():
        h = _rms(acc[...], g_ref[...])
        if final:
            yp_ref[0] = h[:P_ROWS]
            ys_ref[0] = h[P_ROWS:]
        else:
            h_ref[0] = h.astype(BF16)


def _proj(parts, w, layer, x, g, final=False):
    k_total = w.shape[1]
    part_steps = tuple(p.shape[-1] // TK for p in parts)
    n_k = sum(part_steps)
    assert n_k == pl.cdiv(k_total, TK) and n_k >= D_MODEL // TK
    starts = [sum(part_steps[:i]) for i in range(len(parts))]
    in_specs = []
    for s0, cnt in zip(starts, part_steps):
        in_specs.append(pl.BlockSpec(
            (1, TILE, TK), lambda m, k, s0=s0, cnt=cnt: (m, 0, jnp.clip(k - s0, 0, cnt - 1))))
    in_specs += [
        pl.BlockSpec((None, TK, D_MODEL), lambda m, k: (layer, k, 0)),
        pl.BlockSpec((1, TILE, TK), lambda m, k: (m, 0, jnp.minimum(k, D_MODEL // TK - 1))),
        pl.BlockSpec((1, D_MODEL), lambda m, k: (0, 0)),
    ]
    if final:
        out_specs = [
            pl.BlockSpec((1, P_ROWS, D_MODEL), lambda m, k: (m, 0, 0)),
            pl.BlockSpec((1, S_ROWS, D_MODEL), lambda m, k: (m, 0, 0)),
        ]
        out_shape = [
            jax.ShapeDtypeStruct((N_TILES, P_ROWS, D_MODEL), F32),
            jax.ShapeDtypeStruct((N_TILES, S_ROWS, D_MODEL), F32),
        ]
        scratch = [pltpu.VMEM((TILE, D_MODEL), F32)]
    else:
        out_specs = [
            pl.BlockSpec((1, TILE, D_MODEL), lambda m, k: (m, 0, 0)),
            pl.BlockSpec((1, TILE, D_MODEL), lambda m, k: (m, 0, 0)),
        ]
        out_shape = [
            jax.ShapeDtypeStruct((N_TILES, TILE, D_MODEL), F32),
            jax.ShapeDtypeStruct((N_TILES, TILE, D_MODEL), BF16),
        ]
        scratch = []
    return pl.pallas_call(
        functools.partial(_proj_kernel, part_steps=part_steps, k_total=k_total, final=final),
        grid=(N_TILES, n_k),
        in_specs=in_specs,
        out_specs=out_specs,
        out_shape=out_shape,
        scratch_shapes=scratch,
        compiler_params=_cparams(2, 52),
        name="proj_final" if final else "proj",
    )(*parts, w, x, g.reshape(1, D_MODEL))


def _ffn_up_kernel(h_ref, wa_ref, wg_ref, cw_ref, cb_ref, sc_ref, *rest):
    y_ref, cp_ref, cs_ref, wbf_ref = rest[-4:]
    f = pl.program_id(0)

    @pl.when(f == FF_BLOCKS)
    def _():
        y_ref[...] = jnp.zeros(y_ref.shape, BF16)

    @pl.when(f < FF_BLOCKS)
    def _():
        wbf_ref[:, :LANE] = wa_ref[...].astype(BF16)
        wbf_ref[:, LANE:] = wg_ref[...].astype(BF16)
        w0, w1, w2 = cw_ref[0:1, :], cw_ref[1:2, :], cw_ref[2:3, :]
        cb = cb_ref[...]
        row = lax.broadcasted_iota(jnp.int32, (P_ROWS, LANE), 0)
        tail = None
        for j in range(N_TILES):
            r = jnp.dot(h_ref[j], wbf_ref[...], preferred_element_type=F32)
            a, g = r[:, :LANE], r[:, LANE:]
            ap = a[:P_ROWS]
            if j % 2 == 0:
                m1 = jnp.zeros((1, LANE), F32)
                m2 = jnp.zeros((1, LANE), F32)
            else:
                m2, m1 = tail[0:1], tail[1:2]
            a1 = jnp.where(row >= 1, pltpu.roll(ap, 1, 0), m1)
            a2 = jnp.where(row >= 2, pltpu.roll(ap, 2, 0), jnp.where(row == 0, m2, m1))
            c = cb + w0 * a2 + w1 * a1 + w2 * ap
            y_ref[j, :P_ROWS, :] = (jax.nn.silu(c) * g[:P_ROWS]).astype(BF16)
            tail = ap[P_ROWS - 2:]
            if j % 2 == 1:
                cp_ref[j // 2] = tail
            a_s = a[P_ROWS:]
            q0 = sc_ref[j * S_ROWS:(j + 1) * S_ROWS, 0, :]
            q1 = sc_ref[j * S_ROWS:(j + 1) * S_ROWS, 1, :]
            cs = cb + w0 * q0 + w1 * q1 + w2 * a_s
            y_ref[j, P_ROWS:, :] = (jax.nn.silu(cs) * g[P_ROWS:]).astype(BF16)
            cs_ref[j * S_ROWS:(j + 1) * S_ROWS, 0, :] = q1
            cs_ref[j * S_ROWS:(j + 1) * S_ROWS, 1, :] = a_s


def _ffn_up(h, w_up, conv_w, conv_b, state_conv, layer, conv_p_prev, conv_s_prev):
    last = FF_BLOCKS - 1
    fc = lambda f: jnp.minimum(f, last)
    in_specs = [
        pl.BlockSpec((N_TILES, TILE, D_MODEL), lambda f: (0, 0, 0), pipeline_mode=pl.Buffered(1)),
        pl.BlockSpec((None, D_MODEL, LANE), lambda f: (layer, 0, fc(f))),
        pl.BlockSpec((None, D_MODEL, LANE), lambda f: (layer, 0, FF_BLOCKS + fc(f))),
        pl.BlockSpec((None, 3, LANE), lambda f: (layer, 0, fc(f))),
        pl.BlockSpec((None, 1, LANE), lambda f: (layer, 0, fc(f))),
        pl.BlockSpec((None, DEC_BATCH, 2, LANE), lambda f: (layer, 0, 0, fc(f))),
    ]
    args = [h, w_up, w_up, conv_w, conv_b.reshape(DEPTH, 1, D_FF), state_conv]
    aliases = {}
    if conv_p_prev is not None:
        in_specs += [pl.BlockSpec(memory_space=pl.ANY), pl.BlockSpec(memory_space=pl.ANY)]
        aliases = {len(args): 1, len(args) + 1: 2}
        args += [conv_p_prev, conv_s_prev]
    return pl.pallas_call(
        _ffn_up_kernel,
        grid=(FF_BLOCKS + 1,),
        in_specs=in_specs,
        out_specs=[
            pl.BlockSpec((N_TILES, TILE, LANE), lambda f: (0, 0, f)),
            pl.BlockSpec((None, BATCH, 2, LANE), lambda f: (layer, 0, 0, fc(f))),
            pl.BlockSpec((None, DEC_BATCH, 2, LANE), lambda f: (layer, 0, 0, fc(f))),
        ],
        out_shape=[
            jax.ShapeDtypeStruct((N_TILES, TILE, FF_PAD), BF16),
            jax.ShapeDtypeStruct((DEPTH, BATCH, 2, D_FF), F32),
            jax.ShapeDtypeStruct((DEPTH, DEC_BATCH, 2, D_FF), F32),
        ],
        scratch_shapes=[pltpu.VMEM((D_MODEL, 2 * LANE), BF16)],
        input_output_aliases=aliases,
        compiler_params=_cparams(1, 56),
        name="ffn_up",
    )(*args)


def kernel(x_prompt, x_sample, mem_prompt, cache_mem_k, cache_mem_v, state_ssm_re, state_ssm_im, state_conv,
           g_mix, g_ffn, g_mem, g_final, w_mem_kv, sg_w_in, sg_w_out, sg_g_v, sg_w_s, sg_b_s, ssm_w_in,
           ssm_w_out, ssm_lam_re, ssm_lam_im, ssm_log_dt, ssm_b_re, ssm_b_im, ssm_c_re, ssm_c_im, ssm_d,
           ssm_w_glu, ssm_b_glu, ffn_w_up, ffn_conv_w, ffn_conv_b, ffn_w_down):
    mem_k, mem_v = _mem_kv(mem_prompt, g_mem, w_mem_kv)
    x, h = _prep(x_prompt, x_sample, g_mix[0])
    sg_v, re_p, im_p, re_s, im_s = [], [], [], [], []
    conv_p = conv_s = None
    y_prompt = y_sample = None
    for i in range(DEPTH):
        j = i // 2
        if i % 2 == 0:
            z = _in_proj(h, sg_w_in, j, n_act=2 * TOK_WIDTH // 512)
            tok, v = _sg_gate(z, sg_g_v[j], sg_w_s, sg_b_s[j], j)
            sg_v.append(v.reshape(DEC_BATCH, 1, TOK_WIDTH))
            xa = _xattn(z, 2 * TOK_WIDTH // XA_WIDTH, mem_k, mem_v, i, cache_mem_k, cache_mem_v)
            x, h = _proj([tok, xa], sg_w_out, j, x, g_ffn[i])
        else:
            prm = _s5_params(ssm_lam_re[j], ssm_lam_im[j], ssm_log_dt[j], ssm_b_re[j], ssm_b_im[j],
                             ssm_c_re[j], ssm_c_im[j], ssm_d[j])
            z = _in_proj(h, ssm_w_in, j, n_act=0)
            y, hre, him = _s5_prompt(z, prm)
            y, nre, nim = _s5_sample(z, y, state_ssm_re[j], state_ssm_im[j], prm)
            re_p.append(hre.reshape(BATCH, SSM_GROUPS, SSM_STATE))
            im_p.append(him.reshape(BATCH, SSM_GROUPS, SSM_STATE))
            re_s.append(nre.reshape(DEC_BATCH, SSM_GROUPS, SSM_STATE))
            im_s.append(nim.reshape(DEC_BATCH, SSM_GROUPS, SSM_STATE))
            yg = _glu(y, ssm_w_glu, ssm_b_glu[j], j)
            xa = _xattn(z, TOK_WIDTH // XA_WIDTH, mem_k, mem_v, i, cache_mem_k, cache_mem_v)
            x, h = _proj([yg, xa], ssm_w_out, j, x, g_ffn[i])
        yf, conv_p, conv_s = _ffn_up(h, ffn_w_up, ffn_conv_w, ffn_conv_b, state_conv, i, conv_p, conv_s)
        if i + 1 < DEPTH:
            x, h = _proj([yf], ffn_w_down, i, x, g_mix[i + 1])
        else:
            y_prompt, y_sample = _proj([yf], ffn_w_down, i, x, g_final, final=True)
    mem_k = mem_k.reshape(DEPTH, BATCH, N_MEM, XA_HEADS, XA_HEAD_DIM)
    mem_v = mem_v.reshape(DEPTH, BATCH, N_MEM, XA_HEADS, XA_HEAD_DIM)
    return (y_prompt.reshape(BATCH, SEQ, D_MODEL), y_sample.reshape(DEC_BATCH, 1, D_MODEL), mem_k, mem_v,
            jnp.stack(re_p), jnp.stack(im_p), conv_p,
            jnp.stack(re_s), jnp.stack(im_s), conv_s, jnp.stack(sg_v))
```

```python
import functools
import math

import jax
import jax.numpy as jnp
from jax import lax
from jax.experimental import pallas as pl
from jax.experimental.pallas import tpu as pltpu

F32 = jnp.float32
BF16 = jnp.bfloat16

D_MODEL = 2048
BATCH = 4
SEQ = 2048
DEPTH = 4
DEC_BATCH = 128
N_MEM = 256
XA_HEADS = 4
XA_HEAD_DIM = 128
XA_WIDTH = 512
TOK_WIDTH = 1536
CHUNK = 128
SG_GROUPS = 12
SSM_GROUPS = 96
SSM_GROUP_DIM = 16
SSM_STATE = 64
D_FF = 5504
EPS = 1e-6

N_TILES = 8
P_ROWS = 1024
S_ROWS = 16
TILE = P_ROWS + S_ROWS

LANE = 128
FF_BLOCKS = D_FF // LANE
FF_PAD = (FF_BLOCKS + 1) * LANE
TK = 512

S5_CB = 256
S5_NCB = TOK_WIDTH // S5_CB
S5_ST = (S5_CB // SSM_GROUP_DIM) * SSM_STATE
S5_SLABS = S5_ST // LANE
S5_TQ = 512
S5_PITCH = S5_TQ + 8
V7X_VMEM_LIMIT = 56 * 1024 * 1024


def _cparams(n_axes, vmem_mb=None):
    kw = dict(dimension_semantics=("arbitrary",) * n_axes)
    if vmem_mb is not None:
        kw["vmem_limit_bytes"] = min(int(vmem_mb * 1024 * 1024), V7X_VMEM_LIMIT)
    return pltpu.CompilerParams(**kw)


def _rms(x, g):
    return x * lax.rsqrt(jnp.mean(x * x, axis=-1, keepdims=True) + EPS) * g


def _prep_kernel(xp_ref, xs_ref, g_ref, x_ref, h_ref):
    xp = xp_ref[0]
    xs = xs_ref[0]
    g = g_ref[...]
    x_ref[0, :P_ROWS] = xp
    x_ref[0, P_ROWS:] = xs
    h_ref[0, :P_ROWS] = _rms(xp, g).astype(BF16)
    h_ref[0, P_ROWS:] = _rms(xs, g).astype(BF16)


def _prep(x_prompt, x_sample, g):
    xp = x_prompt.reshape(N_TILES, P_ROWS, D_MODEL)
    xs = x_sample.reshape(N_TILES, S_ROWS, D_MODEL)
    return pl.pallas_call(
        _prep_kernel,
        grid=(N_TILES,),
        in_specs=[
            pl.BlockSpec((1, P_ROWS, D_MODEL), lambda j: (j, 0, 0)),
            pl.BlockSpec((1, S_ROWS, D_MODEL), lambda j: (j, 0, 0)),
            pl.BlockSpec((1, D_MODEL), lambda j: (0, 0)),
        ],
        out_specs=[
            pl.BlockSpec((1, TILE, D_MODEL), lambda j: (j, 0, 0)),
            pl.BlockSpec((1, TILE, D_MODEL), lambda j: (j, 0, 0)),
        ],
        out_shape=[
            jax.ShapeDtypeStruct((N_TILES, TILE, D_MODEL), F32),
            jax.ShapeDtypeStruct((N_TILES, TILE, D_MODEL), BF16),
        ],
        compiler_params=_cparams(1, 52),
        name="prep",
    )(xp, xs, g.reshape(1, D_MODEL))


def _mem_kv_kernel(m_ref, g_ref, w_ref, k_ref, v_ref):
    h = _rms(m_ref[...], g_ref[0]).astype(BF16)
    r = jnp.dot(h, w_ref[0].astype(BF16), preferred_element_type=F32)

    @pl.when(pl.program_id(1) == 0)
    def _():
        k_ref[0] = r

    @pl.when(pl.program_id(1) == 1)
    def _():
        v_ref[0] = r


def _mem_kv(mem_prompt, g_mem, w_mem_kv):
    rows = BATCH * N_MEM
    mem = mem_prompt.reshape(rows, D_MODEL)
    return pl.pallas_call(
        _mem_kv_kernel,
        grid=(DEPTH, 2),
        in_specs=[
            pl.BlockSpec((rows, D_MODEL), lambda i, n: (0, 0)),
            pl.BlockSpec((1, 1, D_MODEL), lambda i, n: (i, 0, 0)),
            pl.BlockSpec((1, D_MODEL, XA_WIDTH), lambda i, n: (i, 0, n)),
        ],
        out_specs=[
            pl.BlockSpec((1, rows, XA_WIDTH), lambda i, n: (i, 0, 0)),
            pl.BlockSpec((1, rows, XA_WIDTH), lambda i, n: (i, 0, 0)),
        ],
        out_shape=[
            jax.ShapeDtypeStruct((DEPTH, rows, XA_WIDTH), F32),
            jax.ShapeDtypeStruct((DEPTH, rows, XA_WIDTH), F32),
        ],
        compiler_params=_cparams(2, 40),
        name="mem_kv",
    )(mem, g_mem.reshape(DEPTH, 1, D_MODEL), w_mem_kv)


def _in_proj_kernel(h_ref, w_ref, z_ref, zs_ref, wbf_ref, *, tn, act_cols):
    n = pl.program_id(0)

    @pl.when(pl.program_id(1) == 0)
    def _():
        wbf_ref[...] = w_ref[...].astype(BF16)

    acc = jnp.dot(h_ref[0], wbf_ref[...], preferred_element_type=F32)

    def emit(lim):
        if lim >= tn:
            r = jax.nn.gelu(acc)
        elif lim <= 0:
            r = acc
        else:
            r = jnp.concatenate([jax.nn.gelu(acc[:, :lim]), acc[:, lim:]], axis=1)
        z_ref[0] = r.astype(BF16)
        zs_ref[0] = r[P_ROWS:]

    full, rem = divmod(act_cols, tn)
    if act_cols == 0:
        emit(0)
    else:
        @pl.when(n < full)
        def _():
            emit(tn)

        @pl.when(n == full)
        def _():
            emit(rem)


def _in_proj(h, w, layer, tn, act_cols):
    width = w.shape[2]
    assert width % tn == 0 and act_cols % LANE == 0 and (act_cols == 0 or act_cols // tn == width // tn - 1)
    return pl.pallas_call(
        functools.partial(_in_proj_kernel, tn=tn, act_cols=act_cols),
        grid=(width // tn, N_TILES),
        in_specs=[
            pl.BlockSpec((1, TILE, D_MODEL), lambda n, m: (m, 0, 0)),
            pl.BlockSpec((None, D_MODEL, tn), lambda n, m: (layer, 0, n)),
        ],
        out_specs=[
            pl.BlockSpec((1, TILE, tn), lambda n, m: (m, 0, n)),
            pl.BlockSpec((1, S_ROWS, tn), lambda n, m: (m, 0, n)),
        ],
        out_shape=[
            jax.ShapeDtypeStruct((N_TILES, TILE, width), BF16),
            jax.ShapeDtypeStruct((N_TILES, S_ROWS, width), F32),
        ],
        scratch_shapes=[pltpu.VMEM((D_MODEL, tn), BF16)],
        compiler_params=_cparams(2, 48),
        name="in_proj",
    )(h, w)


def _sg_gate_kernel(u_ref, v_ref, us_ref, vs_ref, gv_ref, ws_ref, bias_ref, coef_ref, tok_ref, sgv_ref, vn_ref):
    gv = gv_ref[...]
    vn_ref[...] = _rms(v_ref[0, :P_ROWS, :].astype(F32), gv)
    row = lax.broadcasted_iota(jnp.int32, (CHUNK, CHUNK), 0)
    col = lax.broadcasted_iota(jnp.int32, (CHUNK, CHUNK), 1)
    causal = col <= row
    wms = [jnp.where(causal, ws_ref[g], 0.0).astype(BF16) for g in range(SG_GROUPS)]

    def chunk(c, carry):
        r0 = pl.multiple_of(c * CHUNK, CHUNK)
        for g in range(SG_GROUPS):
            cols = slice(g * LANE, (g + 1) * LANE)
            blk = vn_ref[pl.ds(r0, CHUNK), cols].astype(BF16)
            s = jnp.dot(wms[g], blk, preferred_element_type=F32) + bias_ref[:, cols]
            tok_ref[0, pl.ds(r0, CHUNK), cols] = (u_ref[0, pl.ds(r0, CHUNK), cols].astype(F32) * s).astype(BF16)
        return carry

    lax.fori_loop(0, P_ROWS // CHUNK, chunk, 0)
    vs = _rms(vs_ref[0], gv)
    sgv_ref[0] = vs
    s = coef_ref[...] * vs + bias_ref[0:1, :]
    tok_ref[0, P_ROWS:, :] = (us_ref[0] * s).astype(BF16)


def _sg_gate(z, zs, g_v, w_s, b_s, layer):
    bias = jnp.repeat(b_s.T, LANE, axis=1)
    coef = jnp.repeat(w_s[layer, :, 0, 0], LANE).reshape(1, TOK_WIDTH)
    return pl.pallas_call(
        _sg_gate_kernel,
        grid=(N_TILES,),
        in_specs=[
            pl.BlockSpec((1, TILE, TOK_WIDTH), lambda j: (j, 0, 0)),
            pl.BlockSpec((1, TILE, TOK_WIDTH), lambda j: (j, 0, 1)),
            pl.BlockSpec((1, S_ROWS, TOK_WIDTH), lambda j: (j, 0, 0)),
            pl.BlockSpec((1, S_ROWS, TOK_WIDTH), lambda j: (j, 0, 1)),
            pl.BlockSpec((1, TOK_WIDTH), lambda j: (0, 0)),
            pl.BlockSpec((None, SG_GROUPS, CHUNK, CHUNK), lambda j: (layer, 0, 0, 0)),
            pl.BlockSpec((CHUNK, TOK_WIDTH), lambda j: (0, 0)),
            pl.BlockSpec((1, TOK_WIDTH), lambda j: (0, 0)),
        ],
        out_specs=[
            pl.BlockSpec((1, TILE, TOK_WIDTH), lambda j: (j, 0, 0)),
            pl.BlockSpec((1, S_ROWS, TOK_WIDTH), lambda j: (j, 0, 0)),
        ],
        out_shape=[
            jax.ShapeDtypeStruct((N_TILES, TILE, TOK_WIDTH), BF16),
            jax.ShapeDtypeStruct((N_TILES, S_ROWS, TOK_WIDTH), F32),
        ],
        scratch_shapes=[pltpu.VMEM((P_ROWS, TOK_WIDTH), F32)],
        compiler_params=_cparams(1, 40),
        name="sg_gate",
    )(z, z, zs, zs, g_v.reshape(1, TOK_WIDTH), w_s, bias, coef)


XA_SUB = 8


def _xattn_kernel(q_ref, qs_ref, mk_ref, mv_ref, ck_ref, cv_ref, o_ref, os_ref, q4_ref, o4_ref):
    scale = XA_HEAD_DIM ** -0.5
    sub = pl.program_id(1)

    @pl.when(sub == 0)
    def _():
        for h in range(XA_HEADS):
            cols = slice(h * XA_HEAD_DIM, (h + 1) * XA_HEAD_DIM)
            qh = q_ref[0, :P_ROWS, cols]
            kh = mk_ref[:, cols].astype(BF16)
            vh = mv_ref[:, cols].astype(BF16)
            s = lax.dot_general(qh, kh, (((1,), (1,)), ((), ())), preferred_element_type=F32) * scale
            s = s - jnp.max(s, axis=-1, keepdims=True)
            e = jnp.exp(s)
            p = (e / jnp.sum(e, axis=-1, keepdims=True)).astype(BF16)
            o_ref[0, :P_ROWS, cols] = jnp.dot(p, vh, preferred_element_type=F32).astype(BF16)

    o0 = pl.multiple_of(sub * XA_SUB, XA_SUB)
    qs = qs_ref[0, pl.ds(o0, XA_SUB), :] * scale
    for h in range(XA_HEADS):
        q4_ref[:, h, :] = qs[:, h * XA_HEAD_DIM:(h + 1) * XA_HEAD_DIM]
    q4 = q4_ref[...]
    s = jnp.sum(ck_ref[...] * q4[:, None], axis=-1, keepdims=True)
    e = jnp.exp(s - jnp.max(s, axis=1, keepdims=True))
    den = jnp.sum(e, axis=1)
    o4_ref[...] = jnp.sum(e * cv_ref[...], axis=1) / den
    for h in range(XA_HEADS):
        os_ref[pl.ds(o0, XA_SUB), h * XA_HEAD_DIM:(h + 1) * XA_HEAD_DIM] = o4_ref[:, h, :]

    @pl.when(sub == S_ROWS // XA_SUB - 1)
    def _():
        o_ref[0, P_ROWS:, :] = os_ref[...].astype(BF16)


def _xattn(z, zs, q_blk, mem_k, mem_v, layer, cache_k, cache_v):
    n_sub = S_ROWS // XA_SUB
    cache_blk = (None, XA_SUB, N_MEM, XA_HEADS, XA_HEAD_DIM)
    return pl.pallas_call(
        _xattn_kernel,
        grid=(N_TILES, n_sub),
        in_specs=[
            pl.BlockSpec((1, TILE, XA_WIDTH), lambda j, s: (j, 0, q_blk)),
            pl.BlockSpec((1, S_ROWS, XA_WIDTH), lambda j, s: (j, 0, q_blk)),
            pl.BlockSpec((None, N_MEM, XA_WIDTH), lambda j, s: (layer, j // 2, 0)),
            pl.BlockSpec((None, N_MEM, XA_WIDTH), lambda j, s: (layer, j // 2, 0)),
            pl.BlockSpec(cache_blk, lambda j, s: (layer, j * n_sub + s, 0, 0, 0)),
            pl.BlockSpec(cache_blk, lambda j, s: (layer, j * n_sub + s, 0, 0, 0)),
        ],
        out_specs=pl.BlockSpec((1, TILE, XA_WIDTH), lambda j, s: (j, 0, 0)),
        out_shape=jax.ShapeDtypeStruct((N_TILES, TILE, XA_WIDTH), BF16),
        scratch_shapes=[
            pltpu.VMEM((S_ROWS, XA_WIDTH), F32),
            pltpu.VMEM((XA_SUB, XA_HEADS, XA_HEAD_DIM), F32),
            pltpu.VMEM((XA_SUB, XA_HEADS, XA_HEAD_DIM), F32),
        ],
        compiler_params=_cparams(2, 52),
        name="xattn",
    )(z, zs, mem_k, mem_v, cache_k, cache_v)


def _s5_kernel(u_ref, bm_ref, cm_ref, lbr_ref, lbi_ref, d_ref, y_ref, hre_ref, him_ref, xs_ref, hs_ref):
    tq = pl.program_id(1)

    @pl.when(tq == 0)
    def _():
        hs_ref[...] = jnp.zeros_like(hs_ref)

    bm = bm_ref[0]
    for b in range(BATCH):
        x = jnp.dot(u_ref[b, 0], bm, preferred_element_type=F32)
        for s in range(2 * S5_SLABS):
            xs_ref[s, b * S5_PITCH:b * S5_PITCH + S5_TQ, :] = x[:, s * LANE:(s + 1) * LANE]

    lbr = [lbr_ref[0, p:p + 1, :] for p in range(S5_SLABS)]
    lbi = [lbi_ref[0, p:p + 1, :] for p in range(S5_SLABS)]

    def step(t, carry):
        new = []
        for p in range(S5_SLABS):
            hr, hi = carry[2 * p], carry[2 * p + 1]
            xr = xs_ref[p, pl.ds(t, BATCH, stride=S5_PITCH), :]
            xi = xs_ref[S5_SLABS + p, pl.ds(t, BATCH, stride=S5_PITCH), :]
            nr = lbr[p] * hr - lbi[p] * hi + xr
            ni = lbr[p] * hi + lbi[p] * hr + xi
            xs_ref[p, pl.ds(t, BATCH, stride=S5_PITCH), :] = nr
            xs_ref[S5_SLABS + p, pl.ds(t, BATCH, stride=S5_PITCH), :] = ni
            new += [nr, ni]
        return tuple(new)

    init = []
    for p in range(S5_SLABS):
        init += [hs_ref[p, :BATCH, :], hs_ref[S5_SLABS + p, :BATCH, :]]
    fin = lax.fori_loop(0, S5_TQ, step, tuple(init), unroll=2)
    for p in range(S5_SLABS):
        hs_ref[p, :BATCH, :] = fin[2 * p]
        hs_ref[S5_SLABS + p, :BATCH, :] = fin[2 * p + 1]
        hre_ref[:, p * LANE:(p + 1) * LANE] = fin[2 * p]
        him_ref[:, p * LANE:(p + 1) * LANE] = fin[2 * p + 1]

    cm = cm_ref[0]
    d = d_ref[...]
    for b in range(BATCH):
        hcat = jnp.concatenate(
            [xs_ref[s, b * S5_PITCH:b * S5_PITCH + S5_TQ, :].astype(BF16) for s in range(2 * S5_SLABS)], axis=1)
        y = jnp.dot(hcat, cm, preferred_element_type=F32) + d * u_ref[b, 0].astype(F32)
        y_ref[b, 0] = jax.nn.gelu(y)


def _s5_prompt(z, prm):
    z4 = z.reshape(BATCH, 2, TILE, D_MODEL)
    blk = (BATCH, 1, S5_TQ, S5_CB)
    y, hre, him = pl.pallas_call(
        _s5_kernel,
        grid=(S5_NCB, SEQ // S5_TQ),
        in_specs=[
            pl.BlockSpec(blk, lambda c, t: (0, t // 2, t % 2, c)),
            pl.BlockSpec((1, S5_CB, 2 * S5_ST), lambda c, t: (c, 0, 0)),
            pl.BlockSpec((1, 2 * S5_ST, S5_CB), lambda c, t: (c, 0, 0)),
            pl.BlockSpec((1, S5_SLABS, LANE), lambda c, t: (c, 0, 0)),
            pl.BlockSpec((1, S5_SLABS, LANE), lambda c, t: (c, 0, 0)),
            pl.BlockSpec((1, S5_CB), lambda c, t: (0, c)),
        ],
        out_specs=[
            pl.BlockSpec(blk, lambda c, t: (0, t // 2, t % 2, c)),
            pl.BlockSpec((BATCH, S5_ST), lambda c, t: (0, c)),
            pl.BlockSpec((BATCH, S5_ST), lambda c, t: (0, c)),
        ],
        out_shape=[
            jax.ShapeDtypeStruct((BATCH, 2, P_ROWS, TOK_WIDTH), F32),
            jax.ShapeDtypeStruct((BATCH, SSM_GROUPS * SSM_STATE), F32),
            jax.ShapeDtypeStruct((BATCH, SSM_GROUPS * SSM_STATE), F32),
        ],
        scratch_shapes=[
            pltpu.VMEM((2 * S5_SLABS, BATCH * S5_PITCH, LANE), F32),
            pltpu.VMEM((2 * S5_SLABS, 8, LANE), F32),
        ],
        compiler_params=_cparams(2, 48),
        name="s5_prompt",
    )(z4, prm["bm_hi"], prm["cm"], prm["lbr"].reshape(S5_NCB, S5_SLABS, LANE),
      prm["lbi"].reshape(S5_NCB, S5_SLABS, LANE), prm["d"])
    return y.reshape(N_TILES, P_ROWS, TOK_WIDTH), hre, him


def _s5_sample_kernel(u_ref, sre_ref, sim_ref, bh_ref, bl_ref, cm_ref, lbr_ref, lbi_ref, d_ref,
                      y_ref, nre_ref, nim_ref):
    u = u_ref[0]
    uh = u.astype(BF16)
    ul = (u - uh.astype(F32)).astype(BF16)
    bh, bl = bh_ref[0], bl_ref[0]
    x = (jnp.dot(ul, bh, preferred_element_type=F32) + jnp.dot(uh, bl, preferred_element_type=F32)
         + jnp.dot(uh, bh, preferred_element_type=F32))
    lbr, lbi = lbr_ref[...], lbi_ref[...]
    sr, si = sre_ref[...], sim_ref[...]
    nr = lbr * sr - lbi * si + x[:, :S5_ST]
    ni = lbr * si + lbi * sr + x[:, S5_ST:]
    nre_ref[...] = nr
    nim_ref[...] = ni
    hcat = jnp.concatenate([nr.astype(BF16), ni.astype(BF16)], axis=1)
    y = jnp.dot(hcat, cm_ref[0], preferred_element_type=F32) + d_ref[...] * u
    y_ref[0] = jax.nn.gelu(y)


def _s5_sample(zs, s_re, s_im, prm):
    nst = SSM_GROUPS * SSM_STATE
    return pl.pallas_call(
        _s5_sample_kernel,
        grid=(S5_NCB, N_TILES),
        in_specs=[
            pl.BlockSpec((1, S_ROWS, S5_CB), lambda c, j: (j, 0, c)),
            pl.BlockSpec((S_ROWS, S5_ST), lambda c, j: (j, c)),
            pl.BlockSpec((S_ROWS, S5_ST), lambda c, j: (j, c)),
            pl.BlockSpec((1, S5_CB, 2 * S5_ST), lambda c, j: (c, 0, 0)),
            pl.BlockSpec((1, S5_CB, 2 * S5_ST), lambda c, j: (c, 0, 0)),
            pl.BlockSpec((1, 2 * S5_ST, S5_CB), lambda c, j: (c, 0, 0)),
            pl.BlockSpec((1, S5_ST), lambda c, j: (0, c)),
            pl.BlockSpec((1, S5_ST), lambda c, j: (0, c)),
            pl.BlockSpec((1, S5_CB), lambda c, j: (0, c)),
        ],
        out_specs=[
            pl.BlockSpec((1, S_ROWS, S5_CB), lambda c, j: (j, 0, c)),
            pl.BlockSpec((S_ROWS, S5_ST), lambda c, j: (j, c)),
            pl.BlockSpec((S_ROWS, S5_ST), lambda c, j: (j, c)),
        ],
        out_shape=[
            jax.ShapeDtypeStruct((N_TILES, S_ROWS, TOK_WIDTH), F32),
            jax.ShapeDtypeStruct((DEC_BATCH, nst), F32),
            jax.ShapeDtypeStruct((DEC_BATCH, nst), F32),
        ],
        compiler_params=_cparams(2, 32),
        name="s5_sample",
    )(zs, s_re.reshape(DEC_BATCH, nst), s_im.reshape(DEC_BATCH, nst), prm["bm_hi"], prm["bm_lo"], prm["cm"],
      prm["lbr"].reshape(1, nst), prm["lbi"].reshape(1, nst), prm["d"])


def _s5_params(lam_re, lam_im, log_dt, b_re, b_im, c_re, c_im, d):
    dt = jnp.exp(log_dt)[:, None]
    ar, ai = lam_re * dt, lam_im * dt
    mag = jnp.exp(ar)
    lb_re, lb_im = mag * jnp.cos(ai), mag * jnp.sin(ai)
    nr, ni = lb_re - 1.0, lb_im
    den = lam_re * lam_re + lam_im * lam_im
    k_re = (nr * lam_re + ni * lam_im) / den
    k_im = (ni * lam_re - nr * lam_im) / den
    bb_re = k_re[..., None] * b_re - k_im[..., None] * b_im
    bb_im = k_re[..., None] * b_im + k_im[..., None] * b_re
    gpb = S5_CB // SSM_GROUP_DIM
    eye = jnp.eye(gpb, dtype=F32)

    def blockdiag_in(m):
        m = m.reshape(S5_NCB, gpb, SSM_STATE, SSM_GROUP_DIM)
        return jnp.einsum("ngpc,gh->ngchp", m, eye).reshape(S5_NCB, S5_CB, S5_ST)

    def blockdiag_out(m):
        m = m.reshape(S5_NCB, gpb, SSM_GROUP_DIM, SSM_STATE)
        return jnp.einsum("ngcp,gh->ngphc", m, eye).reshape(S5_NCB, S5_ST, S5_CB)

    bm = jnp.concatenate([blockdiag_in(bb_re), blockdiag_in(bb_im)], axis=2)
    bm_hi = bm.astype(BF16)
    bm_lo = (bm - bm_hi.astype(F32)).astype(BF16)
    cm = jnp.concatenate([blockdiag_out(c_re), -blockdiag_out(c_im)], axis=1).astype(BF16)
    return dict(bm_hi=bm_hi, bm_lo=bm_lo, cm=cm, lbr=lb_re, lbi=lb_im, d=d.reshape(1, TOK_WIDTH))


def _glu_kernel(yp_ref, ys_ref, w_ref, b_ref, o_ref, wbf_ref, ybf_ref, *, tn):
    @pl.when(pl.program_id(0) == 0)
    def _():
        wbf_ref[...] = w_ref[...].astype(BF16)

    ybf_ref[:P_ROWS, :] = yp_ref[0].astype(BF16)
    ybf_ref[P_ROWS:, :] = ys_ref[0].astype(BF16)
    for c in range(TOK_WIDTH // tn):
        cols = slice(c * tn, (c + 1) * tn)
        gate = jnp.dot(ybf_ref[...], wbf_ref[:, cols], preferred_element_type=F32) + b_ref[:, cols]
        sg = jax.nn.sigmoid(gate)
        o_ref[0, :P_ROWS, cols] = (yp_ref[0, :, cols] * sg[:P_ROWS]).astype(BF16)
        o_ref[0, P_ROWS:, cols] = (ys_ref[0, :, cols] * sg[P_ROWS:]).astype(BF16)


def _glu(y_p, y_s, w, b, layer):
    tn = 512
    return pl.pallas_call(
        functools.partial(_glu_kernel, tn=tn),
        grid=(N_TILES,),
        in_specs=[
            pl.BlockSpec((1, P_ROWS, TOK_WIDTH), lambda m: (m, 0, 0)),
            pl.BlockSpec((1, S_ROWS, TOK_WIDTH), lambda m: (m, 0, 0)),
            pl.BlockSpec((None, TOK_WIDTH, TOK_WIDTH), lambda m: (layer, 0, 0), pipeline_mode=pl.Buffered(1)),
            pl.BlockSpec((1, TOK_WIDTH), lambda m: (0, 0)),
        ],
        out_specs=pl.BlockSpec((1, TILE, TOK_WIDTH), lambda m: (m, 0, 0)),
        out_shape=jax.ShapeDtypeStruct((N_TILES, TILE, TOK_WIDTH), BF16),
        scratch_shapes=[pltpu.VMEM((TOK_WIDTH, TOK_WIDTH), BF16), pltpu.VMEM((TILE, TOK_WIDTH), BF16)],
        compiler_params=_cparams(1, 48),
        name="glu",
    )(y_p, y_s, w, b.reshape(1, TOK_WIDTH))


PN = 256


def _proj_kernel(*refs, part_steps, k_total, final):
    n_parts = len(part_steps)
    parts = refs[:n_parts]
    w_ref, x_ref, g_ref = refs[n_parts:n_parts + 3]
    outs = refs[n_parts + 3:]
    if final:
        yp_ref, ys_ref, acc = outs
    else:
        xn_ref, h_ref = outs
        acc = xn_ref.at[0]
    k = pl.program_id(1)
    n_k = sum(part_steps)
    ragged = k_total % TK

    def accumulate(p_ref, first, masked):
        lhs = p_ref[0]
        for n in range(D_MODEL // PN):
            cols = slice(n * PN, (n + 1) * PN)
            w = w_ref[:, cols]
            if masked:
                rows = lax.broadcasted_iota(jnp.int32, (TK, PN), 0)
                w = jnp.where(rows < ragged, w, 0.0)
            d = jnp.dot(lhs, w.astype(BF16), preferred_element_type=F32)
            xc = x_ref[0, :, (n % (TK // PN)) * PN:(n % (TK // PN) + 1) * PN]
            d = d + jnp.where(k == n // (TK // PN), xc, 0.0)
            if first:
                acc[:, cols] = d
            else:
                acc[:, cols] += d

    spans = []
    start = 0
    for p_ref, cnt in zip(parts, part_steps):
        lo, hi = start, start + cnt
        if lo == 0:
            spans.append((0, 1, p_ref, True, False))
            lo = 1
        if ragged and hi == n_k:
            spans.append((hi - 1, hi, p_ref, False, True))
            hi -= 1
        if lo < hi:
            spans.append((lo, hi, p_ref, False, False))
        start += cnt
    for lo, hi, p_ref, first, masked in spans:
        @pl.when((k >= lo) & (k < hi))
        def _(p_ref=p_ref, first=first, masked=masked):
            accumulate(p_ref, first, masked)

    @pl.when(k == n_k - 1)
    def _():
        h = _rms(acc[...], g_ref[...])
        if final:
            yp_ref[0] = h[:P_ROWS]
            ys_ref[0] = h[P_ROWS:]
        else:
            h_ref[0] = h.astype(BF16)


def _proj(parts, w, layer, x, g, final=False):
    k_total = w.shape[1]
    part_steps = tuple(p.shape[-1] // TK for p in parts)
    n_k = sum(part_steps)
    assert n_k == pl.cdiv(k_total, TK) and n_k >= D_MODEL // TK and n_k > 2
    starts = [sum(part_steps[:i]) for i in range(len(parts))]
    in_specs = []
    for s0, cnt in zip(starts, part_steps):
        in_specs.append(pl.BlockSpec(
            (1, TILE, TK), lambda m, k, s0=s0, cnt=cnt: (m, 0, jnp.clip(k - s0, 0, cnt - 1))))
    in_specs += [
        pl.BlockSpec((None, TK, D_MODEL), lambda m, k: (layer, k, 0)),
        pl.BlockSpec((1, TILE, TK), lambda m, k: (m, 0, jnp.minimum(k, D_MODEL // TK - 1))),
        pl.BlockSpec((1, D_MODEL), lambda m, k: (0, 0)),
    ]
    if final:
        out_specs = [
            pl.BlockSpec((1, P_ROWS, D_MODEL), lambda m, k: (m, 0, 0)),
            pl.BlockSpec((1, S_ROWS, D_MODEL), lambda m, k: (m, 0, 0)),
        ]
        out_shape = [
            jax.ShapeDtypeStruct((N_TILES, P_ROWS, D_MODEL), F32),
            jax.ShapeDtypeStruct((N_TILES, S_ROWS, D_MODEL), F32),
        ]
        scratch = [pltpu.VMEM((TILE, D_MODEL), F32)]
    else:
        out_specs = [
            pl.BlockSpec((1, TILE, D_MODEL), lambda m, k: (m, 0, 0)),
            pl.BlockSpec((1, TILE, D_MODEL), lambda m, k: (m, 0, 0)),
        ]
        out_shape = [
            jax.ShapeDtypeStruct((N_TILES, TILE, D_MODEL), F32),
            jax.ShapeDtypeStruct((N_TILES, TILE, D_MODEL), BF16),
        ]
        scratch = []
    return pl.pallas_call(
        functools.partial(_proj_kernel, part_steps=part_steps, k_total=k_total, final=final),
        grid=(N_TILES, n_k),
        in_specs=in_specs,
        out_specs=out_specs,
        out_shape=out_shape,
        scratch_shapes=scratch,
        compiler_params=_cparams(2, 52),
        name="proj_final" if final else "proj",
    )(*parts, w, x, g.reshape(1, D_MODEL))


def _ffn_up_kernel(h_ref, wa_ref, wg_ref, cw_ref, cb_ref, sc_ref, *rest):
    y_ref, cp_ref, cs_ref, wbf_ref = rest[-4:]
    f = pl.program_id(0)

    @pl.when(f == FF_BLOCKS)
    def _():
        y_ref[...] = jnp.zeros(y_ref.shape, BF16)

    @pl.when(f < FF_BLOCKS)
    def _():
        wbf_ref[:, :LANE] = wa_ref[...].astype(BF16)
        wbf_ref[:, LANE:] = wg_ref[...].astype(BF16)
        w0, w1, w2 = cw_ref[0:1, :], cw_ref[1:2, :], cw_ref[2:3, :]
        cb = cb_ref[...]
        row = lax.broadcasted_iota(jnp.int32, (P_ROWS, LANE), 0)
        tail = None
        for j in range(N_TILES):
            r = jnp.dot(h_ref[j], wbf_ref[...], preferred_element_type=F32)
            a, g = r[:, :LANE], r[:, LANE:]
            ap = a[:P_ROWS]
            if j % 2 == 0:
                m1 = jnp.zeros((1, LANE), F32)
                m2 = jnp.zeros((1, LANE), F32)
            else:
                m2, m1 = tail[0:1], tail[1:2]
            a1 = jnp.where(row >= 1, pltpu.roll(ap, 1, 0), m1)
            a2 = jnp.where(row >= 2, pltpu.roll(ap, 2, 0), jnp.where(row == 0, m2, m1))
            c = cb + w0 * a2 + w1 * a1 + w2 * ap
            y_ref[j, :P_ROWS, :] = (jax.nn.silu(c) * g[:P_ROWS]).astype(BF16)
            tail = ap[P_ROWS - 2:]
            if j % 2 == 1:
                cp_ref[j // 2] = tail
            a_s = a[P_ROWS:]
            q0 = sc_ref[j * S_ROWS:(j + 1) * S_ROWS, 0, :]
            q1 = sc_ref[j * S_ROWS:(j + 1) * S_ROWS, 1, :]
            cs = cb + w0 * q0 + w1 * q1 + w2 * a_s
            y_ref[j, P_ROWS:, :] = (jax.nn.silu(cs) * g[P_ROWS:]).astype(BF16)
            cs_ref[j * S_ROWS:(j + 1) * S_ROWS, 0, :] = q1
            cs_ref[j * S_ROWS:(j + 1) * S_ROWS, 1, :] = a_s


def _ffn_up(h, w_up, conv_w, conv_b, state_conv, layer, conv_p_prev, conv_s_prev):
    last = FF_BLOCKS - 1
    fc = lambda f: jnp.minimum(f, last)
    in_specs = [
        pl.BlockSpec((N_TILES, TILE, D_MODEL), lambda f: (0, 0, 0), pipeline_mode=pl.Buffered(1)),
        pl.BlockSpec((None, D_MODEL, LANE), lambda f: (layer, 0, fc(f))),
        pl.BlockSpec((None, D_MODEL, LANE), lambda f: (layer, 0, FF_BLOCKS + fc(f))),
        pl.BlockSpec((None, 3, LANE), lambda f: (layer, 0, fc(f))),
        pl.BlockSpec((None, 1, LANE), lambda f: (layer, 0, fc(f))),
        pl.BlockSpec((None, DEC_BATCH, 2, LANE), lambda f: (layer, 0, 0, fc(f))),
    ]
    args = [h, w_up, w_up, conv_w, conv_b.reshape(DEPTH, 1, D_FF), state_conv]
    aliases = {}
    if conv_p_prev is not None:
        in_specs += [pl.BlockSpec(memory_space=pl.ANY), pl.BlockSpec(memory_space=pl.ANY)]
        aliases = {len(args): 1, len(args) + 1: 2}
        args += [conv_p_prev, conv_s_prev]
    return pl.pallas_call(
        _ffn_up_kernel,
        grid=(FF_BLOCKS + 1,),
        in_specs=in_specs,
        out_specs=[
            pl.BlockSpec((N_TILES, TILE, LANE), lambda f: (0, 0, f)),
            pl.BlockSpec((None, BATCH, 2, LANE), lambda f: (layer, 0, 0, fc(f))),
            pl.BlockSpec((None, DEC_BATCH, 2, LANE), lambda f: (layer, 0, 0, fc(f))),
        ],
        out_shape=[
            jax.ShapeDtypeStruct((N_TILES, TILE, FF_PAD), BF16),
            jax.ShapeDtypeStruct((DEPTH, BATCH, 2, D_FF), F32),
            jax.ShapeDtypeStruct((DEPTH, DEC_BATCH, 2, D_FF), F32),
        ],
        scratch_shapes=[pltpu.VMEM((D_MODEL, 2 * LANE), BF16)],
        input_output_aliases=aliases,
        compiler_params=_cparams(1, 56),
        name="ffn_up",
    )(*args)


def kernel(x_prompt, x_sample, mem_prompt, cache_mem_k, cache_mem_v, state_ssm_re, state_ssm_im, state_conv,
           g_mix, g_ffn, g_mem, g_final, w_mem_kv, sg_w_in, sg_w_out, sg_g_v, sg_w_s, sg_b_s, ssm_w_in,
           ssm_w_out, ssm_lam_re, ssm_lam_im, ssm_log_dt, ssm_b_re, ssm_b_im, ssm_c_re, ssm_c_im, ssm_d,
           ssm_w_glu, ssm_b_glu, ffn_w_up, ffn_conv_w, ffn_conv_b, ffn_w_down):
    mem_k, mem_v = _mem_kv(mem_prompt, g_mem, w_mem_kv)
    x, h = _prep(x_prompt, x_sample, g_mix[0])
    sg_v, re_p, im_p, re_s, im_s = [], [], [], [], []
    conv_p = conv_s = None
    y_prompt = y_sample = None
    for i in range(DEPTH):
        j = i // 2
        if i % 2 == 0:
            z, zs = _in_proj(h, sg_w_in, j, tn=896, act_cols=2 * TOK_WIDTH)
            tok, v = _sg_gate(z, zs, sg_g_v[j], sg_w_s, sg_b_s[j], j)
            sg_v.append(v.reshape(DEC_BATCH, 1, TOK_WIDTH))
            xa = _xattn(z, zs, 2 * TOK_WIDTH // XA_WIDTH, mem_k, mem_v, i, cache_mem_k, cache_mem_v)
            x, h = _proj([tok, xa], sg_w_out, j, x, g_ffn[i])
        else:
            prm = _s5_params(ssm_lam_re[j], ssm_lam_im[j], ssm_log_dt[j], ssm_b_re[j], ssm_b_im[j],
                             ssm_c_re[j], ssm_c_im[j], ssm_d[j])
            z, zs = _in_proj(h, ssm_w_in, j, tn=1024, act_cols=0)
            y_p, hre, him = _s5_prompt(z, prm)
            y_s, nre, nim = _s5_sample(zs, state_ssm_re[j], state_ssm_im[j], prm)
            re_p.append(hre.reshape(BATCH, SSM_GROUPS, SSM_STATE))
            im_p.append(him.reshape(BATCH, SSM_GROUPS, SSM_STATE))
            re_s.append(nre.reshape(DEC_BATCH, SSM_GROUPS, SSM_STATE))
            im_s.append(nim.reshape(DEC_BATCH, SSM_GROUPS, SSM_STATE))
            yg = _glu(y_p, y_s, ssm_w_glu, ssm_b_glu[j], j)
            xa = _xattn(z, zs, TOK_WIDTH // XA_WIDTH, mem_k, mem_v, i, cache_mem_k, cache_mem_v)
            x, h = _proj([yg, xa], ssm_w_out, j, x, g_ffn[i])
        yf, conv_p, conv_s = _ffn_up(h, ffn_w_up, ffn_conv_w, ffn_conv_b, state_conv, i, conv_p, conv_s)
        if i + 1 < DEPTH:
            x, h = _proj([yf], ffn_w_down, i, x, g_mix[i + 1])
        else:
            y_prompt, y_sample = _proj([yf], ffn_w_down, i, x, g_final, final=True)
    mem_k = mem_k.reshape(DEPTH, BATCH, N_MEM, XA_HEADS, XA_HEAD_DIM)
    mem_v = mem_v.reshape(DEPTH, BATCH, N_MEM, XA_HEADS, XA_HEAD_DIM)
    return (y_prompt.reshape(BATCH, SEQ, D_MODEL), y_sample.reshape(DEC_BATCH, 1, D_MODEL), mem_k, mem_v,
            jnp.stack(re_p), jnp.stack(im_p), conv_p,
            jnp.stack(re_s), jnp.stack(im_s), conv_s, jnp.stack(sg_v))
```

```python
import functools
import math

import jax
import jax.numpy as jnp
from jax import lax
from jax.experimental import pallas as pl
from jax.experimental.pallas import tpu as pltpu

F32 = jnp.float32
BF16 = jnp.bfloat16

D_MODEL = 2048
BATCH = 4
SEQ = 2048
DEPTH = 4
DEC_BATCH = 128
N_MEM = 256
XA_HEADS = 4
XA_HEAD_DIM = 128
XA_WIDTH = 512
TOK_WIDTH = 1536
CHUNK = 128
SG_GROUPS = 12
SSM_GROUPS = 96
SSM_GROUP_DIM = 16
SSM_STATE = 64
D_FF = 5504
EPS = 1e-6

N_TILES = 8
P_ROWS = 1024
S_ROWS = 16
TILE = P_ROWS + S_ROWS

LANE = 128
FF_BLOCKS = D_FF // LANE
FF_PAD = (FF_BLOCKS + 1) * LANE
TK = 512

S5_CB = 256
S5_NCB = TOK_WIDTH // S5_CB
S5_ST = (S5_CB // SSM_GROUP_DIM) * SSM_STATE
S5_SLABS = S5_ST // LANE
S5_TQ = 512
S5_PITCH = S5_TQ + 8
V7X_VMEM_LIMIT = 56 * 1024 * 1024


def _cparams(n_axes, vmem_mb=None):
    kw = dict(dimension_semantics=("arbitrary",) * n_axes)
    if vmem_mb is not None:
        kw["vmem_limit_bytes"] = min(int(vmem_mb * 1024 * 1024), V7X_VMEM_LIMIT)
    return pltpu.CompilerParams(**kw)


def _rms(x, g):
    return x * lax.rsqrt(jnp.mean(x * x, axis=-1, keepdims=True) + EPS) * g


def _prep_kernel(xp_ref, xs_ref, g_ref, x_ref, h_ref):
    xp = xp_ref[0]
    xs = xs_ref[0]
    g = g_ref[...]
    x_ref[0, :P_ROWS] = xp
    x_ref[0, P_ROWS:] = xs
    h_ref[0, :P_ROWS] = _rms(xp, g).astype(BF16)
    h_ref[0, P_ROWS:] = _rms(xs, g).astype(BF16)


def _prep(x_prompt, x_sample, g):
    xp = x_prompt.reshape(N_TILES, P_ROWS, D_MODEL)
    xs = x_sample.reshape(N_TILES, S_ROWS, D_MODEL)
    return pl.pallas_call(
        _prep_kernel,
        grid=(N_TILES,),
        in_specs=[
            pl.BlockSpec((1, P_ROWS, D_MODEL), lambda j: (j, 0, 0)),
            pl.BlockSpec((1, S_ROWS, D_MODEL), lambda j: (j, 0, 0)),
            pl.BlockSpec((None, 1, D_MODEL), lambda j: (0, 0, 0)),
        ],
        out_specs=[
            pl.BlockSpec((1, TILE, D_MODEL), lambda j: (j, 0, 0)),
            pl.BlockSpec((1, TILE, D_MODEL), lambda j: (j, 0, 0)),
        ],
        out_shape=[
            jax.ShapeDtypeStruct((N_TILES, TILE, D_MODEL), F32),
            jax.ShapeDtypeStruct((N_TILES, TILE, D_MODEL), BF16),
        ],
        compiler_params=_cparams(1, 52),
        name="prep",
    )(xp, xs, g)


def _mem_kv_kernel(m_ref, g_ref, w_ref, k_ref, v_ref):
    h = _rms(m_ref[...], g_ref[0]).astype(BF16)
    r = jnp.dot(h, w_ref[0].astype(BF16), preferred_element_type=F32)

    @pl.when(pl.program_id(1) == 0)
    def _():
        k_ref[0] = r

    @pl.when(pl.program_id(1) == 1)
    def _():
        v_ref[0] = r


def _mem_kv(mem_prompt, g_mem, w_mem_kv):
    rows = BATCH * N_MEM
    mem = mem_prompt.reshape(rows, D_MODEL)
    return pl.pallas_call(
        _mem_kv_kernel,
        grid=(DEPTH, 2),
        in_specs=[
            pl.BlockSpec((rows, D_MODEL), lambda i, n: (0, 0)),
            pl.BlockSpec((1, 1, D_MODEL), lambda i, n: (i, 0, 0)),
            pl.BlockSpec((1, D_MODEL, XA_WIDTH), lambda i, n: (i, 0, n)),
        ],
        out_specs=[
            pl.BlockSpec((1, rows, XA_WIDTH), lambda i, n: (i, 0, 0)),
            pl.BlockSpec((1, rows, XA_WIDTH), lambda i, n: (i, 0, 0)),
        ],
        out_shape=[
            jax.ShapeDtypeStruct((DEPTH, rows, XA_WIDTH), F32),
            jax.ShapeDtypeStruct((DEPTH, rows, XA_WIDTH), F32),
        ],
        compiler_params=_cparams(2, 40),
        name="mem_kv",
    )(mem, g_mem.reshape(DEPTH, 1, D_MODEL), w_mem_kv)


def _in_proj_kernel(h_ref, w_ref, z_ref, zs_ref, wbf_ref, *, act):
    @pl.when(pl.program_id(1) == 0)
    def _():
        wbf_ref[...] = w_ref[...].astype(BF16)

    r = jnp.dot(h_ref[0], wbf_ref[...], preferred_element_type=F32)
    if act:
        r = jax.nn.gelu(r)
    z_ref[0] = r.astype(BF16)
    zs_ref[0] = r[P_ROWS:]


def _in_proj(h, w, layer, col0, width, tn, act):
    assert col0 % tn == 0 and width % tn == 0
    c0 = col0 // tn
    return pl.pallas_call(
        functools.partial(_in_proj_kernel, act=act),
        grid=(width // tn, N_TILES),
        in_specs=[
            pl.BlockSpec((1, TILE, D_MODEL), lambda n, m: (m, 0, 0)),
            pl.BlockSpec((None, D_MODEL, tn), lambda n, m: (layer, 0, c0 + n)),
        ],
        out_specs=[
            pl.BlockSpec((1, TILE, tn), lambda n, m: (m, 0, n)),
            pl.BlockSpec((1, S_ROWS, tn), lambda n, m: (m, 0, n)),
        ],
        out_shape=[
            jax.ShapeDtypeStruct((N_TILES, TILE, width), BF16),
            jax.ShapeDtypeStruct((N_TILES, S_ROWS, width), F32),
        ],
        scratch_shapes=[pltpu.VMEM((D_MODEL, tn), BF16)],
        compiler_params=_cparams(2, 48),
        name="in_proj",
    )(h, w)


def _sg_gate_kernel(u_ref, v_ref, us_ref, vs_ref, gv_ref, ws_ref, bias_ref, coef_ref, tok_ref, sgv_ref, vn_ref):
    gv = gv_ref[...]
    vn_ref[...] = _rms(v_ref[0, :P_ROWS, :].astype(F32), gv)
    row = lax.broadcasted_iota(jnp.int32, (CHUNK, CHUNK), 0)
    col = lax.broadcasted_iota(jnp.int32, (CHUNK, CHUNK), 1)
    causal = col <= row
    wms = [jnp.where(causal, ws_ref[g], 0.0).astype(BF16) for g in range(SG_GROUPS)]

    def chunk(c, carry):
        r0 = pl.multiple_of(c * CHUNK, CHUNK)
        for g in range(SG_GROUPS):
            cols = slice(g * LANE, (g + 1) * LANE)
            blk = vn_ref[pl.ds(r0, CHUNK), cols].astype(BF16)
            s = jnp.dot(wms[g], blk, preferred_element_type=F32) + bias_ref[:, cols]
            tok_ref[0, pl.ds(r0, CHUNK), cols] = (u_ref[0, pl.ds(r0, CHUNK), cols].astype(F32) * s).astype(BF16)
        return carry

    lax.fori_loop(0, P_ROWS // CHUNK, chunk, 0)
    vs = _rms(vs_ref[0], gv)
    sgv_ref[0] = vs
    s = coef_ref[...] * vs + bias_ref[0:1, :]
    tok_ref[0, P_ROWS:, :] = (us_ref[0] * s).astype(BF16)


def _sg_gate(uv, uvs, g_v, w_s, bias, coef, layer):
    par = lambda shape: pl.BlockSpec((None,) + shape, lambda j: (layer,) + (0,) * len(shape))
    return pl.pallas_call(
        _sg_gate_kernel,
        grid=(N_TILES,),
        in_specs=[
            pl.BlockSpec((1, TILE, TOK_WIDTH), lambda j: (j, 0, 0)),
            pl.BlockSpec((1, TILE, TOK_WIDTH), lambda j: (j, 0, 1)),
            pl.BlockSpec((1, S_ROWS, TOK_WIDTH), lambda j: (j, 0, 0)),
            pl.BlockSpec((1, S_ROWS, TOK_WIDTH), lambda j: (j, 0, 1)),
            par((1, TOK_WIDTH)),
            par((SG_GROUPS, CHUNK, CHUNK)),
            par((CHUNK, TOK_WIDTH)),
            par((1, TOK_WIDTH)),
        ],
        out_specs=[
            pl.BlockSpec((1, TILE, TOK_WIDTH), lambda j: (j, 0, 0)),
            pl.BlockSpec((1, S_ROWS, TOK_WIDTH), lambda j: (j, 0, 0)),
        ],
        out_shape=[
            jax.ShapeDtypeStruct((N_TILES, TILE, TOK_WIDTH), BF16),
            jax.ShapeDtypeStruct((N_TILES, S_ROWS, TOK_WIDTH), F32),
        ],
        scratch_shapes=[pltpu.VMEM((P_ROWS, TOK_WIDTH), F32)],
        compiler_params=_cparams(1, 40),
        name="sg_gate",
    )(uv, uv, uvs, uvs, g_v, w_s, bias, coef)


XA_SUB = 8


def _xattn_kernel(q_ref, qs_ref, mk_ref, mv_ref, ck_ref, cv_ref, o_ref, os_ref, q4_ref, o4_ref):
    scale = XA_HEAD_DIM ** -0.5
    sub = pl.program_id(1)

    @pl.when(sub == 0)
    def _():
        for h in range(XA_HEADS):
            cols = slice(h * XA_HEAD_DIM, (h + 1) * XA_HEAD_DIM)
            qh = q_ref[0, :P_ROWS, cols]
            kh = mk_ref[:, cols].astype(BF16)
            vh = mv_ref[:, cols].astype(BF16)
            s = lax.dot_general(qh, kh, (((1,), (1,)), ((), ())), preferred_element_type=F32) * scale
            s = s - jnp.max(s, axis=-1, keepdims=True)
            e = jnp.exp(s)
            p = (e / jnp.sum(e, axis=-1, keepdims=True)).astype(BF16)
            o_ref[0, :P_ROWS, cols] = jnp.dot(p, vh, preferred_element_type=F32).astype(BF16)

    o0 = pl.multiple_of(sub * XA_SUB, XA_SUB)
    qs = qs_ref[0, pl.ds(o0, XA_SUB), :] * scale
    for h in range(XA_HEADS):
        q4_ref[:, h, :] = qs[:, h * XA_HEAD_DIM:(h + 1) * XA_HEAD_DIM]
    q4 = q4_ref[...]
    s = jnp.sum(ck_ref[...] * q4[:, None], axis=-1, keepdims=True)
    e = jnp.exp(s - jnp.max(s, axis=1, keepdims=True))
    den = jnp.sum(e, axis=1)
    o4_ref[...] = jnp.sum(e * cv_ref[...], axis=1) / den
    for h in range(XA_HEADS):
        os_ref[pl.ds(o0, XA_SUB), h * XA_HEAD_DIM:(h + 1) * XA_HEAD_DIM] = o4_ref[:, h, :]

    @pl.when(sub == S_ROWS // XA_SUB - 1)
    def _():
        o_ref[0, P_ROWS:, :] = os_ref[...].astype(BF16)


def _xattn(z, zs, q_blk, mem_k, mem_v, layer, cache_k, cache_v):
    n_sub = S_ROWS // XA_SUB
    cache_blk = (None, XA_SUB, N_MEM, XA_HEADS, XA_HEAD_DIM)
    return pl.pallas_call(
        _xattn_kernel,
        grid=(N_TILES, n_sub),
        in_specs=[
            pl.BlockSpec((1, TILE, XA_WIDTH), lambda j, s: (j, 0, q_blk)),
            pl.BlockSpec((1, S_ROWS, XA_WIDTH), lambda j, s: (j, 0, q_blk)),
            pl.BlockSpec((None, N_MEM, XA_WIDTH), lambda j, s: (layer, j // 2, 0)),
            pl.BlockSpec((None, N_MEM, XA_WIDTH), lambda j, s: (layer, j // 2, 0)),
            pl.BlockSpec(cache_blk, lambda j, s: (layer, j * n_sub + s, 0, 0, 0)),
            pl.BlockSpec(cache_blk, lambda j, s: (layer, j * n_sub + s, 0, 0, 0)),
        ],
        out_specs=pl.BlockSpec((1, TILE, XA_WIDTH), lambda j, s: (j, 0, 0)),
        out_shape=jax.ShapeDtypeStruct((N_TILES, TILE, XA_WIDTH), BF16),
        scratch_shapes=[
            pltpu.VMEM((S_ROWS, XA_WIDTH), F32),
            pltpu.VMEM((XA_SUB, XA_HEADS, XA_HEAD_DIM), F32),
            pltpu.VMEM((XA_SUB, XA_HEADS, XA_HEAD_DIM), F32),
        ],
        compiler_params=_cparams(2, 52),
        name="xattn",
    )(z, zs, mem_k, mem_v, cache_k, cache_v)


def _s5_kernel(u_ref, bm_ref, cm_ref, lbr_ref, lbi_ref, d_ref, *rest):
    y_ref, hre_ref, him_ref, xs_ref, hs_ref = rest[-5:]
    tq = pl.program_id(1)

    @pl.when(tq == 0)
    def _():
        hs_ref[...] = jnp.zeros_like(hs_ref)

    bm = bm_ref[0]
    for b in range(BATCH):
        x = jnp.dot(u_ref[b, 0], bm, preferred_element_type=F32)
        for s in range(2 * S5_SLABS):
            xs_ref[s, b * S5_PITCH:b * S5_PITCH + S5_TQ, :] = x[:, s * LANE:(s + 1) * LANE]

    lbr = [lbr_ref[0, p:p + 1, :] for p in range(S5_SLABS)]
    lbi = [lbi_ref[0, p:p + 1, :] for p in range(S5_SLABS)]

    def step(t, carry):
        new = []
        for p in range(S5_SLABS):
            hr, hi = carry[2 * p], carry[2 * p + 1]
            xr = xs_ref[p, pl.ds(t, BATCH, stride=S5_PITCH), :]
            xi = xs_ref[S5_SLABS + p, pl.ds(t, BATCH, stride=S5_PITCH), :]
            nr = lbr[p] * hr - lbi[p] * hi + xr
            ni = lbr[p] * hi + lbi[p] * hr + xi
            xs_ref[p, pl.ds(t, BATCH, stride=S5_PITCH), :] = nr
            xs_ref[S5_SLABS + p, pl.ds(t, BATCH, stride=S5_PITCH), :] = ni
            new += [nr, ni]
        return tuple(new)

    init = []
    for p in range(S5_SLABS):
        init += [hs_ref[p, :BATCH, :], hs_ref[S5_SLABS + p, :BATCH, :]]
    fin = lax.fori_loop(0, S5_TQ, step, tuple(init), unroll=2)
    for p in range(S5_SLABS):
        hs_ref[p, :BATCH, :] = fin[2 * p]
        hs_ref[S5_SLABS + p, :BATCH, :] = fin[2 * p + 1]
        hre_ref[:, p * LANE:(p + 1) * LANE] = fin[2 * p]
        him_ref[:, p * LANE:(p + 1) * LANE] = fin[2 * p + 1]

    cm = cm_ref[0]
    d = d_ref[...]
    for b in range(BATCH):
        hcat = jnp.concatenate(
            [xs_ref[s, b * S5_PITCH:b * S5_PITCH + S5_TQ, :].astype(BF16) for s in range(2 * S5_SLABS)], axis=1)
        y = jnp.dot(hcat, cm, preferred_element_type=F32) + d * u_ref[b, 0].astype(F32)
        y_ref[b, 0] = jax.nn.gelu(y)


def _s5_prompt(z, prm, layer, prev):
    z4 = z.reshape(BATCH, 2, TILE, D_MODEL)
    blk = (BATCH, 1, S5_TQ, S5_CB)
    nst = SSM_GROUPS * SSM_STATE
    n_layers = prm["d"].shape[0]
    in_specs = [
        pl.BlockSpec(blk, lambda c, t: (0, t // 2, t % 2, c)),
        pl.BlockSpec((None, 1, S5_CB, 2 * S5_ST), lambda c, t: (layer, c, 0, 0)),
        pl.BlockSpec((None, 1, 2 * S5_ST, S5_CB), lambda c, t: (layer, c, 0, 0)),
        pl.BlockSpec((None, 1, S5_SLABS, LANE), lambda c, t: (layer, c, 0, 0)),
        pl.BlockSpec((None, 1, S5_SLABS, LANE), lambda c, t: (layer, c, 0, 0)),
        pl.BlockSpec((None, 1, S5_CB), lambda c, t: (layer, 0, c)),
    ]
    args = [z4, prm["bm_hi"], prm["cm"], prm["lbr"].reshape(n_layers, S5_NCB, S5_SLABS, LANE),
            prm["lbi"].reshape(n_layers, S5_NCB, S5_SLABS, LANE), prm["d"]]
    aliases = {}
    if prev is not None:
        in_specs += [pl.BlockSpec(memory_space=pl.ANY), pl.BlockSpec(memory_space=pl.ANY)]
        aliases = {len(args): 1, len(args) + 1: 2}
        args += list(prev)
    y, hre, him = pl.pallas_call(
        _s5_kernel,
        grid=(S5_NCB, SEQ // S5_TQ),
        in_specs=in_specs,
        out_specs=[
            pl.BlockSpec(blk, lambda c, t: (0, t // 2, t % 2, c)),
            pl.BlockSpec((None, BATCH, S5_ST), lambda c, t: (layer, 0, c)),
            pl.BlockSpec((None, BATCH, S5_ST), lambda c, t: (layer, 0, c)),
        ],
        out_shape=[
            jax.ShapeDtypeStruct((BATCH, 2, P_ROWS, TOK_WIDTH), F32),
            jax.ShapeDtypeStruct((n_layers, BATCH, nst), F32),
            jax.ShapeDtypeStruct((n_layers, BATCH, nst), F32),
        ],
        scratch_shapes=[
            pltpu.VMEM((2 * S5_SLABS, BATCH * S5_PITCH, LANE), F32),
            pltpu.VMEM((2 * S5_SLABS, 8, LANE), F32),
        ],
        input_output_aliases=aliases,
        compiler_params=_cparams(2, 48),
        name="s5_prompt",
    )(*args)
    return y.reshape(N_TILES, P_ROWS, TOK_WIDTH), hre, him


def _s5_sample_kernel(u_ref, sre_ref, sim_ref, bh_ref, bl_ref, cm_ref, lbr_ref, lbi_ref, d_ref, *rest):
    y_ref, nre_ref, nim_ref = rest[-3:]
    u = u_ref[...].reshape(DEC_BATCH, S5_CB)
    uh = u.astype(BF16)
    ul = (u - uh.astype(F32)).astype(BF16)
    bh, bl = bh_ref[0], bl_ref[0]
    x = (jnp.dot(ul, bh, preferred_element_type=F32) + jnp.dot(uh, bl, preferred_element_type=F32)
         + jnp.dot(uh, bh, preferred_element_type=F32))
    lbr, lbi = lbr_ref[...], lbi_ref[...]
    sr, si = sre_ref[...], sim_ref[...]
    nr = lbr * sr - lbi * si + x[:, :S5_ST]
    ni = lbr * si + lbi * sr + x[:, S5_ST:]
    nre_ref[...] = nr
    nim_ref[...] = ni
    hcat = jnp.concatenate([nr.astype(BF16), ni.astype(BF16)], axis=1)
    y = jnp.dot(hcat, cm_ref[0], preferred_element_type=F32) + d_ref[...] * u
    y_ref[...] = jax.nn.gelu(y).reshape(N_TILES, S_ROWS, S5_CB)


def _s5_sample(zs, s_re, s_im, prm, layer, prev):
    nst = SSM_GROUPS * SSM_STATE
    n_layers = prm["d"].shape[0]
    st_spec = pl.BlockSpec((None, DEC_BATCH, S5_ST), lambda c: (layer, 0, c))
    in_specs = [
        pl.BlockSpec((N_TILES, S_ROWS, S5_CB), lambda c: (0, 0, c)),
        st_spec,
        st_spec,
        pl.BlockSpec((None, 1, S5_CB, 2 * S5_ST), lambda c: (layer, c, 0, 0)),
        pl.BlockSpec((None, 1, S5_CB, 2 * S5_ST), lambda c: (layer, c, 0, 0)),
        pl.BlockSpec((None, 1, 2 * S5_ST, S5_CB), lambda c: (layer, c, 0, 0)),
        pl.BlockSpec((None, 1, S5_ST), lambda c: (layer, 0, c)),
        pl.BlockSpec((None, 1, S5_ST), lambda c: (layer, 0, c)),
        pl.BlockSpec((None, 1, S5_CB), lambda c: (layer, 0, c)),
    ]
    args = [zs, s_re, s_im, prm["bm_hi"], prm["bm_lo"], prm["cm"], prm["lbr"].reshape(n_layers, 1, nst),
            prm["lbi"].reshape(n_layers, 1, nst), prm["d"]]
    aliases = {}
    if prev is not None:
        in_specs += [pl.BlockSpec(memory_space=pl.ANY), pl.BlockSpec(memory_space=pl.ANY)]
        aliases = {len(args): 1, len(args) + 1: 2}
        args += list(prev)
    return pl.pallas_call(
        _s5_sample_kernel,
        grid=(S5_NCB,),
        in_specs=in_specs,
        out_specs=[pl.BlockSpec((N_TILES, S_ROWS, S5_CB), lambda c: (0, 0, c)), st_spec, st_spec],
        out_shape=[
            jax.ShapeDtypeStruct((N_TILES, S_ROWS, TOK_WIDTH), F32),
            jax.ShapeDtypeStruct((n_layers, DEC_BATCH, nst), F32),
            jax.ShapeDtypeStruct((n_layers, DEC_BATCH, nst), F32),
        ],
        input_output_aliases=aliases,
        compiler_params=_cparams(1, 40),
        name="s5_sample",
    )(*args)


def _s5_params(lam_re, lam_im, log_dt, b_re, b_im, c_re, c_im, d):
    n_layers = lam_re.shape[0]
    dt = jnp.exp(log_dt)[..., None]
    ar, ai = lam_re * dt, lam_im * dt
    mag = jnp.exp(ar)
    lb_re, lb_im = mag * jnp.cos(ai), mag * jnp.sin(ai)
    nr, ni = lb_re - 1.0, lb_im
    den = lam_re * lam_re + lam_im * lam_im
    k_re = (nr * lam_re + ni * lam_im) / den
    k_im = (ni * lam_re - nr * lam_im) / den
    bb_re = k_re[..., None] * b_re - k_im[..., None] * b_im
    bb_im = k_re[..., None] * b_im + k_im[..., None] * b_re
    gpb = S5_CB // SSM_GROUP_DIM
    eye = jnp.eye(gpb, dtype=F32)

    def blockdiag_in(m):
        m = m.reshape(n_layers, S5_NCB, gpb, SSM_STATE, SSM_GROUP_DIM)
        return jnp.einsum("lngpc,gh->lngchp", m, eye).reshape(n_layers, S5_NCB, S5_CB, S5_ST)

    def blockdiag_out(m):
        m = m.reshape(n_layers, S5_NCB, gpb, SSM_GROUP_DIM, SSM_STATE)
        return jnp.einsum("lngcp,gh->lngphc", m, eye).reshape(n_layers, S5_NCB, S5_ST, S5_CB)

    bm = jnp.concatenate([blockdiag_in(bb_re), blockdiag_in(bb_im)], axis=3)
    bm_hi = bm.astype(BF16)
    bm_lo = (bm - bm_hi.astype(F32)).astype(BF16)
    cm = jnp.concatenate([blockdiag_out(c_re), -blockdiag_out(c_im)], axis=2).astype(BF16)
    return dict(bm_hi=bm_hi, bm_lo=bm_lo, cm=cm, lbr=lb_re, lbi=lb_im, d=d.reshape(n_layers, 1, TOK_WIDTH))


def _glu_kernel(yp_ref, ys_ref, w_ref, b_ref, o_ref, wbf_ref, ybf_ref, *, tn):
    @pl.when(pl.program_id(0) == 0)
    def _():
        wbf_ref[...] = w_ref[...].astype(BF16)

    ybf_ref[:P_ROWS, :] = yp_ref[0].astype(BF16)
    ybf_ref[P_ROWS:, :] = ys_ref[0].astype(BF16)
    for c in range(TOK_WIDTH // tn):
        cols = slice(c * tn, (c + 1) * tn)
        gate = jnp.dot(ybf_ref[...], wbf_ref[:, cols], preferred_element_type=F32) + b_ref[:, cols]
        sg = jax.nn.sigmoid(gate)
        o_ref[0, :P_ROWS, cols] = (yp_ref[0, :, cols] * sg[:P_ROWS]).astype(BF16)
        o_ref[0, P_ROWS:, cols] = (ys_ref[0, :, cols] * sg[P_ROWS:]).astype(BF16)


def _glu(y_p, y_s, w, b, layer):
    tn = 512
    return pl.pallas_call(
        functools.partial(_glu_kernel, tn=tn),
        grid=(N_TILES,),
        in_specs=[
            pl.BlockSpec((1, P_ROWS, TOK_WIDTH), lambda m: (m, 0, 0)),
            pl.BlockSpec((1, S_ROWS, TOK_WIDTH), lambda m: (m, 0, 0)),
            pl.BlockSpec((None, TOK_WIDTH, TOK_WIDTH), lambda m: (layer, 0, 0), pipeline_mode=pl.Buffered(1)),
            pl.BlockSpec((None, 1, TOK_WIDTH), lambda m: (layer, 0, 0)),
        ],
        out_specs=pl.BlockSpec((1, TILE, TOK_WIDTH), lambda m: (m, 0, 0)),
        out_shape=jax.ShapeDtypeStruct((N_TILES, TILE, TOK_WIDTH), BF16),
        scratch_shapes=[pltpu.VMEM((TOK_WIDTH, TOK_WIDTH), BF16), pltpu.VMEM((TILE, TOK_WIDTH), BF16)],
        compiler_params=_cparams(1, 48),
        name="glu",
    )(y_p, y_s, w, b)


PN = 256


def _proj_kernel(*refs, part_steps, final):
    n_parts = len(part_steps)
    parts = refs[:n_parts]
    w_ref, x_ref, g_ref = refs[n_parts:n_parts + 3]
    outs = refs[n_parts + 3:]
    if final:
        yp_ref, ys_ref, acc = outs
    else:
        xn_ref, h_ref = outs
        acc = xn_ref.at[0]
    k = pl.program_id(1)
    n_k = sum(part_steps)

    def accumulate(p_ref, first):
        lhs = p_ref[0]
        for n in range(D_MODEL // PN):
            cols = slice(n * PN, (n + 1) * PN)
            d = jnp.dot(lhs, w_ref[:, cols], preferred_element_type=F32)
            xc = x_ref[0, :, (n % (TK // PN)) * PN:(n % (TK // PN) + 1) * PN]
            d = d + jnp.where(k == n // (TK // PN), xc, 0.0)
            if first:
                acc[:, cols] = d
            else:
                acc[:, cols] += d

    spans = []
    start = 0
    for p_ref, cnt in zip(parts, part_steps):
        lo, hi = start, start + cnt
        if lo == 0:
            spans.append((0, 1, p_ref, True))
            lo = 1
        if lo < hi:
            spans.append((lo, hi, p_ref, False))
        start += cnt
    for lo, hi, p_ref, first in spans:
        @pl.when((k >= lo) & (k < hi))
        def _(p_ref=p_ref, first=first):
            accumulate(p_ref, first)

    @pl.when(k == n_k - 1)
    def _():
        h = _rms(acc[...], g_ref[...])
        if final:
            yp_ref[0] = h[:P_ROWS]
            ys_ref[0] = h[P_ROWS:]
        else:
            h_ref[0] = h.astype(BF16)


def _proj(parts, w, layer, x, g, g_idx, final=False):
    part_steps = tuple(p.shape[-1] // TK for p in parts)
    n_k = sum(part_steps)
    assert w.dtype == BF16 and n_k * TK == w.shape[1] and n_k >= D_MODEL // TK
    starts = [sum(part_steps[:i]) for i in range(len(parts))]
    in_specs = []
    for s0, cnt in zip(starts, part_steps):
        in_specs.append(pl.BlockSpec(
            (1, TILE, TK), lambda m, k, s0=s0, cnt=cnt: (m, 0, jnp.clip(k - s0, 0, cnt - 1))))
    in_specs += [
        pl.BlockSpec((None, TK, D_MODEL), lambda m, k: (layer, k, 0)),
        pl.BlockSpec((1, TILE, TK), lambda m, k: (m, 0, jnp.minimum(k, D_MODEL // TK - 1))),
        pl.BlockSpec((None, 1, D_MODEL), lambda m, k: (g_idx, 0, 0)),
    ]
    if final:
        out_specs = [
            pl.BlockSpec((1, P_ROWS, D_MODEL), lambda m, k: (m, 0, 0)),
            pl.BlockSpec((1, S_ROWS, D_MODEL), lambda m, k: (m, 0, 0)),
        ]
        out_shape = [
            jax.ShapeDtypeStruct((N_TILES, P_ROWS, D_MODEL), F32),
            jax.ShapeDtypeStruct((N_TILES, S_ROWS, D_MODEL), F32),
        ]
        scratch = [pltpu.VMEM((TILE, D_MODEL), F32)]
    else:
        out_specs = [
            pl.BlockSpec((1, TILE, D_MODEL), lambda m, k: (m, 0, 0)),
            pl.BlockSpec((1, TILE, D_MODEL), lambda m, k: (m, 0, 0)),
        ]
        out_shape = [
            jax.ShapeDtypeStruct((N_TILES, TILE, D_MODEL), F32),
            jax.ShapeDtypeStruct((N_TILES, TILE, D_MODEL), BF16),
        ]
        scratch = []
    return pl.pallas_call(
        functools.partial(_proj_kernel, part_steps=part_steps, final=final),
        grid=(N_TILES, n_k),
        in_specs=in_specs,
        out_specs=out_specs,
        out_shape=out_shape,
        scratch_shapes=scratch,
        compiler_params=_cparams(2, 52),
        name="proj_final" if final else "proj",
    )(*parts, w, x, g)


def _ffn_up_kernel(h_ref, wa_ref, wg_ref, cw_ref, cb_ref, sc_ref, wd_ref, *rest):
    y_ref, cp_ref, cs_ref, wdb_ref, wbf_ref = rest[-5:]
    f = pl.program_id(0)

    @pl.when(f == FF_BLOCKS)
    def _():
        y_ref[...] = jnp.zeros(y_ref.shape, BF16)
        wdb_ref[...] = jnp.zeros(wdb_ref.shape, BF16)

    @pl.when(f < FF_BLOCKS)
    def _():
        wdb_ref[...] = wd_ref[...].astype(BF16)
        wbf_ref[:, :LANE] = wa_ref[...].astype(BF16)
        wbf_ref[:, LANE:] = wg_ref[...].astype(BF16)
        w0, w1, w2 = cw_ref[0:1, :], cw_ref[1:2, :], cw_ref[2:3, :]
        cb = cb_ref[...]
        row = lax.broadcasted_iota(jnp.int32, (P_ROWS, LANE), 0)
        tail = None
        for j in range(N_TILES):
            r = jnp.dot(h_ref[j], wbf_ref[...], preferred_element_type=F32)
            a, g = r[:, :LANE], r[:, LANE:]
            ap = a[:P_ROWS]
            if j % 2 == 0:
                m1 = jnp.zeros((1, LANE), F32)
                m2 = jnp.zeros((1, LANE), F32)
            else:
                m2, m1 = tail[0:1], tail[1:2]
            a1 = jnp.where(row >= 1, pltpu.roll(ap, 1, 0), m1)
            a2 = jnp.where(row >= 2, pltpu.roll(ap, 2, 0), jnp.where(row == 0, m2, m1))
            c = cb + w0 * a2 + w1 * a1 + w2 * ap
            y_ref[j, :P_ROWS, :] = (jax.nn.silu(c) * g[:P_ROWS]).astype(BF16)
            tail = ap[P_ROWS - 2:]
            if j % 2 == 1:
                cp_ref[j // 2] = tail
            a_s = a[P_ROWS:]
            q0 = sc_ref[j * S_ROWS:(j + 1) * S_ROWS, 0, :]
            q1 = sc_ref[j * S_ROWS:(j + 1) * S_ROWS, 1, :]
            cs = cb + w0 * q0 + w1 * q1 + w2 * a_s
            y_ref[j, P_ROWS:, :] = (jax.nn.silu(cs) * g[P_ROWS:]).astype(BF16)
            cs_ref[j * S_ROWS:(j + 1) * S_ROWS, 0, :] = q1
            cs_ref[j * S_ROWS:(j + 1) * S_ROWS, 1, :] = a_s


def _ffn_up(h, w_up, conv_w, conv_b, state_conv, w_down, layer, conv_p_prev, conv_s_prev):
    last = FF_BLOCKS - 1
    fc = lambda f: jnp.minimum(f, last)
    in_specs = [
        pl.BlockSpec((N_TILES, TILE, D_MODEL), lambda f: (0, 0, 0), pipeline_mode=pl.Buffered(1)),
        pl.BlockSpec((None, D_MODEL, LANE), lambda f: (layer, 0, fc(f))),
        pl.BlockSpec((None, D_MODEL, LANE), lambda f: (layer, 0, FF_BLOCKS + fc(f))),
        pl.BlockSpec((None, 3, LANE), lambda f: (layer, 0, fc(f))),
        pl.BlockSpec((None, 1, LANE), lambda f: (layer, 0, fc(f))),
        pl.BlockSpec((None, DEC_BATCH, 2, LANE), lambda f: (layer, 0, 0, fc(f))),
        pl.BlockSpec((None, LANE, D_MODEL), lambda f: (layer, fc(f), 0)),
    ]
    args = [h, w_up, w_up, conv_w, conv_b, state_conv, w_down]
    aliases = {}
    if conv_p_prev is not None:
        in_specs += [pl.BlockSpec(memory_space=pl.ANY), pl.BlockSpec(memory_space=pl.ANY)]
        aliases = {len(args): 1, len(args) + 1: 2}
        args += [conv_p_prev, conv_s_prev]
    return pl.pallas_call(
        _ffn_up_kernel,
        grid=(FF_BLOCKS + 1,),
        in_specs=in_specs,
        out_specs=[
            pl.BlockSpec((N_TILES, TILE, LANE), lambda f: (0, 0, f)),
            pl.BlockSpec((None, BATCH, 2, LANE), lambda f: (layer, 0, 0, fc(f))),
            pl.BlockSpec((None, DEC_BATCH, 2, LANE), lambda f: (layer, 0, 0, fc(f))),
            pl.BlockSpec((None, LANE, D_MODEL), lambda f: (0, f, 0)),
        ],
        out_shape=[
            jax.ShapeDtypeStruct((N_TILES, TILE, FF_PAD), BF16),
            jax.ShapeDtypeStruct((DEPTH, BATCH, 2, D_FF), F32),
            jax.ShapeDtypeStruct((DEPTH, DEC_BATCH, 2, D_FF), F32),
            jax.ShapeDtypeStruct((1, FF_PAD, D_MODEL), BF16),
        ],
        scratch_shapes=[pltpu.VMEM((D_MODEL, 2 * LANE), BF16)],
        input_output_aliases=aliases,
        compiler_params=_cparams(1, 56),
        name="ffn_up",
    )(*args)


def kernel(x_prompt, x_sample, mem_prompt, cache_mem_k, cache_mem_v, state_ssm_re, state_ssm_im, state_conv,
           g_mix, g_ffn, g_mem, g_final, w_mem_kv, sg_w_in, sg_w_out, sg_g_v, sg_w_s, sg_b_s, ssm_w_in,
           ssm_w_out, ssm_lam_re, ssm_lam_im, ssm_log_dt, ssm_b_re, ssm_b_im, ssm_c_re, ssm_c_im, ssm_d,
           ssm_w_glu, ssm_b_glu, ffn_w_up, ffn_conv_w, ffn_conv_b, ffn_w_down):
    n_sg, n_ssm = sg_w_in.shape[0], ssm_w_in.shape[0]
    nst = SSM_GROUPS * SSM_STATE
    g_mix3, g_ffn3 = g_mix.reshape(DEPTH, 1, D_MODEL), g_ffn.reshape(DEPTH, 1, D_MODEL)
    g_fin3 = g_final.reshape(1, 1, D_MODEL)
    g_v3 = sg_g_v.reshape(n_sg, 1, TOK_WIDTH)
    sg_bias = jnp.repeat(jnp.swapaxes(sg_b_s, 1, 2), LANE, axis=2)
    sg_coef = jnp.repeat(sg_w_s[:, :, 0, 0], LANE, axis=1).reshape(n_sg, 1, TOK_WIDTH)
    b_glu3 = ssm_b_glu.reshape(n_ssm, 1, TOK_WIDTH)
    conv_b3 = ffn_conv_b.reshape(DEPTH, 1, D_FF)
    prm = _s5_params(ssm_lam_re, ssm_lam_im, ssm_log_dt, ssm_b_re, ssm_b_im, ssm_c_re, ssm_c_im, ssm_d)
    s_re, s_im = state_ssm_re.reshape(n_ssm, DEC_BATCH, nst), state_ssm_im.reshape(n_ssm, DEC_BATCH, nst)
    sg_w_out_bf, ssm_w_out_bf = sg_w_out.astype(BF16), ssm_w_out.astype(BF16)

    mem_k, mem_v = _mem_kv(mem_prompt, g_mem, w_mem_kv)
    x, h = _prep(x_prompt, x_sample, g_mix3)
    sg_v = []
    conv_p = conv_s = st_p = st_s = None
    y_prompt = y_sample = None
    for i in range(DEPTH):
        j = i // 2
        if i % 2 == 0:
            uv, uvs = _in_proj(h, sg_w_in, j, 0, 2 * TOK_WIDTH, tn=1024, act=True)
            q, qs = _in_proj(h, sg_w_in, j, 2 * TOK_WIDTH, XA_WIDTH, tn=512, act=False)
            tok, v = _sg_gate(uv, uvs, g_v3, sg_w_s, sg_bias, sg_coef, j)
            sg_v.append(v.reshape(DEC_BATCH, 1, TOK_WIDTH))
            xa = _xattn(q, qs, 0, mem_k, mem_v, i, cache_mem_k, cache_mem_v)
            x, h = _proj([tok, xa], sg_w_out_bf, j, x, g_ffn3, i)
        else:
            z, zs = _in_proj(h, ssm_w_in, j, 0, D_MODEL, tn=1024, act=False)
            y_p, *st_p = _s5_prompt(z, prm, j, st_p)
            y_s, *st_s = _s5_sample(zs, s_re, s_im, prm, j, st_s)
            yg = _glu(y_p, y_s, ssm_w_glu, b_glu3, j)
            xa = _xattn(z, zs, TOK_WIDTH // XA_WIDTH, mem_k, mem_v, i, cache_mem_k, cache_mem_v)
            x, h = _proj([yg, xa], ssm_w_out_bf, j, x, g_ffn3, i)
        yf, conv_p, conv_s, w_dn = _ffn_up(h, ffn_w_up, ffn_conv_w, conv_b3, state_conv, ffn_w_down, i,
                                           conv_p, conv_s)
        if i + 1 < DEPTH:
            x, h = _proj([yf], w_dn, 0, x, g_mix3, i + 1)
        else:
            y_prompt, y_sample = _proj([yf], w_dn, 0, x, g_fin3, 0, final=True)
    mem_k = mem_k.reshape(DEPTH, BATCH, N_MEM, XA_HEADS, XA_HEAD_DIM)
    mem_v = mem_v.reshape(DEPTH, BATCH, N_MEM, XA_HEADS, XA_HEAD_DIM)
    st4 = lambda a, b: a.reshape(n_ssm, b, SSM_GROUPS, SSM_STATE)
    return (y_prompt.reshape(BATCH, SEQ, D_MODEL), y_sample.reshape(DEC_BATCH, 1, D_MODEL), mem_k, mem_v,
            st4(st_p[0], BATCH), st4(st_p[1], BATCH), conv_p,
            st4(st_s[0], DEC_BATCH), st4(st_s[1], DEC_BATCH), conv_s, jnp.stack(sg_v))
```

```python
import functools
import math

import jax
import jax.numpy as jnp
from jax import lax
from jax.experimental import pallas as pl
from jax.experimental.pallas import tpu as pltpu

F32 = jnp.float32
BF16 = jnp.bfloat16

D_MODEL = 2048
BATCH = 4
SEQ = 2048
DEPTH = 4
DEC_BATCH = 128
N_MEM = 256
XA_HEADS = 4
XA_HEAD_DIM = 128
XA_WIDTH = 512
TOK_WIDTH = 1536
CHUNK = 128
SG_GROUPS = 12
SSM_GROUPS = 96
SSM_GROUP_DIM = 16
SSM_STATE = 64
D_FF = 5504
EPS = 1e-6

N_TILES = 8
P_ROWS = 1024
S_ROWS = 16
TILE = P_ROWS + S_ROWS

LANE = 128
FF_BLOCKS = D_FF // LANE
FF_PAD = (FF_BLOCKS + 1) * LANE
TK = 512
CONV_PAD = 8

S5_CB = 256
S5_NCB = TOK_WIDTH // S5_CB
S5_ST = (S5_CB // SSM_GROUP_DIM) * SSM_STATE
S5_SLABS = S5_ST // LANE
S5_TQ = 512
S5_PITCH = S5_TQ + 8
V7X_VMEM_LIMIT = 56 * 1024 * 1024


def _cparams(n_axes, vmem_mb=None):
    kw = dict(dimension_semantics=("arbitrary",) * n_axes)
    if vmem_mb is not None:
        kw["vmem_limit_bytes"] = min(int(vmem_mb * 1024 * 1024), V7X_VMEM_LIMIT)
    return pltpu.CompilerParams(**kw)


def _rms(x, g):
    return x * lax.rsqrt(jnp.mean(x * x, axis=-1, keepdims=True) + EPS) * g


def _prep_kernel(xp_ref, xs_ref, g_ref, x_ref, h_ref):
    xp = xp_ref[0]
    xs = xs_ref[0]
    g = g_ref[...]
    x_ref[0, :P_ROWS] = xp
    x_ref[0, P_ROWS:] = xs
    h_ref[0, :P_ROWS] = _rms(xp, g).astype(BF16)
    h_ref[0, P_ROWS:] = _rms(xs, g).astype(BF16)


def _prep(x_prompt, x_sample, g):
    xp = x_prompt.reshape(N_TILES, P_ROWS, D_MODEL)
    xs = x_sample.reshape(N_TILES, S_ROWS, D_MODEL)
    return pl.pallas_call(
        _prep_kernel,
        grid=(N_TILES,),
        in_specs=[
            pl.BlockSpec((1, P_ROWS, D_MODEL), lambda j: (j, 0, 0)),
            pl.BlockSpec((1, S_ROWS, D_MODEL), lambda j: (j, 0, 0)),
            pl.BlockSpec((None, 1, D_MODEL), lambda j: (0, 0, 0)),
        ],
        out_specs=[
            pl.BlockSpec((1, TILE, D_MODEL), lambda j: (j, 0, 0)),
            pl.BlockSpec((1, TILE, D_MODEL), lambda j: (j, 0, 0)),
        ],
        out_shape=[
            jax.ShapeDtypeStruct((N_TILES, TILE, D_MODEL), F32),
            jax.ShapeDtypeStruct((N_TILES, TILE, D_MODEL), BF16),
        ],
        compiler_params=_cparams(1, 52),
        name="prep",
    )(xp, xs, g)


def _mem_kv_kernel(m_ref, g_ref, w_ref, k_ref, v_ref):
    h = _rms(m_ref[...], g_ref[0]).astype(BF16)
    r = jnp.dot(h, w_ref[0].astype(BF16), preferred_element_type=F32)

    @pl.when(pl.program_id(1) == 0)
    def _():
        k_ref[0] = r

    @pl.when(pl.program_id(1) == 1)
    def _():
        v_ref[0] = r


def _mem_kv(mem_prompt, g_mem, w_mem_kv):
    rows = BATCH * N_MEM
    mem = mem_prompt.reshape(rows, D_MODEL)
    return pl.pallas_call(
        _mem_kv_kernel,
        grid=(DEPTH, 2),
        in_specs=[
            pl.BlockSpec((rows, D_MODEL), lambda i, n: (0, 0)),
            pl.BlockSpec((1, 1, D_MODEL), lambda i, n: (i, 0, 0)),
            pl.BlockSpec((1, D_MODEL, XA_WIDTH), lambda i, n: (i, 0, n)),
        ],
        out_specs=[
            pl.BlockSpec((1, rows, XA_WIDTH), lambda i, n: (i, 0, 0)),
            pl.BlockSpec((1, rows, XA_WIDTH), lambda i, n: (i, 0, 0)),
        ],
        out_shape=[
            jax.ShapeDtypeStruct((DEPTH, rows, XA_WIDTH), F32),
            jax.ShapeDtypeStruct((DEPTH, rows, XA_WIDTH), F32),
        ],
        compiler_params=_cparams(2, 40),
        name="mem_kv",
    )(mem, g_mem.reshape(DEPTH, 1, D_MODEL), w_mem_kv)


def _in_proj_kernel(h_ref, w_ref, z_ref, zs_ref, wbf_ref, *, act):
    @pl.when(pl.program_id(1) == 0)
    def _():
        wbf_ref[...] = w_ref[...].astype(BF16)

    r = jnp.dot(h_ref[0], wbf_ref[...], preferred_element_type=F32)
    if act:
        r = jax.nn.gelu(r)
    z_ref[0] = r.astype(BF16)
    zs_ref[0] = r[P_ROWS:]


def _in_proj(h, w, layer, col0, width, tn, act):
    assert col0 % tn == 0 and width % tn == 0
    c0 = col0 // tn
    return pl.pallas_call(
        functools.partial(_in_proj_kernel, act=act),
        grid=(width // tn, N_TILES),
        in_specs=[
            pl.BlockSpec((1, TILE, D_MODEL), lambda n, m: (m, 0, 0)),
            pl.BlockSpec((None, D_MODEL, tn), lambda n, m: (layer, 0, c0 + n)),
        ],
        out_specs=[
            pl.BlockSpec((1, TILE, tn), lambda n, m: (m, 0, n)),
            pl.BlockSpec((1, S_ROWS, tn), lambda n, m: (m, 0, n)),
        ],
        out_shape=[
            jax.ShapeDtypeStruct((N_TILES, TILE, width), BF16),
            jax.ShapeDtypeStruct((N_TILES, S_ROWS, width), F32),
        ],
        scratch_shapes=[pltpu.VMEM((D_MODEL, tn), BF16)],
        compiler_params=_cparams(2, 48),
        name="in_proj",
    )(h, w)


def _sg_gate_kernel(u_ref, v_ref, us_ref, vs_ref, gv_ref, ws_ref, bias_ref, coef_ref, tok_ref, sgv_ref, vn_ref):
    gv = gv_ref[...]
    vn_ref[...] = _rms(v_ref[0, :P_ROWS, :].astype(F32), gv)
    row = lax.broadcasted_iota(jnp.int32, (CHUNK, CHUNK), 0)
    col = lax.broadcasted_iota(jnp.int32, (CHUNK, CHUNK), 1)
    causal = col <= row
    wms = [jnp.where(causal, ws_ref[g], 0.0).astype(BF16) for g in range(SG_GROUPS)]

    def chunk(c, carry):
        r0 = pl.multiple_of(c * CHUNK, CHUNK)
        for g in range(SG_GROUPS):
            cols = slice(g * LANE, (g + 1) * LANE)
            blk = vn_ref[pl.ds(r0, CHUNK), cols].astype(BF16)
            s = jnp.dot(wms[g], blk, preferred_element_type=F32) + bias_ref[:, cols]
            tok_ref[0, pl.ds(r0, CHUNK), cols] = (u_ref[0, pl.ds(r0, CHUNK), cols].astype(F32) * s).astype(BF16)
        return carry

    lax.fori_loop(0, P_ROWS // CHUNK, chunk, 0)
    vs = _rms(vs_ref[0], gv)
    sgv_ref[0] = vs
    s = coef_ref[...] * vs + bias_ref[0:1, :]
    tok_ref[0, P_ROWS:, :] = (us_ref[0] * s).astype(BF16)


def _sg_gate(uv, uvs, g_v, w_s, bias, coef, layer):
    par = lambda shape: pl.BlockSpec((None,) + shape, lambda j: (layer,) + (0,) * len(shape))
    return pl.pallas_call(
        _sg_gate_kernel,
        grid=(N_TILES,),
        in_specs=[
            pl.BlockSpec((1, TILE, TOK_WIDTH), lambda j: (j, 0, 0)),
            pl.BlockSpec((1, TILE, TOK_WIDTH), lambda j: (j, 0, 1)),
            pl.BlockSpec((1, S_ROWS, TOK_WIDTH), lambda j: (j, 0, 0)),
            pl.BlockSpec((1, S_ROWS, TOK_WIDTH), lambda j: (j, 0, 1)),
            par((1, TOK_WIDTH)),
            par((SG_GROUPS, CHUNK, CHUNK)),
            par((CHUNK, TOK_WIDTH)),
            par((1, TOK_WIDTH)),
        ],
        out_specs=[
            pl.BlockSpec((1, TILE, TOK_WIDTH), lambda j: (j, 0, 0)),
            pl.BlockSpec((1, S_ROWS, TOK_WIDTH), lambda j: (j, 0, 0)),
        ],
        out_shape=[
            jax.ShapeDtypeStruct((N_TILES, TILE, TOK_WIDTH), BF16),
            jax.ShapeDtypeStruct((N_TILES, S_ROWS, TOK_WIDTH), F32),
        ],
        scratch_shapes=[pltpu.VMEM((P_ROWS, TOK_WIDTH), F32)],
        compiler_params=_cparams(1, 40),
        name="sg_gate",
    )(uv, uv, uvs, uvs, g_v, w_s, bias, coef)


XA_SUB = 8


def _xattn_kernel(q_ref, qs_ref, mk_ref, mv_ref, ck_ref, cv_ref, o_ref, os_ref, q8_ref, o4_ref):
    scale = XA_HEAD_DIM ** -0.5
    sub = pl.program_id(1)

    @pl.when(sub == 0)
    def _():
        for h in range(XA_HEADS):
            cols = slice(h * XA_HEAD_DIM, (h + 1) * XA_HEAD_DIM)
            qh = q_ref[0, :P_ROWS, cols]
            kh = mk_ref[:, cols].astype(BF16)
            vh = mv_ref[:, cols].astype(BF16)
            s = lax.dot_general(qh, kh, (((1,), (1,)), ((), ())), preferred_element_type=F32) * scale
            s = s - jnp.max(s, axis=-1, keepdims=True)
            e = jnp.exp(s)
            p = (e / jnp.sum(e, axis=-1, keepdims=True)).astype(BF16)
            o_ref[0, :P_ROWS, cols] = jnp.dot(p, vh, preferred_element_type=F32).astype(BF16)

    o0 = pl.multiple_of(sub * XA_SUB, XA_SUB)
    qs = qs_ref[0, pl.ds(o0, XA_SUB), :] * scale
    for h in range(XA_HEADS):
        qh = qs[:, h * XA_HEAD_DIM:(h + 1) * XA_HEAD_DIM]
        q8_ref[:, h, :] = qh
        q8_ref[:, XA_HEADS + h, :] = qh
    q8 = q8_ref[...]
    s = jnp.sum(ck_ref[...] * q8[:, None], axis=-1, keepdims=True)
    mx = jnp.max(s, axis=1, keepdims=True)
    mx = jnp.maximum(mx[:, :, :XA_HEADS], mx[:, :, XA_HEADS:])
    e = jnp.exp(s - jnp.concatenate([mx, mx], axis=2))
    den = jnp.sum(e, axis=1)
    o8 = jnp.sum(e * cv_ref[...], axis=1)
    o4_ref[...] = (o8[:, :XA_HEADS] + o8[:, XA_HEADS:]) / (den[:, :XA_HEADS] + den[:, XA_HEADS:])
    for h in range(XA_HEADS):
        os_ref[pl.ds(o0, XA_SUB), h * XA_HEAD_DIM:(h + 1) * XA_HEAD_DIM] = o4_ref[:, h, :]

    @pl.when(sub == S_ROWS // XA_SUB - 1)
    def _():
        o_ref[0, P_ROWS:, :] = os_ref[...].astype(BF16)


def _xattn(z, zs, q_blk, mem_k, mem_v, layer, cache_k, cache_v):
    n_sub = S_ROWS // XA_SUB
    pair_shape = (DEPTH, DEC_BATCH, N_MEM // 2, 2 * XA_HEADS, XA_HEAD_DIM)
    cache_k, cache_v = cache_k.reshape(pair_shape), cache_v.reshape(pair_shape)
    cache_blk = (None, XA_SUB, N_MEM // 2, 2 * XA_HEADS, XA_HEAD_DIM)
    return pl.pallas_call(
        _xattn_kernel,
        grid=(N_TILES, n_sub),
        in_specs=[
            pl.BlockSpec((1, TILE, XA_WIDTH), lambda j, s: (j, 0, q_blk)),
            pl.BlockSpec((1, S_ROWS, XA_WIDTH), lambda j, s: (j, 0, q_blk)),
            pl.BlockSpec((None, N_MEM, XA_WIDTH), lambda j, s: (layer, j // 2, 0)),
            pl.BlockSpec((None, N_MEM, XA_WIDTH), lambda j, s: (layer, j // 2, 0)),
            pl.BlockSpec(cache_blk, lambda j, s: (layer, j * n_sub + s, 0, 0, 0)),
            pl.BlockSpec(cache_blk, lambda j, s: (layer, j * n_sub + s, 0, 0, 0)),
        ],
        out_specs=pl.BlockSpec((1, TILE, XA_WIDTH), lambda j, s: (j, 0, 0)),
        out_shape=jax.ShapeDtypeStruct((N_TILES, TILE, XA_WIDTH), BF16),
        scratch_shapes=[
            pltpu.VMEM((S_ROWS, XA_WIDTH), F32),
            pltpu.VMEM((XA_SUB, 2 * XA_HEADS, XA_HEAD_DIM), F32),
            pltpu.VMEM((XA_SUB, XA_HEADS, XA_HEAD_DIM), F32),
        ],
        compiler_params=_cparams(2, 52),
        name="xattn",
    )(z, zs, mem_k, mem_v, cache_k, cache_v)


def _s5_kernel(u_ref, bm_ref, cm_ref, lbr_ref, lbi_ref, d_ref, *rest):
    y_ref, hre_ref, him_ref, xs_ref, hs_ref = rest[-5:]
    tq = pl.program_id(1)

    @pl.when(tq == 0)
    def _():
        hs_ref[...] = jnp.zeros_like(hs_ref)

    bm = bm_ref[0]
    for b in range(BATCH):
        x = jnp.dot(u_ref[b, 0], bm, preferred_element_type=F32)
        for s in range(2 * S5_SLABS):
            xs_ref[s, b * S5_PITCH:b * S5_PITCH + S5_TQ, :] = x[:, s * LANE:(s + 1) * LANE]

    lbr = [lbr_ref[0, p:p + 1, :] for p in range(S5_SLABS)]
    lbi = [lbi_ref[0, p:p + 1, :] for p in range(S5_SLABS)]

    def step(t, carry):
        new = []
        for p in range(S5_SLABS):
            hr, hi = carry[2 * p], carry[2 * p + 1]
            xr = xs_ref[p, pl.ds(t, BATCH, stride=S5_PITCH), :]
            xi = xs_ref[S5_SLABS + p, pl.ds(t, BATCH, stride=S5_PITCH), :]
            nr = lbr[p] * hr - lbi[p] * hi + xr
            ni = lbr[p] * hi + lbi[p] * hr + xi
            xs_ref[p, pl.ds(t, BATCH, stride=S5_PITCH), :] = nr
            xs_ref[S5_SLABS + p, pl.ds(t, BATCH, stride=S5_PITCH), :] = ni
            new += [nr, ni]
        return tuple(new)

    init = []
    for p in range(S5_SLABS):
        init += [hs_ref[p, :BATCH, :], hs_ref[S5_SLABS + p, :BATCH, :]]
    fin = lax.fori_loop(0, S5_TQ, step, tuple(init), unroll=2)
    for p in range(S5_SLABS):
        hs_ref[p, :BATCH, :] = fin[2 * p]
        hs_ref[S5_SLABS + p, :BATCH, :] = fin[2 * p + 1]
        hre_ref[:, p * LANE:(p + 1) * LANE] = fin[2 * p]
        him_ref[:, p * LANE:(p + 1) * LANE] = fin[2 * p + 1]

    cm = cm_ref[0]
    d = d_ref[...]
    for b in range(BATCH):
        hcat = jnp.concatenate(
            [xs_ref[s, b * S5_PITCH:b * S5_PITCH + S5_TQ, :].astype(BF16) for s in range(2 * S5_SLABS)], axis=1)
        y = jnp.dot(hcat, cm, preferred_element_type=F32) + d * u_ref[b, 0].astype(F32)
        y_ref[b, 0] = jax.nn.gelu(y)


def _s5_prompt(z, prm, layer, prev):
    z4 = z.reshape(BATCH, 2, TILE, D_MODEL)
    blk = (BATCH, 1, S5_TQ, S5_CB)
    nst = SSM_GROUPS * SSM_STATE
    n_layers = prm["d"].shape[0]
    in_specs = [
        pl.BlockSpec(blk, lambda c, t: (0, t // 2, t % 2, c)),
        pl.BlockSpec((None, 1, S5_CB, 2 * S5_ST), lambda c, t: (layer, c, 0, 0)),
        pl.BlockSpec((None, 1, 2 * S5_ST, S5_CB), lambda c, t: (layer, c, 0, 0)),
        pl.BlockSpec((None, 1, S5_SLABS, LANE), lambda c, t: (layer, c, 0, 0)),
        pl.BlockSpec((None, 1, S5_SLABS, LANE), lambda c, t: (layer, c, 0, 0)),
        pl.BlockSpec((None, 1, S5_CB), lambda c, t: (layer, 0, c)),
    ]
    args = [z4, prm["bm_hi"], prm["cm"], prm["lbr"].reshape(n_layers, S5_NCB, S5_SLABS, LANE),
            prm["lbi"].reshape(n_layers, S5_NCB, S5_SLABS, LANE), prm["d"]]
    aliases = {}
    if prev is not None:
        in_specs += [pl.BlockSpec(memory_space=pl.ANY), pl.BlockSpec(memory_space=pl.ANY)]
        aliases = {len(args): 1, len(args) + 1: 2}
        args += list(prev)
    y, hre, him = pl.pallas_call(
        _s5_kernel,
        grid=(S5_NCB, SEQ // S5_TQ),
        in_specs=in_specs,
        out_specs=[
            pl.BlockSpec(blk, lambda c, t: (0, t // 2, t % 2, c)),
            pl.BlockSpec((None, BATCH, S5_ST), lambda c, t: (layer, 0, c)),
            pl.BlockSpec((None, BATCH, S5_ST), lambda c, t: (layer, 0, c)),
        ],
        out_shape=[
            jax.ShapeDtypeStruct((BATCH, 2, P_ROWS, TOK_WIDTH), F32),
            jax.ShapeDtypeStruct((n_layers, BATCH, nst), F32),
            jax.ShapeDtypeStruct((n_layers, BATCH, nst), F32),
        ],
        scratch_shapes=[
            pltpu.VMEM((2 * S5_SLABS, BATCH * S5_PITCH, LANE), F32),
            pltpu.VMEM((2 * S5_SLABS, 8, LANE), F32),
        ],
        input_output_aliases=aliases,
        compiler_params=_cparams(2, 48),
        name="s5_prompt",
    )(*args)
    return y.reshape(N_TILES, P_ROWS, TOK_WIDTH), hre, him


def _s5_sample_kernel(u_ref, sre_ref, sim_ref, bh_ref, bl_ref, cm_ref, lbr_ref, lbi_ref, d_ref, *rest):
    y_ref, nre_ref, nim_ref = rest[-3:]
    u = u_ref[...].reshape(DEC_BATCH, S5_CB)
    uh = u.astype(BF16)
    ul = (u - uh.astype(F32)).astype(BF16)
    bh, bl = bh_ref[0], bl_ref[0]
    x = (jnp.dot(ul, bh, preferred_element_type=F32) + jnp.dot(uh, bl, preferred_element_type=F32)
         + jnp.dot(uh, bh, preferred_element_type=F32))
    lbr, lbi = lbr_ref[...], lbi_ref[...]
    sr, si = sre_ref[...], sim_ref[...]
    nr = lbr * sr - lbi * si + x[:, :S5_ST]
    ni = lbr * si + lbi * sr + x[:, S5_ST:]
    nre_ref[...] = nr
    nim_ref[...] = ni
    hcat = jnp.concatenate([nr.astype(BF16), ni.astype(BF16)], axis=1)
    y = jnp.dot(hcat, cm_ref[0], preferred_element_type=F32) + d_ref[...] * u
    y_ref[...] = jax.nn.gelu(y).reshape(N_TILES, S_ROWS, S5_CB)


def _s5_sample(zs, s_re, s_im, prm, layer, prev):
    nst = SSM_GROUPS * SSM_STATE
    n_layers = prm["d"].shape[0]
    st_spec = pl.BlockSpec((None, DEC_BATCH, S5_ST), lambda c: (layer, 0, c))
    in_specs = [
        pl.BlockSpec((N_TILES, S_ROWS, S5_CB), lambda c: (0, 0, c)),
        st_spec,
        st_spec,
        pl.BlockSpec((None, 1, S5_CB, 2 * S5_ST), lambda c: (layer, c, 0, 0)),
        pl.BlockSpec((None, 1, S5_CB, 2 * S5_ST), lambda c: (layer, c, 0, 0)),
        pl.BlockSpec((None, 1, 2 * S5_ST, S5_CB), lambda c: (layer, c, 0, 0)),
        pl.BlockSpec((None, 1, S5_ST), lambda c: (layer, 0, c)),
        pl.BlockSpec((None, 1, S5_ST), lambda c: (layer, 0, c)),
        pl.BlockSpec((None, 1, S5_CB), lambda c: (layer, 0, c)),
    ]
    args = [zs, s_re, s_im, prm["bm_hi"], prm["bm_lo"], prm["cm"], prm["lbr"].reshape(n_layers, 1, nst),
            prm["lbi"].reshape(n_layers, 1, nst), prm["d"]]
    aliases = {}
    if prev is not None:
        in_specs += [pl.BlockSpec(memory_space=pl.ANY), pl.BlockSpec(memory_space=pl.ANY)]
        aliases = {len(args): 1, len(args) + 1: 2}
        args += list(prev)
    return pl.pallas_call(
        _s5_sample_kernel,
        grid=(S5_NCB,),
        in_specs=in_specs,
        out_specs=[pl.BlockSpec((N_TILES, S_ROWS, S5_CB), lambda c: (0, 0, c)), st_spec, st_spec],
        out_shape=[
            jax.ShapeDtypeStruct((N_TILES, S_ROWS, TOK_WIDTH), F32),
            jax.ShapeDtypeStruct((n_layers, DEC_BATCH, nst), F32),
            jax.ShapeDtypeStruct((n_layers, DEC_BATCH, nst), F32),
        ],
        input_output_aliases=aliases,
        compiler_params=_cparams(1, 40),
        name="s5_sample",
    )(*args)


def _s5_params(lam_re, lam_im, log_dt, b_re, b_im, c_re, c_im, d):
    n_layers = lam_re.shape[0]
    dt = jnp.exp(log_dt)[..., None]
    ar, ai = lam_re * dt, lam_im * dt
    mag = jnp.exp(ar)
    lb_re, lb_im = mag * jnp.cos(ai), mag * jnp.sin(ai)
    nr, ni = lb_re - 1.0, lb_im
    den = lam_re * lam_re + lam_im * lam_im
    k_re = (nr * lam_re + ni * lam_im) / den
    k_im = (ni * lam_re - nr * lam_im) / den
    bb_re = k_re[..., None] * b_re - k_im[..., None] * b_im
    bb_im = k_re[..., None] * b_im + k_im[..., None] * b_re
    gpb = S5_CB // SSM_GROUP_DIM
    eye = jnp.eye(gpb, dtype=F32)

    def blockdiag_in(m):
        m = m.reshape(n_layers, S5_NCB, gpb, SSM_STATE, SSM_GROUP_DIM)
        return jnp.einsum("lngpc,gh->lngchp", m, eye).reshape(n_layers, S5_NCB, S5_CB, S5_ST)

    def blockdiag_out(m):
        m = m.reshape(n_layers, S5_NCB, gpb, SSM_GROUP_DIM, SSM_STATE)
        return jnp.einsum("lngcp,gh->lngphc", m, eye).reshape(n_layers, S5_NCB, S5_ST, S5_CB)

    bm = jnp.concatenate([blockdiag_in(bb_re), blockdiag_in(bb_im)], axis=3)
    bm_hi = bm.astype(BF16)
    bm_lo = (bm - bm_hi.astype(F32)).astype(BF16)
    cm = jnp.concatenate([blockdiag_out(c_re), -blockdiag_out(c_im)], axis=2).astype(BF16)
    return dict(bm_hi=bm_hi, bm_lo=bm_lo, cm=cm, lbr=lb_re, lbi=lb_im, d=d.reshape(n_layers, 1, TOK_WIDTH))


def _glu_kernel(yp_ref, ys_ref, w_ref, b_ref, o_ref, wbf_ref, ybf_ref, *, tn):
    @pl.when(pl.program_id(0) == 0)
    def _():
        wbf_ref[...] = w_ref[...].astype(BF16)

    ybf_ref[:P_ROWS, :] = yp_ref[0].astype(BF16)
    ybf_ref[P_ROWS:, :] = ys_ref[0].astype(BF16)
    for c in range(TOK_WIDTH // tn):
        cols = slice(c * tn, (c + 1) * tn)
        gate = jnp.dot(ybf_ref[...], wbf_ref[:, cols], preferred_element_type=F32) + b_ref[:, cols]
        sg = jax.nn.sigmoid(gate)
        o_ref[0, :P_ROWS, cols] = (yp_ref[0, :, cols] * sg[:P_ROWS]).astype(BF16)
        o_ref[0, P_ROWS:, cols] = (ys_ref[0, :, cols] * sg[P_ROWS:]).astype(BF16)


def _glu(y_p, y_s, w, b, layer):
    tn = 512
    return pl.pallas_call(
        functools.partial(_glu_kernel, tn=tn),
        grid=(N_TILES,),
        in_specs=[
            pl.BlockSpec((1, P_ROWS, TOK_WIDTH), lambda m: (m, 0, 0)),
            pl.BlockSpec((1, S_ROWS, TOK_WIDTH), lambda m: (m, 0, 0)),
            pl.BlockSpec((None, TOK_WIDTH, TOK_WIDTH), lambda m: (layer, 0, 0), pipeline_mode=pl.Buffered(1)),
            pl.BlockSpec((None, 1, TOK_WIDTH), lambda m: (layer, 0, 0)),
        ],
        out_specs=pl.BlockSpec((1, TILE, TOK_WIDTH), lambda m: (m, 0, 0)),
        out_shape=jax.ShapeDtypeStruct((N_TILES, TILE, TOK_WIDTH), BF16),
        scratch_shapes=[pltpu.VMEM((TOK_WIDTH, TOK_WIDTH), BF16), pltpu.VMEM((TILE, TOK_WIDTH), BF16)],
        compiler_params=_cparams(1, 48),
        name="glu",
    )(y_p, y_s, w, b)


PN = 256


def _proj_kernel(*refs, part_steps, final):
    n_parts = len(part_steps)
    parts = refs[:n_parts]
    w_ref, x_ref, g_ref = refs[n_parts:n_parts + 3]
    outs = refs[n_parts + 3:]
    if final:
        yp_ref, ys_ref, acc = outs
    else:
        xn_ref, h_ref = outs
        acc = xn_ref.at[0]
    k = pl.program_id(1)
    n_k = sum(part_steps)

    def accumulate(p_ref, first):
        lhs = p_ref[0]
        for n in range(D_MODEL // PN):
            cols = slice(n * PN, (n + 1) * PN)
            d = jnp.dot(lhs, w_ref[:, cols], preferred_element_type=F32)
            xc = x_ref[0, :, (n % (TK // PN)) * PN:(n % (TK // PN) + 1) * PN]
            d = d + jnp.where(k == n // (TK // PN), xc, 0.0)
            if first:
                acc[:, cols] = d
            else:
                acc[:, cols] += d

    spans = []
    start = 0
    for p_ref, cnt in zip(parts, part_steps):
        lo, hi = start, start + cnt
        if lo == 0:
            spans.append((0, 1, p_ref, True))
            lo = 1
        if lo < hi:
            spans.append((lo, hi, p_ref, False))
        start += cnt
    for lo, hi, p_ref, first in spans:
        @pl.when((k >= lo) & (k < hi))
        def _(p_ref=p_ref, first=first):
            accumulate(p_ref, first)

    @pl.when(k == n_k - 1)
    def _():
        h = _rms(acc[...], g_ref[...])
        if final:
            yp_ref[0] = h[:P_ROWS]
            ys_ref[0] = h[P_ROWS:]
        else:
            h_ref[0] = h.astype(BF16)


def _proj(parts, w, layer, x, g, g_idx, final=False):
    part_steps = tuple(p.shape[-1] // TK for p in parts)
    n_k = sum(part_steps)
    assert w.dtype == BF16 and n_k * TK == w.shape[1] and n_k >= D_MODEL // TK
    starts = [sum(part_steps[:i]) for i in range(len(parts))]
    in_specs = []
    for s0, cnt in zip(starts, part_steps):
        in_specs.append(pl.BlockSpec(
            (1, TILE, TK), lambda m, k, s0=s0, cnt=cnt: (m, 0, jnp.clip(k - s0, 0, cnt - 1))))
    in_specs += [
        pl.BlockSpec((None, TK, D_MODEL), lambda m, k: (layer, k, 0)),
        pl.BlockSpec((1, TILE, TK), lambda m, k: (m, 0, jnp.minimum(k, D_MODEL // TK - 1))),
        pl.BlockSpec((None, 1, D_MODEL), lambda m, k: (g_idx, 0, 0)),
    ]
    if final:
        out_specs = [
            pl.BlockSpec((1, P_ROWS, D_MODEL), lambda m, k: (m, 0, 0)),
            pl.BlockSpec((1, S_ROWS, D_MODEL), lambda m, k: (m, 0, 0)),
        ]
        out_shape = [
            jax.ShapeDtypeStruct((N_TILES, P_ROWS, D_MODEL), F32),
            jax.ShapeDtypeStruct((N_TILES, S_ROWS, D_MODEL), F32),
        ]
        scratch = [pltpu.VMEM((TILE, D_MODEL), F32)]
    else:
        out_specs = [
            pl.BlockSpec((1, TILE, D_MODEL), lambda m, k: (m, 0, 0)),
            pl.BlockSpec((1, TILE, D_MODEL), lambda m, k: (m, 0, 0)),
        ]
        out_shape = [
            jax.ShapeDtypeStruct((N_TILES, TILE, D_MODEL), F32),
            jax.ShapeDtypeStruct((N_TILES, TILE, D_MODEL), BF16),
        ]
        scratch = []
    return pl.pallas_call(
        functools.partial(_proj_kernel, part_steps=part_steps, final=final),
        grid=(N_TILES, n_k),
        in_specs=in_specs,
        out_specs=out_specs,
        out_shape=out_shape,
        scratch_shapes=scratch,
        compiler_params=_cparams(2, 52),
        name="proj_final" if final else "proj",
    )(*parts, w, x, g)


def _ffn_up_kernel(h_ref, wa_ref, wg_ref, cw_ref, cb_ref, sc_ref, wd_ref, *rest):
    y_ref, cp_ref, cs_ref, wdb_ref, wbf_ref, as_ref = rest[-6:]
    f = pl.program_id(0)

    @pl.when(f == FF_BLOCKS)
    def _():
        y_ref[...] = jnp.zeros(y_ref.shape, BF16)
        wdb_ref[...] = jnp.zeros(wdb_ref.shape, BF16)

    @pl.when(f < FF_BLOCKS)
    def _():
        wdb_ref[...] = wd_ref[...].astype(BF16)
        wbf_ref[:, :LANE] = wa_ref[...].astype(BF16)
        wbf_ref[:, LANE:] = wg_ref[...].astype(BF16)
        w0, w1, w2 = cw_ref[0:1, :], cw_ref[1:2, :], cw_ref[2:3, :]
        cb = cb_ref[...]
        for j in range(N_TILES):
            r = jnp.dot(h_ref[j], wbf_ref[...], preferred_element_type=F32)
            a, g = r[:, :LANE], r[:, LANE:]
            ap = a[:P_ROWS]
            if j % 2 == 0:
                as_ref[:CONV_PAD, :] = jnp.zeros((CONV_PAD, LANE), F32)
            else:
                as_ref[CONV_PAD - 2:CONV_PAD, :] = as_ref[CONV_PAD + P_ROWS - 2:, :]
            as_ref[CONV_PAD:, :] = ap
            a1 = as_ref[CONV_PAD - 1:CONV_PAD - 1 + P_ROWS, :]
            a2 = as_ref[CONV_PAD - 2:CONV_PAD - 2 + P_ROWS, :]
            c = cb + w0 * a2 + w1 * a1 + w2 * ap
            y_ref[j, :P_ROWS, :] = (jax.nn.silu(c) * g[:P_ROWS]).astype(BF16)
            if j % 2 == 1:
                cp_ref[j // 2] = as_ref[CONV_PAD + P_ROWS - 2:, :]
            a_s = a[P_ROWS:]
            q0 = sc_ref[j * S_ROWS:(j + 1) * S_ROWS, 0, :]
            q1 = sc_ref[j * S_ROWS:(j + 1) * S_ROWS, 1, :]
            cs = cb + w0 * q0 + w1 * q1 + w2 * a_s
            y_ref[j, P_ROWS:, :] = (jax.nn.silu(cs) * g[P_ROWS:]).astype(BF16)
            cs_ref[j * S_ROWS:(j + 1) * S_ROWS, 0, :] = q1
            cs_ref[j * S_ROWS:(j + 1) * S_ROWS, 1, :] = a_s


def _ffn_up(h, w_up, conv_w, conv_b, state_conv, w_down, layer, conv_p_prev, conv_s_prev):
    last = FF_BLOCKS - 1
    fc = lambda f: jnp.minimum(f, last)
    in_specs = [
        pl.BlockSpec((N_TILES, TILE, D_MODEL), lambda f: (0, 0, 0), pipeline_mode=pl.Buffered(1)),
        pl.BlockSpec((None, D_MODEL, LANE), lambda f: (layer, 0, fc(f))),
        pl.BlockSpec((None, D_MODEL, LANE), lambda f: (layer, 0, FF_BLOCKS + fc(f))),
        pl.BlockSpec((None, 3, LANE), lambda f: (layer, 0, fc(f))),
        pl.BlockSpec((None, 1, LANE), lambda f: (layer, 0, fc(f))),
        pl.BlockSpec((None, DEC_BATCH, 2, LANE), lambda f: (layer, 0, 0, fc(f))),
        pl.BlockSpec((None, LANE, D_MODEL), lambda f: (layer, fc(f), 0)),
    ]
    args = [h, w_up, w_up, conv_w, conv_b, state_conv, w_down]
    aliases = {}
    if conv_p_prev is not None:
        in_specs += [pl.BlockSpec(memory_space=pl.ANY), pl.BlockSpec(memory_space=pl.ANY)]
        aliases = {len(args): 1, len(args) + 1: 2}
        args += [conv_p_prev, conv_s_prev]
    return pl.pallas_call(
        _ffn_up_kernel,
        grid=(FF_BLOCKS + 1,),
        in_specs=in_specs,
        out_specs=[
            pl.BlockSpec((N_TILES, TILE, LANE), lambda f: (0, 0, f)),
            pl.BlockSpec((None, BATCH, 2, LANE), lambda f: (layer, 0, 0, fc(f))),
            pl.BlockSpec((None, DEC_BATCH, 2, LANE), lambda f: (layer, 0, 0, fc(f))),
            pl.BlockSpec((None, LANE, D_MODEL), lambda f: (0, f, 0)),
        ],
        out_shape=[
            jax.ShapeDtypeStruct((N_TILES, TILE, FF_PAD), BF16),
            jax.ShapeDtypeStruct((DEPTH, BATCH, 2, D_FF), F32),
            jax.ShapeDtypeStruct((DEPTH, DEC_BATCH, 2, D_FF), F32),
            jax.ShapeDtypeStruct((1, FF_PAD, D_MODEL), BF16),
        ],
        scratch_shapes=[pltpu.VMEM((D_MODEL, 2 * LANE), BF16), pltpu.VMEM((CONV_PAD + P_ROWS, LANE), F32)],
        input_output_aliases=aliases,
        compiler_params=_cparams(1, 56),
        name="ffn_up",
    )(*args)


def kernel(x_prompt, x_sample, mem_prompt, cache_mem_k, cache_mem_v, state_ssm_re, state_ssm_im, state_conv,
           g_mix, g_ffn, g_mem, g_final, w_mem_kv, sg_w_in, sg_w_out, sg_g_v, sg_w_s, sg_b_s, ssm_w_in,
           ssm_w_out, ssm_lam_re, ssm_lam_im, ssm_log_dt, ssm_b_re, ssm_b_im, ssm_c_re, ssm_c_im, ssm_d,
           ssm_w_glu, ssm_b_glu, ffn_w_up, ffn_conv_w, ffn_conv_b, ffn_w_down):
    n_sg, n_ssm = sg_w_in.shape[0], ssm_w_in.shape[0]
    nst = SSM_GROUPS * SSM_STATE
    g_mix3, g_ffn3 = g_mix.reshape(DEPTH, 1, D_MODEL), g_ffn.reshape(DEPTH, 1, D_MODEL)
    g_fin3 = g_final.reshape(1, 1, D_MODEL)
    g_v3 = sg_g_v.reshape(n_sg, 1, TOK_WIDTH)
    sg_bias = jnp.repeat(jnp.swapaxes(sg_b_s, 1, 2), LANE, axis=2)
    sg_coef = jnp.repeat(sg_w_s[:, :, 0, 0], LANE, axis=1).reshape(n_sg, 1, TOK_WIDTH)
    b_glu3 = ssm_b_glu.reshape(n_ssm, 1, TOK_WIDTH)
    conv_b3 = ffn_conv_b.reshape(DEPTH, 1, D_FF)
    prm = _s5_params(ssm_lam_re, ssm_lam_im, ssm_log_dt, ssm_b_re, ssm_b_im, ssm_c_re, ssm_c_im, ssm_d)
    s_re, s_im = state_ssm_re.reshape(n_ssm, DEC_BATCH, nst), state_ssm_im.reshape(n_ssm, DEC_BATCH, nst)
    sg_w_out_bf, ssm_w_out_bf = sg_w_out.astype(BF16), ssm_w_out.astype(BF16)

    mem_k, mem_v = _mem_kv(mem_prompt, g_mem, w_mem_kv)
    x, h = _prep(x_prompt, x_sample, g_mix3)
    sg_v = []
    conv_p = conv_s = st_p = st_s = None
    y_prompt = y_sample = None
    for i in range(DEPTH):
        j = i // 2
        if i % 2 == 0:
            uv, uvs = _in_proj(h, sg_w_in, j, 0, 2 * TOK_WIDTH, tn=1024, act=True)
            q, qs = _in_proj(h, sg_w_in, j, 2 * TOK_WIDTH, XA_WIDTH, tn=512, act=False)
            tok, v = _sg_gate(uv, uvs, g_v3, sg_w_s, sg_bias, sg_coef, j)
            sg_v.append(v.reshape(DEC_BATCH, 1, TOK_WIDTH))
            xa = _xattn(q, qs, 0, mem_k, mem_v, i, cache_mem_k, cache_mem_v)
            x, h = _proj([tok, xa], sg_w_out_bf, j, x, g_ffn3, i)
        else:
            z, zs = _in_proj(h, ssm_w_in, j, 0, D_MODEL, tn=1024, act=False)
            y_p, *st_p = _s5_prompt(z, prm, j, st_p)
            y_s, *st_s = _s5_sample(zs, s_re, s_im, prm, j, st_s)
            yg = _glu(y_p, y_s, ssm_w_glu, b_glu3, j)
            xa = _xattn(z, zs, TOK_WIDTH // XA_WIDTH, mem_k, mem_v, i, cache_mem_k, cache_mem_v)
            x, h = _proj([yg, xa], ssm_w_out_bf, j, x, g_ffn3, i)
        yf, conv_p, conv_s, w_dn = _ffn_up(h, ffn_w_up, ffn_conv_w, conv_b3, state_conv, ffn_w_down, i,
                                           conv_p, conv_s)
        if i + 1 < DEPTH:
            x, h = _proj([yf], w_dn, 0, x, g_mix3, i + 1)
        else:
            y_prompt, y_sample = _proj([yf], w_dn, 0, x, g_fin3, 0, final=True)
    mem_k = mem_k.reshape(DEPTH, BATCH, N_MEM, XA_HEADS, XA_HEAD_DIM)
    mem_v = mem_v.reshape(DEPTH, BATCH, N_MEM, XA_HEADS, XA_HEAD_DIM)
    st4 = lambda a, b: a.reshape(n_ssm, b, SSM_GROUPS, SSM_STATE)
    return (y_prompt.reshape(BATCH, SEQ, D_MODEL), y_sample.reshape(DEC_BATCH, 1, D_MODEL), mem_k, mem_v,
            st4(st_p[0], BATCH), st4(st_p[1], BATCH), conv_p,
            st4(st_s[0], DEC_BATCH), st4(st_s[1], DEC_BATCH), conv_s, jnp.stack(sg_v))
```

```python
import functools
import math

import jax
import jax.numpy as jnp
from jax import lax
from jax.experimental import pallas as pl
from jax.experimental.pallas import tpu as pltpu

F32 = jnp.float32
BF16 = jnp.bfloat16

D_MODEL = 2048
BATCH = 4
SEQ = 2048
DEPTH = 4
DEC_BATCH = 128
N_MEM = 256
XA_HEADS = 4
XA_HEAD_DIM = 128
XA_WIDTH = 512
TOK_WIDTH = 1536
CHUNK = 128
SG_GROUPS = 12
SSM_GROUPS = 96
SSM_GROUP_DIM = 16
SSM_STATE = 64
D_FF = 5504
EPS = 1e-6

N_TILES = 8
P_ROWS = 1024
S_ROWS = 16
TILE = P_ROWS + S_ROWS

LANE = 128
FF_BLOCKS = D_FF // LANE
FF_PAD = (FF_BLOCKS + 1) * LANE
TK = 512
CONV_PAD = 8

S5_CB = 256
S5_NCB = TOK_WIDTH // S5_CB
S5_ST = (S5_CB // SSM_GROUP_DIM) * SSM_STATE
S5_SLABS = S5_ST // LANE
S5_TQ = P_ROWS
S5_SEG = 64
S5_NSEG = S5_TQ // S5_SEG
S5_PITCH = S5_SEG + 8
V7X_VMEM_LIMIT = 56 * 1024 * 1024


def _cparams(n_axes, vmem_mb=None):
    kw = dict(dimension_semantics=("arbitrary",) * n_axes)
    if vmem_mb is not None:
        kw["vmem_limit_bytes"] = min(int(vmem_mb * 1024 * 1024), V7X_VMEM_LIMIT)
    return pltpu.CompilerParams(**kw)


def _rms(x, g):
    return x * lax.rsqrt(jnp.mean(x * x, axis=-1, keepdims=True) + EPS) * g


def _prep_kernel(xp_ref, xs_ref, g_ref, x_ref, h_ref):
    xp = xp_ref[0]
    xs = xs_ref[0]
    g = g_ref[...]
    x_ref[0, :P_ROWS] = xp
    x_ref[0, P_ROWS:] = xs
    h_ref[0, :P_ROWS] = _rms(xp, g).astype(BF16)
    h_ref[0, P_ROWS:] = _rms(xs, g).astype(BF16)


def _prep(x_prompt, x_sample, g):
    xp = x_prompt.reshape(N_TILES, P_ROWS, D_MODEL)
    xs = x_sample.reshape(N_TILES, S_ROWS, D_MODEL)
    return pl.pallas_call(
        _prep_kernel,
        grid=(N_TILES,),
        in_specs=[
            pl.BlockSpec((1, P_ROWS, D_MODEL), lambda j: (j, 0, 0)),
            pl.BlockSpec((1, S_ROWS, D_MODEL), lambda j: (j, 0, 0)),
            pl.BlockSpec((None, 1, D_MODEL), lambda j: (0, 0, 0)),
        ],
        out_specs=[
            pl.BlockSpec((1, TILE, D_MODEL), lambda j: (j, 0, 0)),
            pl.BlockSpec((1, TILE, D_MODEL), lambda j: (j, 0, 0)),
        ],
        out_shape=[
            jax.ShapeDtypeStruct((N_TILES, TILE, D_MODEL), F32),
            jax.ShapeDtypeStruct((N_TILES, TILE, D_MODEL), BF16),
        ],
        compiler_params=_cparams(1, 52),
        name="prep",
    )(xp, xs, g)


def _mem_kv_kernel(m_ref, g_ref, w_ref, k_ref, v_ref):
    h = _rms(m_ref[...], g_ref[0]).astype(BF16)
    r = jnp.dot(h, w_ref[0].astype(BF16), preferred_element_type=F32)

    @pl.when(pl.program_id(1) == 0)
    def _():
        k_ref[0] = r

    @pl.when(pl.program_id(1) == 1)
    def _():
        v_ref[0] = r


def _mem_kv(mem_prompt, g_mem, w_mem_kv):
    rows = BATCH * N_MEM
    mem = mem_prompt.reshape(rows, D_MODEL)
    return pl.pallas_call(
        _mem_kv_kernel,
        grid=(DEPTH, 2),
        in_specs=[
            pl.BlockSpec((rows, D_MODEL), lambda i, n: (0, 0)),
            pl.BlockSpec((1, 1, D_MODEL), lambda i, n: (i, 0, 0)),
            pl.BlockSpec((1, D_MODEL, XA_WIDTH), lambda i, n: (i, 0, n)),
        ],
        out_specs=[
            pl.BlockSpec((1, rows, XA_WIDTH), lambda i, n: (i, 0, 0)),
            pl.BlockSpec((1, rows, XA_WIDTH), lambda i, n: (i, 0, 0)),
        ],
        out_shape=[
            jax.ShapeDtypeStruct((DEPTH, rows, XA_WIDTH), F32),
            jax.ShapeDtypeStruct((DEPTH, rows, XA_WIDTH), F32),
        ],
        compiler_params=_cparams(2, 40),
        name="mem_kv",
    )(mem, g_mem.reshape(DEPTH, 1, D_MODEL), w_mem_kv)


def _in_proj_kernel(h_ref, w_ref, z_ref, zs_ref, wbf_ref, *, act):
    @pl.when(pl.program_id(1) == 0)
    def _():
        wbf_ref[...] = w_ref[...].astype(BF16)

    r = jnp.dot(h_ref[0], wbf_ref[...], preferred_element_type=F32)
    if act:
        r = jax.nn.gelu(r)
    z_ref[0] = r.astype(BF16)
    zs_ref[0] = r[P_ROWS:]


def _in_proj(h, w, layer, col0, width, tn, act):
    assert col0 % tn == 0 and width % tn == 0
    c0 = col0 // tn
    return pl.pallas_call(
        functools.partial(_in_proj_kernel, act=act),
        grid=(width // tn, N_TILES),
        in_specs=[
            pl.BlockSpec((1, TILE, D_MODEL), lambda n, m: (m, 0, 0)),
            pl.BlockSpec((None, D_MODEL, tn), lambda n, m: (layer, 0, c0 + n)),
        ],
        out_specs=[
            pl.BlockSpec((1, TILE, tn), lambda n, m: (m, 0, n)),
            pl.BlockSpec((1, S_ROWS, tn), lambda n, m: (m, 0, n)),
        ],
        out_shape=[
            jax.ShapeDtypeStruct((N_TILES, TILE, width), BF16),
            jax.ShapeDtypeStruct((N_TILES, S_ROWS, width), F32),
        ],
        scratch_shapes=[pltpu.VMEM((D_MODEL, tn), BF16)],
        compiler_params=_cparams(2, 48),
        name="in_proj",
    )(h, w)


def _sg_gate_kernel(u_ref, v_ref, us_ref, vs_ref, gv_ref, ws_ref, bias_ref, coef_ref, tok_ref, sgv_ref, vn_ref):
    gv = gv_ref[...]
    vn_ref[...] = _rms(v_ref[0, :P_ROWS, :].astype(F32), gv)
    row = lax.broadcasted_iota(jnp.int32, (CHUNK, CHUNK), 0)
    col = lax.broadcasted_iota(jnp.int32, (CHUNK, CHUNK), 1)
    causal = col <= row
    wms = [jnp.where(causal, ws_ref[g], 0.0).astype(BF16) for g in range(SG_GROUPS)]

    def chunk(c, carry):
        r0 = pl.multiple_of(c * CHUNK, CHUNK)
        for g in range(SG_GROUPS):
            cols = slice(g * LANE, (g + 1) * LANE)
            blk = vn_ref[pl.ds(r0, CHUNK), cols].astype(BF16)
            s = jnp.dot(wms[g], blk, preferred_element_type=F32) + bias_ref[:, cols]
            tok_ref[0, pl.ds(r0, CHUNK), cols] = (u_ref[0, pl.ds(r0, CHUNK), cols].astype(F32) * s).astype(BF16)
        return carry

    lax.fori_loop(0, P_ROWS // CHUNK, chunk, 0)
    vs = _rms(vs_ref[0], gv)
    sgv_ref[0] = vs
    s = coef_ref[...] * vs + bias_ref[0:1, :]
    tok_ref[0, P_ROWS:, :] = (us_ref[0] * s).astype(BF16)


def _sg_gate(uv, uvs, g_v, w_s, bias, coef, layer):
    par = lambda shape: pl.BlockSpec((None,) + shape, lambda j: (layer,) + (0,) * len(shape))
    return pl.pallas_call(
        _sg_gate_kernel,
        grid=(N_TILES,),
        in_specs=[
            pl.BlockSpec((1, TILE, TOK_WIDTH), lambda j: (j, 0, 0)),
            pl.BlockSpec((1, TILE, TOK_WIDTH), lambda j: (j, 0, 1)),
            pl.BlockSpec((1, S_ROWS, TOK_WIDTH), lambda j: (j, 0, 0)),
            pl.BlockSpec((1, S_ROWS, TOK_WIDTH), lambda j: (j, 0, 1)),
            par((1, TOK_WIDTH)),
            par((SG_GROUPS, CHUNK, CHUNK)),
            par((CHUNK, TOK_WIDTH)),
            par((1, TOK_WIDTH)),
        ],
        out_specs=[
            pl.BlockSpec((1, TILE, TOK_WIDTH), lambda j: (j, 0, 0)),
            pl.BlockSpec((1, S_ROWS, TOK_WIDTH), lambda j: (j, 0, 0)),
        ],
        out_shape=[
            jax.ShapeDtypeStruct((N_TILES, TILE, TOK_WIDTH), BF16),
            jax.ShapeDtypeStruct((N_TILES, S_ROWS, TOK_WIDTH), F32),
        ],
        scratch_shapes=[pltpu.VMEM((P_ROWS, TOK_WIDTH), F32)],
        compiler_params=_cparams(1, 40),
        name="sg_gate",
    )(uv, uv, uvs, uvs, g_v, w_s, bias, coef)


XA_SUB = 8


def _xattn_kernel(q_ref, qs_ref, mk_ref, mv_ref, ck_ref, cv_ref, o_ref, os_ref, q8_ref, o4_ref):
    scale = XA_HEAD_DIM ** -0.5
    sub = pl.program_id(1)

    @pl.when(sub == 0)
    def _():
        for h in range(XA_HEADS):
            cols = slice(h * XA_HEAD_DIM, (h + 1) * XA_HEAD_DIM)
            qh = q_ref[0, :P_ROWS, cols]
            kh = mk_ref[:, cols].astype(BF16)
            vh = mv_ref[:, cols].astype(BF16)
            s = lax.dot_general(qh, kh, (((1,), (1,)), ((), ())), preferred_element_type=F32) * scale
            s = s - jnp.max(s, axis=-1, keepdims=True)
            e = jnp.exp(s)
            p = (e / jnp.sum(e, axis=-1, keepdims=True)).astype(BF16)
            o_ref[0, :P_ROWS, cols] = jnp.dot(p, vh, preferred_element_type=F32).astype(BF16)

    o0 = pl.multiple_of(sub * XA_SUB, XA_SUB)
    qs = qs_ref[0, pl.ds(o0, XA_SUB), :] * scale
    for h in range(XA_HEADS):
        qh = qs[:, h * XA_HEAD_DIM:(h + 1) * XA_HEAD_DIM]
        q8_ref[:, h, :] = qh
        q8_ref[:, XA_HEADS + h, :] = qh
    q8 = q8_ref[...]
    s = jnp.sum(ck_ref[...] * q8[:, None], axis=-1, keepdims=True)
    mx = jnp.max(s, axis=1, keepdims=True)
    mx = jnp.maximum(mx[:, :, :XA_HEADS], mx[:, :, XA_HEADS:])
    e = jnp.exp(s - jnp.concatenate([mx, mx], axis=2))
    den = jnp.sum(e, axis=1)
    o8 = jnp.sum(e * cv_ref[...], axis=1)
    o4_ref[...] = (o8[:, :XA_HEADS] + o8[:, XA_HEADS:]) / (den[:, :XA_HEADS] + den[:, XA_HEADS:])
    for h in range(XA_HEADS):
        os_ref[pl.ds(o0, XA_SUB), h * XA_HEAD_DIM:(h + 1) * XA_HEAD_DIM] = o4_ref[:, h, :]

    @pl.when(sub == S_ROWS // XA_SUB - 1)
    def _():
        o_ref[0, P_ROWS:, :] = os_ref[...].astype(BF16)


def _xattn(z, zs, q_blk, mem_k, mem_v, layer, cache_k, cache_v):
    n_sub = S_ROWS // XA_SUB
    pair_shape = (DEPTH, DEC_BATCH, N_MEM // 2, 2 * XA_HEADS, XA_HEAD_DIM)
    cache_k, cache_v = cache_k.reshape(pair_shape), cache_v.reshape(pair_shape)
    cache_blk = (None, XA_SUB, N_MEM // 2, 2 * XA_HEADS, XA_HEAD_DIM)
    return pl.pallas_call(
        _xattn_kernel,
        grid=(N_TILES, n_sub),
        in_specs=[
            pl.BlockSpec((1, TILE, XA_WIDTH), lambda j, s: (j, 0, q_blk)),
            pl.BlockSpec((1, S_ROWS, XA_WIDTH), lambda j, s: (j, 0, q_blk)),
            pl.BlockSpec((None, N_MEM, XA_WIDTH), lambda j, s: (layer, j // 2, 0)),
            pl.BlockSpec((None, N_MEM, XA_WIDTH), lambda j, s: (layer, j // 2, 0)),
            pl.BlockSpec(cache_blk, lambda j, s: (layer, j * n_sub + s, 0, 0, 0)),
            pl.BlockSpec(cache_blk, lambda j, s: (layer, j * n_sub + s, 0, 0, 0)),
        ],
        out_specs=pl.BlockSpec((1, TILE, XA_WIDTH), lambda j, s: (j, 0, 0)),
        out_shape=jax.ShapeDtypeStruct((N_TILES, TILE, XA_WIDTH), BF16),
        scratch_shapes=[
            pltpu.VMEM((S_ROWS, XA_WIDTH), F32),
            pltpu.VMEM((XA_SUB, 2 * XA_HEADS, XA_HEAD_DIM), F32),
            pltpu.VMEM((XA_SUB, XA_HEADS, XA_HEAD_DIM), F32),
        ],
        compiler_params=_cparams(2, 52),
        name="xattn",
    )(z, zs, mem_k, mem_v, cache_k, cache_v)


def _s5_kernel(u_ref, bm_ref, cm_ref, lbr_ref, lbi_ref, d_ref, *rest):
    y_ref, hre_ref, him_ref, xa_ref, xb_ref, ha_ref, hb_ref, hs_ref = rest[-8:]
    tq = pl.program_id(1)

    @pl.when(tq == 0)
    def _():
        hs_ref[...] = jnp.zeros_like(hs_ref)

    bm, cm, d = bm_ref[0], cm_ref[0], d_ref[...]
    lbr = [lbr_ref[0, p:p + 1, :] for p in range(S5_SLABS)]
    lbi = [lbi_ref[0, p:p + 1, :] for p in range(S5_SLABS)]

    def seg_rows(ref, g):
        r0 = g * S5_SEG if isinstance(g, int) else pl.multiple_of(g * S5_SEG, S5_SEG)
        return r0, jnp.concatenate([ref[b, 0, pl.ds(r0, S5_SEG), :] for b in range(BATCH)], axis=0)

    def x_piece(g, dst):
        _, lhs = seg_rows(u_ref, g)
        x = jnp.dot(lhs, bm, preferred_element_type=F32)
        for b in range(BATCH):
            for s in range(2 * S5_SLABS):
                dst[s, b * S5_PITCH:b * S5_PITCH + S5_SEG, :] = x[b * S5_SEG:(b + 1) * S5_SEG, s * LANE:(s + 1) * LANE]

    def scan_seg(src, dst, carry):
        carry = list(carry)
        for tau in range(S5_SEG):
            rows = pl.ds(tau, BATCH, stride=S5_PITCH)
            for p in range(S5_SLABS):
                hr, hi = carry[2 * p], carry[2 * p + 1]
                nr = lbr[p] * hr - lbi[p] * hi + src[p, rows, :]
                ni = lbr[p] * hi + lbi[p] * hr + src[S5_SLABS + p, rows, :]
                dst[p, rows, :] = nr
                dst[S5_SLABS + p, rows, :] = ni
                carry[2 * p], carry[2 * p + 1] = nr, ni
        return tuple(carry)

    def y_piece(g, src):
        r0, u = seg_rows(u_ref, g)
        hcat = jnp.concatenate(
            [jnp.concatenate([src[s, b * S5_PITCH:b * S5_PITCH + S5_SEG, :].astype(BF16)
                              for s in range(2 * S5_SLABS)], axis=1) for b in range(BATCH)], axis=0)
        y = jax.nn.gelu(jnp.dot(hcat, cm, preferred_element_type=F32) + d * u.astype(F32))
        for b in range(BATCH):
            y_ref[b, 0, pl.ds(r0, S5_SEG), :] = y[b * S5_SEG:(b + 1) * S5_SEG]

    carry = []
    for p in range(S5_SLABS):
        carry += [hs_ref[p, :BATCH, :], hs_ref[S5_SLABS + p, :BATCH, :]]
    x_piece(0, xa_ref)
    x_piece(1, xb_ref)
    carry = scan_seg(xa_ref, ha_ref, tuple(carry))

    def body(k, carry):
        g = 2 * k + 1
        x_piece(g + 1, xa_ref)
        carry = scan_seg(xb_ref, hb_ref, carry)
        y_piece(g - 1, ha_ref)
        x_piece(g + 2, xb_ref)
        carry = scan_seg(xa_ref, ha_ref, carry)
        y_piece(g, hb_ref)
        return carry

    carry = lax.fori_loop(0, (S5_NSEG - 2) // 2, body, carry)
    carry = scan_seg(xb_ref, hb_ref, carry)
    y_piece(S5_NSEG - 2, ha_ref)
    y_piece(S5_NSEG - 1, hb_ref)
    for p in range(S5_SLABS):
        hs_ref[p, :BATCH, :] = carry[2 * p]
        hs_ref[S5_SLABS + p, :BATCH, :] = carry[2 * p + 1]
        hre_ref[:, p * LANE:(p + 1) * LANE] = carry[2 * p]
        him_ref[:, p * LANE:(p + 1) * LANE] = carry[2 * p + 1]


def _s5_prompt(z, prm, layer, prev):
    z4 = z.reshape(BATCH, 2, TILE, D_MODEL)
    blk = (BATCH, 1, S5_TQ, S5_CB)
    nst = SSM_GROUPS * SSM_STATE
    n_layers = prm["d"].shape[0]
    in_specs = [
        pl.BlockSpec(blk, lambda c, t: (0, t, 0, c)),
        pl.BlockSpec((None, 1, S5_CB, 2 * S5_ST), lambda c, t: (layer, c, 0, 0)),
        pl.BlockSpec((None, 1, 2 * S5_ST, S5_CB), lambda c, t: (layer, c, 0, 0)),
        pl.BlockSpec((None, 1, S5_SLABS, LANE), lambda c, t: (layer, c, 0, 0)),
        pl.BlockSpec((None, 1, S5_SLABS, LANE), lambda c, t: (layer, c, 0, 0)),
        pl.BlockSpec((None, 1, S5_CB), lambda c, t: (layer, 0, c)),
    ]
    args = [z4, prm["bm_hi"], prm["cm"], prm["lbr"].reshape(n_layers, S5_NCB, S5_SLABS, LANE),
            prm["lbi"].reshape(n_layers, S5_NCB, S5_SLABS, LANE), prm["d"]]
    aliases = {}
    if prev is not None:
        in_specs += [pl.BlockSpec(memory_space=pl.ANY), pl.BlockSpec(memory_space=pl.ANY)]
        aliases = {len(args): 1, len(args) + 1: 2}
        args += list(prev)
    y, hre, him = pl.pallas_call(
        _s5_kernel,
        grid=(S5_NCB, SEQ // S5_TQ),
        in_specs=in_specs,
        out_specs=[
            pl.BlockSpec(blk, lambda c, t: (0, t, 0, c)),
            pl.BlockSpec((None, BATCH, S5_ST), lambda c, t: (layer, 0, c)),
            pl.BlockSpec((None, BATCH, S5_ST), lambda c, t: (layer, 0, c)),
        ],
        out_shape=[
            jax.ShapeDtypeStruct((BATCH, 2, P_ROWS, TOK_WIDTH), F32),
            jax.ShapeDtypeStruct((n_layers, BATCH, nst), F32),
            jax.ShapeDtypeStruct((n_layers, BATCH, nst), F32),
        ],
        scratch_shapes=[pltpu.VMEM((2 * S5_SLABS, BATCH * S5_PITCH, LANE), F32)] * 4
        + [pltpu.VMEM((2 * S5_SLABS, 8, LANE), F32)],
        input_output_aliases=aliases,
        compiler_params=_cparams(2, 40),
        name="s5_prompt",
    )(*args)
    return y.reshape(N_TILES, P_ROWS, TOK_WIDTH), hre, him


def _s5_sample_kernel(u_ref, sre_ref, sim_ref, bh_ref, bl_ref, cm_ref, lbr_ref, lbi_ref, d_ref, *rest):
    y_ref, nre_ref, nim_ref = rest[-3:]
    u = u_ref[...].reshape(DEC_BATCH, S5_CB)
    uh = u.astype(BF16)
    ul = (u - uh.astype(F32)).astype(BF16)
    bh, bl = bh_ref[0], bl_ref[0]
    x = (jnp.dot(ul, bh, preferred_element_type=F32) + jnp.dot(uh, bl, preferred_element_type=F32)
         + jnp.dot(uh, bh, preferred_element_type=F32))
    lbr, lbi = lbr_ref[...], lbi_ref[...]
    sr, si = sre_ref[...], sim_ref[...]
    nr = lbr * sr - lbi * si + x[:, :S5_ST]
    ni = lbr * si + lbi * sr + x[:, S5_ST:]
    nre_ref[...] = nr
    nim_ref[...] = ni
    hcat = jnp.concatenate([nr.astype(BF16), ni.astype(BF16)], axis=1)
    y = jnp.dot(hcat, cm_ref[0], preferred_element_type=F32) + d_ref[...] * u
    y_ref[...] = jax.nn.gelu(y).reshape(N_TILES, S_ROWS, S5_CB)


def _s5_sample(zs, s_re, s_im, prm, layer, prev):
    nst = SSM_GROUPS * SSM_STATE
    n_layers = prm["d"].shape[0]
    st_spec = pl.BlockSpec((None, DEC_BATCH, S5_ST), lambda c: (layer, 0, c))
    in_specs = [
        pl.BlockSpec((N_TILES, S_ROWS, S5_CB), lambda c: (0, 0, c)),
        st_spec,
        st_spec,
        pl.BlockSpec((None, 1, S5_CB, 2 * S5_ST), lambda c: (layer, c, 0, 0)),
        pl.BlockSpec((None, 1, S5_CB, 2 * S5_ST), lambda c: (layer, c, 0, 0)),
        pl.BlockSpec((None, 1, 2 * S5_ST, S5_CB), lambda c: (layer, c, 0, 0)),
        pl.BlockSpec((None, 1, S5_ST), lambda c: (layer, 0, c)),
        pl.BlockSpec((None, 1, S5_ST), lambda c: (layer, 0, c)),
        pl.BlockSpec((None, 1, S5_CB), lambda c: (layer, 0, c)),
    ]
    args = [zs, s_re, s_im, prm["bm_hi"], prm["bm_lo"], prm["cm"], prm["lbr"].reshape(n_layers, 1, nst),
            prm["lbi"].reshape(n_layers, 1, nst), prm["d"]]
    aliases = {}
    if prev is not None:
        in_specs += [pl.BlockSpec(memory_space=pl.ANY), pl.BlockSpec(memory_space=pl.ANY)]
        aliases = {len(args): 1, len(args) + 1: 2}
        args += list(prev)
    return pl.pallas_call(
        _s5_sample_kernel,
        grid=(S5_NCB,),
        in_specs=in_specs,
        out_specs=[pl.BlockSpec((N_TILES, S_ROWS, S5_CB), lambda c: (0, 0, c)), st_spec, st_spec],
        out_shape=[
            jax.ShapeDtypeStruct((N_TILES, S_ROWS, TOK_WIDTH), F32),
            jax.ShapeDtypeStruct((n_layers, DEC_BATCH, nst), F32),
            jax.ShapeDtypeStruct((n_layers, DEC_BATCH, nst), F32),
        ],
        input_output_aliases=aliases,
        compiler_params=_cparams(1, 40),
        name="s5_sample",
    )(*args)


def _s5_params(lam_re, lam_im, log_dt, b_re, b_im, c_re, c_im, d):
    n_layers = lam_re.shape[0]
    dt = jnp.exp(log_dt)[..., None]
    ar, ai = lam_re * dt, lam_im * dt
    mag = jnp.exp(ar)
    lb_re, lb_im = mag * jnp.cos(ai), mag * jnp.sin(ai)
    nr, ni = lb_re - 1.0, lb_im
    den = lam_re * lam_re + lam_im * lam_im
    k_re = (nr * lam_re + ni * lam_im) / den
    k_im = (ni * lam_re - nr * lam_im) / den
    bb_re = k_re[..., None] * b_re - k_im[..., None] * b_im
    bb_im = k_re[..., None] * b_im + k_im[..., None] * b_re
    gpb = S5_CB // SSM_GROUP_DIM
    eye = jnp.eye(gpb, dtype=F32)

    def blockdiag_in(m):
        m = m.reshape(n_layers, S5_NCB, gpb, SSM_STATE, SSM_GROUP_DIM)
        return jnp.einsum("lngpc,gh->lngchp", m, eye).reshape(n_layers, S5_NCB, S5_CB, S5_ST)

    def blockdiag_out(m):
        m = m.reshape(n_layers, S5_NCB, gpb, SSM_GROUP_DIM, SSM_STATE)
        return jnp.einsum("lngcp,gh->lngphc", m, eye).reshape(n_layers, S5_NCB, S5_ST, S5_CB)

    bm = jnp.concatenate([blockdiag_in(bb_re), blockdiag_in(bb_im)], axis=3)
    bm_hi = bm.astype(BF16)
    bm_lo = (bm - bm_hi.astype(F32)).astype(BF16)
    cm = jnp.concatenate([blockdiag_out(c_re), -blockdiag_out(c_im)], axis=2).astype(BF16)
    return dict(bm_hi=bm_hi, bm_lo=bm_lo, cm=cm, lbr=lb_re, lbi=lb_im, d=d.reshape(n_layers, 1, TOK_WIDTH))


def _glu_kernel(yp_ref, ys_ref, w_ref, b_ref, o_ref, wbf_ref, ybf_ref, *, tn):
    @pl.when(pl.program_id(0) == 0)
    def _():
        wbf_ref[...] = w_ref[...].astype(BF16)

    ybf_ref[:P_ROWS, :] = yp_ref[0].astype(BF16)
    ybf_ref[P_ROWS:, :] = ys_ref[0].astype(BF16)
    for c in range(TOK_WIDTH // tn):
        cols = slice(c * tn, (c + 1) * tn)
        gate = jnp.dot(ybf_ref[...], wbf_ref[:, cols], preferred_element_type=F32) + b_ref[:, cols]
        sg = jax.nn.sigmoid(gate)
        o_ref[0, :P_ROWS, cols] = (yp_ref[0, :, cols] * sg[:P_ROWS]).astype(BF16)
        o_ref[0, P_ROWS:, cols] = (ys_ref[0, :, cols] * sg[P_ROWS:]).astype(BF16)


def _glu(y_p, y_s, w, b, layer):
    tn = 512
    return pl.pallas_call(
        functools.partial(_glu_kernel, tn=tn),
        grid=(N_TILES,),
        in_specs=[
            pl.BlockSpec((1, P_ROWS, TOK_WIDTH), lambda m: (m, 0, 0)),
            pl.BlockSpec((1, S_ROWS, TOK_WIDTH), lambda m: (m, 0, 0)),
            pl.BlockSpec((None, TOK_WIDTH, TOK_WIDTH), lambda m: (layer, 0, 0), pipeline_mode=pl.Buffered(1)),
            pl.BlockSpec((None, 1, TOK_WIDTH), lambda m: (layer, 0, 0)),
        ],
        out_specs=pl.BlockSpec((1, TILE, TOK_WIDTH), lambda m: (m, 0, 0)),
        out_shape=jax.ShapeDtypeStruct((N_TILES, TILE, TOK_WIDTH), BF16),
        scratch_shapes=[pltpu.VMEM((TOK_WIDTH, TOK_WIDTH), BF16), pltpu.VMEM((TILE, TOK_WIDTH), BF16)],
        compiler_params=_cparams(1, 48),
        name="glu",
    )(y_p, y_s, w, b)


PN = 256


def _proj_kernel(*refs, part_steps, final):
    n_parts = len(part_steps)
    parts = refs[:n_parts]
    w_ref, x_ref, g_ref = refs[n_parts:n_parts + 3]
    outs = refs[n_parts + 3:]
    if final:
        yp_ref, ys_ref, acc = outs
    else:
        xn_ref, h_ref = outs
        acc = xn_ref.at[0]
    k = pl.program_id(1)
    n_k = sum(part_steps)

    def accumulate(p_ref, first):
        lhs = p_ref[0]
        for n in range(D_MODEL // PN):
            cols = slice(n * PN, (n + 1) * PN)
            d = jnp.dot(lhs, w_ref[:, cols], preferred_element_type=F32)
            xc = x_ref[0, :, (n % (TK // PN)) * PN:(n % (TK // PN) + 1) * PN]
            d = d + jnp.where(k == n // (TK // PN), xc, 0.0)
            if first:
                acc[:, cols] = d
            else:
                acc[:, cols] += d

    spans = []
    start = 0
    for p_ref, cnt in zip(parts, part_steps):
        lo, hi = start, start + cnt
        if lo == 0:
            spans.append((0, 1, p_ref, True))
            lo = 1
        if lo < hi:
            spans.append((lo, hi, p_ref, False))
        start += cnt
    for lo, hi, p_ref, first in spans:
        @pl.when((k >= lo) & (k < hi))
        def _(p_ref=p_ref, first=first):
            accumulate(p_ref, first)

    @pl.when(k == n_k - 1)
    def _():
        h = _rms(acc[...], g_ref[...])
        if final:
            yp_ref[0] = h[:P_ROWS]
            ys_ref[0] = h[P_ROWS:]
        else:
            h_ref[0] = h.astype(BF16)


def _proj(parts, w, layer, x, g, g_idx, final=False):
    part_steps = tuple(p.shape[-1] // TK for p in parts)
    n_k = sum(part_steps)
    assert w.dtype == BF16 and n_k * TK == w.shape[1] and n_k >= D_MODEL // TK
    starts = [sum(part_steps[:i]) for i in range(len(parts))]
    in_specs = []
    for s0, cnt in zip(starts, part_steps):
        in_specs.append(pl.BlockSpec(
            (1, TILE, TK), lambda m, k, s0=s0, cnt=cnt: (m, 0, jnp.clip(k - s0, 0, cnt - 1))))
    in_specs += [
        pl.BlockSpec((None, TK, D_MODEL), lambda m, k: (layer, k, 0)),
        pl.BlockSpec((1, TILE, TK), lambda m, k: (m, 0, jnp.minimum(k, D_MODEL // TK - 1))),
        pl.BlockSpec((None, 1, D_MODEL), lambda m, k: (g_idx, 0, 0)),
    ]
    if final:
        out_specs = [
            pl.BlockSpec((1, P_ROWS, D_MODEL), lambda m, k: (m, 0, 0)),
            pl.BlockSpec((1, S_ROWS, D_MODEL), lambda m, k: (m, 0, 0)),
        ]
        out_shape = [
            jax.ShapeDtypeStruct((N_TILES, P_ROWS, D_MODEL), F32),
            jax.ShapeDtypeStruct((N_TILES, S_ROWS, D_MODEL), F32),
        ]
        scratch = [pltpu.VMEM((TILE, D_MODEL), F32)]
    else:
        out_specs = [
            pl.BlockSpec((1, TILE, D_MODEL), lambda m, k: (m, 0, 0)),
            pl.BlockSpec((1, TILE, D_MODEL), lambda m, k: (m, 0, 0)),
        ]
        out_shape = [
            jax.ShapeDtypeStruct((N_TILES, TILE, D_MODEL), F32),
            jax.ShapeDtypeStruct((N_TILES, TILE, D_MODEL), BF16),
        ]
        scratch = []
    return pl.pallas_call(
        functools.partial(_proj_kernel, part_steps=part_steps, final=final),
        grid=(N_TILES, n_k),
        in_specs=in_specs,
        out_specs=out_specs,
        out_shape=out_shape,
        scratch_shapes=scratch,
        compiler_params=_cparams(2, 52),
        name="proj_final" if final else "proj",
    )(*parts, w, x, g)


def _ffn_up_kernel(h_ref, wa_ref, wg_ref, cw_ref, cb_ref, sc_ref, wd_ref, *rest):
    y_ref, cp_ref, cs_ref, wdb_ref, wbf_ref, as_ref = rest[-6:]
    f = pl.program_id(0)

    @pl.when(f == FF_BLOCKS)
    def _():
        y_ref[...] = jnp.zeros(y_ref.shape, BF16)
        wdb_ref[...] = jnp.zeros(wdb_ref.shape, BF16)

    @pl.when(f < FF_BLOCKS)
    def _():
        wdb_ref[...] = wd_ref[...].astype(BF16)
        wbf_ref[:, :LANE] = wa_ref[...].astype(BF16)
        wbf_ref[:, LANE:] = wg_ref[...].astype(BF16)
        w0, w1, w2 = cw_ref[0:1, :], cw_ref[1:2, :], cw_ref[2:3, :]
        cb = cb_ref[...]
        for j in range(N_TILES):
            r = jnp.dot(h_ref[j], wbf_ref[...], preferred_element_type=F32)
            a, g = r[:, :LANE], r[:, LANE:]
            ap = a[:P_ROWS]
            if j % 2 == 0:
                as_ref[:CONV_PAD, :] = jnp.zeros((CONV_PAD, LANE), F32)
            else:
                as_ref[CONV_PAD - 2:CONV_PAD, :] = as_ref[CONV_PAD + P_ROWS - 2:, :]
            as_ref[CONV_PAD:, :] = ap
            a1 = as_ref[CONV_PAD - 1:CONV_PAD - 1 + P_ROWS, :]
            a2 = as_ref[CONV_PAD - 2:CONV_PAD - 2 + P_ROWS, :]
            c = cb + w0 * a2 + w1 * a1 + w2 * ap
            y_ref[j, :P_ROWS, :] = (jax.nn.silu(c) * g[:P_ROWS]).astype(BF16)
            if j % 2 == 1:
                cp_ref[j // 2] = as_ref[CONV_PAD + P_ROWS - 2:, :]
            a_s = a[P_ROWS:]
            q0 = sc_ref[j * S_ROWS:(j + 1) * S_ROWS, 0, :]
            q1 = sc_ref[j * S_ROWS:(j + 1) * S_ROWS, 1, :]
            cs = cb + w0 * q0 + w1 * q1 + w2 * a_s
            y_ref[j, P_ROWS:, :] = (jax.nn.silu(cs) * g[P_ROWS:]).astype(BF16)
            cs_ref[j * S_ROWS:(j + 1) * S_ROWS, 0, :] = q1
            cs_ref[j * S_ROWS:(j + 1) * S_ROWS, 1, :] = a_s


def _ffn_up(h, w_up, conv_w, conv_b, state_conv, w_down, layer, conv_p_prev, conv_s_prev):
    last = FF_BLOCKS - 1
    fc = lambda f: jnp.minimum(f, last)
    in_specs = [
        pl.BlockSpec((N_TILES, TILE, D_MODEL), lambda f: (0, 0, 0), pipeline_mode=pl.Buffered(1)),
        pl.BlockSpec((None, D_MODEL, LANE), lambda f: (layer, 0, fc(f))),
        pl.BlockSpec((None, D_MODEL, LANE), lambda f: (layer, 0, FF_BLOCKS + fc(f))),
        pl.BlockSpec((None, 3, LANE), lambda f: (layer, 0, fc(f))),
        pl.BlockSpec((None, 1, LANE), lambda f: (layer, 0, fc(f))),
        pl.BlockSpec((None, DEC_BATCH, 2, LANE), lambda f: (layer, 0, 0, fc(f))),
        pl.BlockSpec((None, LANE, D_MODEL), lambda f: (layer, fc(f), 0)),
    ]
    args = [h, w_up, w_up, conv_w, conv_b, state_conv, w_down]
    aliases = {}
    if conv_p_prev is not None:
        in_specs += [pl.BlockSpec(memory_space=pl.ANY), pl.BlockSpec(memory_space=pl.ANY)]
        aliases = {len(args): 1, len(args) + 1: 2}
        args += [conv_p_prev, conv_s_prev]
    return pl.pallas_call(
        _ffn_up_kernel,
        grid=(FF_BLOCKS + 1,),
        in_specs=in_specs,
        out_specs=[
            pl.BlockSpec((N_TILES, TILE, LANE), lambda f: (0, 0, f)),
            pl.BlockSpec((None, BATCH, 2, LANE), lambda f: (layer, 0, 0, fc(f))),
            pl.BlockSpec((None, DEC_BATCH, 2, LANE), lambda f: (layer, 0, 0, fc(f))),
            pl.BlockSpec((None, LANE, D_MODEL), lambda f: (0, f, 0)),
        ],
        out_shape=[
            jax.ShapeDtypeStruct((N_TILES, TILE, FF_PAD), BF16),
            jax.ShapeDtypeStruct((DEPTH, BATCH, 2, D_FF), F32),
            jax.ShapeDtypeStruct((DEPTH, DEC_BATCH, 2, D_FF), F32),
            jax.ShapeDtypeStruct((1, FF_PAD, D_MODEL), BF16),
        ],
        scratch_shapes=[pltpu.VMEM((D_MODEL, 2 * LANE), BF16), pltpu.VMEM((CONV_PAD + P_ROWS, LANE), F32)],
        input_output_aliases=aliases,
        compiler_params=_cparams(1, 56),
        name="ffn_up",
    )(*args)


def kernel(x_prompt, x_sample, mem_prompt, cache_mem_k, cache_mem_v, state_ssm_re, state_ssm_im, state_conv,
           g_mix, g_ffn, g_mem, g_final, w_mem_kv, sg_w_in, sg_w_out, sg_g_v, sg_w_s, sg_b_s, ssm_w_in,
           ssm_w_out, ssm_lam_re, ssm_lam_im, ssm_log_dt, ssm_b_re, ssm_b_im, ssm_c_re, ssm_c_im, ssm_d,
           ssm_w_glu, ssm_b_glu, ffn_w_up, ffn_conv_w, ffn_conv_b, ffn_w_down):
    n_sg, n_ssm = sg_w_in.shape[0], ssm_w_in.shape[0]
    nst = SSM_GROUPS * SSM_STATE
    g_mix3, g_ffn3 = g_mix.reshape(DEPTH, 1, D_MODEL), g_ffn.reshape(DEPTH, 1, D_MODEL)
    g_fin3 = g_final.reshape(1, 1, D_MODEL)
    g_v3 = sg_g_v.reshape(n_sg, 1, TOK_WIDTH)
    sg_bias = jnp.repeat(jnp.swapaxes(sg_b_s, 1, 2), LANE, axis=2)
    sg_coef = jnp.repeat(sg_w_s[:, :, 0, 0], LANE, axis=1).reshape(n_sg, 1, TOK_WIDTH)
    b_glu3 = ssm_b_glu.reshape(n_ssm, 1, TOK_WIDTH)
    conv_b3 = ffn_conv_b.reshape(DEPTH, 1, D_FF)
    prm = _s5_params(ssm_lam_re, ssm_lam_im, ssm_log_dt, ssm_b_re, ssm_b_im, ssm_c_re, ssm_c_im, ssm_d)
    s_re, s_im = state_ssm_re.reshape(n_ssm, DEC_BATCH, nst), state_ssm_im.reshape(n_ssm, DEC_BATCH, nst)
    sg_w_out_bf, ssm_w_out_bf = sg_w_out.astype(BF16), ssm_w_out.astype(BF16)

    mem_k, mem_v = _mem_kv(mem_prompt, g_mem, w_mem_kv)
    x, h = _prep(x_prompt, x_sample, g_mix3)
    sg_v = []
    conv_p = conv_s = st_p = st_s = None
    y_prompt = y_sample = None
    for i in range(DEPTH):
        j = i // 2
        if i % 2 == 0:
            uv, uvs = _in_proj(h, sg_w_in, j, 0, 2 * TOK_WIDTH, tn=1024, act=True)
            q, qs = _in_proj(h, sg_w_in, j, 2 * TOK_WIDTH, XA_WIDTH, tn=512, act=False)
            tok, v = _sg_gate(uv, uvs, g_v3, sg_w_s, sg_bias, sg_coef, j)
            sg_v.append(v.reshape(DEC_BATCH, 1, TOK_WIDTH))
            xa = _xattn(q, qs, 0, mem_k, mem_v, i, cache_mem_k, cache_mem_v)
            x, h = _proj([tok, xa], sg_w_out_bf, j, x, g_ffn3, i)
        else:
            z, zs = _in_proj(h, ssm_w_in, j, 0, D_MODEL, tn=1024, act=False)
            y_p, *st_p = _s5_prompt(z, prm, j, st_p)
            y_s, *st_s = _s5_sample(zs, s_re, s_im, prm, j, st_s)
            yg = _glu(y_p, y_s, ssm_w_glu, b_glu3, j)
            xa = _xattn(z, zs, TOK_WIDTH // XA_WIDTH, mem_k, mem_v, i, cache_mem_k, cache_mem_v)
            x, h = _proj([yg, xa], ssm_w_out_bf, j, x, g_ffn3, i)
        yf, conv_p, conv_s, w_dn = _ffn_up(h, ffn_w_up, ffn_conv_w, conv_b3, state_conv, ffn_w_down, i,
                                           conv_p, conv_s)
        if i + 1 < DEPTH:
            x, h = _proj([yf], w_dn, 0, x, g_mix3, i + 1)
        else:
            y_prompt, y_sample = _proj([yf], w_dn, 0, x, g_fin3, 0, final=True)
    mem_k = mem_k.reshape(DEPTH, BATCH, N_MEM, XA_HEADS, XA_HEAD_DIM)
    mem_v = mem_v.reshape(DEPTH, BATCH, N_MEM, XA_HEADS, XA_HEAD_DIM)
    st4 = lambda a, b: a.reshape(n_ssm, b, SSM_GROUPS, SSM_STATE)
    return (y_prompt.reshape(BATCH, SEQ, D_MODEL), y_sample.reshape(DEC_BATCH, 1, D_MODEL), mem_k, mem_v,
            st4(st_p[0], BATCH), st4(st_p[1], BATCH), conv_p,
            st4(st_s[0], DEC_BATCH), st4(st_s[1], DEC_BATCH), conv_s, jnp.stack(sg_v))
```

```python
import functools
import math

import jax
import jax.numpy as jnp
from jax import lax
from jax.experimental import pallas as pl
from jax.experimental.pallas import tpu as pltpu

F32 = jnp.float32
BF16 = jnp.bfloat16

D_MODEL = 2048
BATCH = 4
SEQ = 2048
DEPTH = 4
DEC_BATCH = 128
N_MEM = 256
XA_HEADS = 4
XA_HEAD_DIM = 128
XA_WIDTH = 512
TOK_WIDTH = 1536
CHUNK = 128
SG_GROUPS = 12
SSM_GROUPS = 96
SSM_GROUP_DIM = 16
SSM_STATE = 64
D_FF = 5504
EPS = 1e-6

N_TILES = 8
P_ROWS = 1024
S_ROWS = 16
TILE = P_ROWS + S_ROWS

LANE = 128
FF_BLOCKS = D_FF // LANE
FF_PAD = (FF_BLOCKS + 1) * LANE
TK = 512
CONV_PAD = 8

S5_CB = 256
S5_NCB = TOK_WIDTH // S5_CB
S5_ST = (S5_CB // SSM_GROUP_DIM) * SSM_STATE
S5_SLABS = S5_ST // LANE
S5_TQ = P_ROWS
S5_SEG = 64
S5_NSEG = S5_TQ // S5_SEG
S5_PITCH = S5_SEG + 8
V7X_VMEM_LIMIT = 56 * 1024 * 1024


def _cparams(n_axes, vmem_mb=None):
    kw = dict(dimension_semantics=("arbitrary",) * n_axes)
    if vmem_mb is not None:
        kw["vmem_limit_bytes"] = min(int(vmem_mb * 1024 * 1024), V7X_VMEM_LIMIT)
    return pltpu.CompilerParams(**kw)


def _rms(x, g):
    return x * lax.rsqrt(jnp.mean(x * x, axis=-1, keepdims=True) + EPS) * g


def _prep_kernel(xp_ref, xs_ref, g_ref, x_ref, h_ref):
    xp = xp_ref[0]
    xs = xs_ref[0]
    g = g_ref[...]
    x_ref[0, :P_ROWS] = xp
    x_ref[0, P_ROWS:] = xs
    h_ref[0, :P_ROWS] = _rms(xp, g).astype(BF16)
    h_ref[0, P_ROWS:] = _rms(xs, g).astype(BF16)


def _prep(x_prompt, x_sample, g):
    xp = x_prompt.reshape(N_TILES, P_ROWS, D_MODEL)
    xs = x_sample.reshape(N_TILES, S_ROWS, D_MODEL)
    return pl.pallas_call(
        _prep_kernel,
        grid=(N_TILES,),
        in_specs=[
            pl.BlockSpec((1, P_ROWS, D_MODEL), lambda j: (j, 0, 0)),
            pl.BlockSpec((1, S_ROWS, D_MODEL), lambda j: (j, 0, 0)),
            pl.BlockSpec((None, 1, D_MODEL), lambda j: (0, 0, 0)),
        ],
        out_specs=[
            pl.BlockSpec((1, TILE, D_MODEL), lambda j: (j, 0, 0)),
            pl.BlockSpec((1, TILE, D_MODEL), lambda j: (j, 0, 0)),
        ],
        out_shape=[
            jax.ShapeDtypeStruct((N_TILES, TILE, D_MODEL), F32),
            jax.ShapeDtypeStruct((N_TILES, TILE, D_MODEL), BF16),
        ],
        compiler_params=_cparams(1, 52),
        name="prep",
    )(xp, xs, g)


def _mem_kv_kernel(m_ref, g_ref, w_ref, k_ref, v_ref):
    h = _rms(m_ref[...], g_ref[0]).astype(BF16)
    r = jnp.dot(h, w_ref[0].astype(BF16), preferred_element_type=F32)

    def put(o_ref):
        for b in range(BATCH):
            for hd in range(XA_HEADS):
                o_ref[b, :, hd, :] = r[b * N_MEM:(b + 1) * N_MEM, hd * XA_HEAD_DIM:(hd + 1) * XA_HEAD_DIM]

    @pl.when(pl.program_id(1) == 0)
    def _():
        put(k_ref)

    @pl.when(pl.program_id(1) == 1)
    def _():
        put(v_ref)


def _mem_kv(mem_prompt, g_mem, w_mem_kv):
    rows = BATCH * N_MEM
    mem = mem_prompt.reshape(rows, D_MODEL)
    kv_shape = (DEPTH, BATCH, N_MEM, XA_HEADS, XA_HEAD_DIM)
    kv_spec = pl.BlockSpec((None,) + kv_shape[1:], lambda i, n: (i, 0, 0, 0, 0))
    return pl.pallas_call(
        _mem_kv_kernel,
        grid=(DEPTH, 2),
        in_specs=[
            pl.BlockSpec((rows, D_MODEL), lambda i, n: (0, 0)),
            pl.BlockSpec((1, 1, D_MODEL), lambda i, n: (i, 0, 0)),
            pl.BlockSpec((1, D_MODEL, XA_WIDTH), lambda i, n: (i, 0, n)),
        ],
        out_specs=[kv_spec, kv_spec],
        out_shape=[jax.ShapeDtypeStruct(kv_shape, F32), jax.ShapeDtypeStruct(kv_shape, F32)],
        compiler_params=_cparams(2, 48),
        name="mem_kv",
    )(mem, g_mem.reshape(DEPTH, 1, D_MODEL), w_mem_kv)


def _in_proj_kernel(h_ref, w_ref, z_ref, zs_ref, wbf_ref, *, act):
    @pl.when(pl.program_id(1) == 0)
    def _():
        wbf_ref[...] = w_ref[...].astype(BF16)

    r = jnp.dot(h_ref[0], wbf_ref[...], preferred_element_type=F32)
    if act:
        r = jax.nn.gelu(r)
    z_ref[0] = r.astype(BF16)
    zs_ref[0] = r[P_ROWS:]


def _in_proj(h, w, layer, col0, width, tn, act):
    assert col0 % tn == 0 and width % tn == 0
    c0 = col0 // tn
    return pl.pallas_call(
        functools.partial(_in_proj_kernel, act=act),
        grid=(width // tn, N_TILES),
        in_specs=[
            pl.BlockSpec((1, TILE, D_MODEL), lambda n, m: (m, 0, 0)),
            pl.BlockSpec((None, D_MODEL, tn), lambda n, m: (layer, 0, c0 + n)),
        ],
        out_specs=[
            pl.BlockSpec((1, TILE, tn), lambda n, m: (m, 0, n)),
            pl.BlockSpec((1, S_ROWS, tn), lambda n, m: (m, 0, n)),
        ],
        out_shape=[
            jax.ShapeDtypeStruct((N_TILES, TILE, width), BF16),
            jax.ShapeDtypeStruct((N_TILES, S_ROWS, width), F32),
        ],
        scratch_shapes=[pltpu.VMEM((D_MODEL, tn), BF16)],
        compiler_params=_cparams(2, 48),
        name="in_proj",
    )(h, w)


def _sg_gate_kernel(u_ref, v_ref, us_ref, vs_ref, gv_ref, ws_ref, bias_ref, coef_ref, tok_ref, sgv_ref, vn_ref):
    gv = gv_ref[...]
    vn_ref[...] = _rms(v_ref[0, :P_ROWS, :].astype(F32), gv)
    row = lax.broadcasted_iota(jnp.int32, (CHUNK, CHUNK), 0)
    col = lax.broadcasted_iota(jnp.int32, (CHUNK, CHUNK), 1)
    causal = col <= row
    wms = [jnp.where(causal, ws_ref[g], 0.0).astype(BF16) for g in range(SG_GROUPS)]

    def chunk(c, carry):
        r0 = pl.multiple_of(c * CHUNK, CHUNK)
        for g in range(SG_GROUPS):
            cols = slice(g * LANE, (g + 1) * LANE)
            blk = vn_ref[pl.ds(r0, CHUNK), cols].astype(BF16)
            s = jnp.dot(wms[g], blk, preferred_element_type=F32) + bias_ref[:, cols]
            tok_ref[0, pl.ds(r0, CHUNK), cols] = (u_ref[0, pl.ds(r0, CHUNK), cols].astype(F32) * s).astype(BF16)
        return carry

    lax.fori_loop(0, P_ROWS // CHUNK, chunk, 0)
    vs = _rms(vs_ref[0], gv)
    sgv_ref[0] = vs
    s = coef_ref[...] * vs + bias_ref[0:1, :]
    tok_ref[0, P_ROWS:, :] = (us_ref[0] * s).astype(BF16)


def _sg_gate(uv, uvs, g_v, w_s, bias, coef, layer):
    par = lambda shape: pl.BlockSpec((None,) + shape, lambda j: (layer,) + (0,) * len(shape))
    return pl.pallas_call(
        _sg_gate_kernel,
        grid=(N_TILES,),
        in_specs=[
            pl.BlockSpec((1, TILE, TOK_WIDTH), lambda j: (j, 0, 0)),
            pl.BlockSpec((1, TILE, TOK_WIDTH), lambda j: (j, 0, 1)),
            pl.BlockSpec((1, S_ROWS, TOK_WIDTH), lambda j: (j, 0, 0)),
            pl.BlockSpec((1, S_ROWS, TOK_WIDTH), lambda j: (j, 0, 1)),
            par((1, TOK_WIDTH)),
            par((SG_GROUPS, CHUNK, CHUNK)),
            par((CHUNK, TOK_WIDTH)),
            par((1, TOK_WIDTH)),
        ],
        out_specs=[
            pl.BlockSpec((1, TILE, TOK_WIDTH), lambda j: (j, 0, 0)),
            pl.BlockSpec((1, S_ROWS, TOK_WIDTH), lambda j: (j, 0, 0)),
        ],
        out_shape=[
            jax.ShapeDtypeStruct((N_TILES, TILE, TOK_WIDTH), BF16),
            jax.ShapeDtypeStruct((N_TILES, S_ROWS, TOK_WIDTH), F32),
        ],
        scratch_shapes=[pltpu.VMEM((P_ROWS, TOK_WIDTH), F32)],
        compiler_params=_cparams(1, 40),
        name="sg_gate",
    )(uv, uv, uvs, uvs, g_v, w_s, bias, coef)


XA_SUB = 8


def _xattn_kernel(q_ref, qs_ref, mk_ref, mv_ref, ck_ref, cv_ref, w_ref, o_ref, wb_ref, os_ref, q8_ref, o4_ref):
    scale = XA_HEAD_DIM ** -0.5
    sub = pl.program_id(1)
    wb_ref[...] = w_ref[...].astype(BF16)

    @pl.when(sub == 0)
    def _():
        for h in range(XA_HEADS):
            cols = slice(h * XA_HEAD_DIM, (h + 1) * XA_HEAD_DIM)
            qh = q_ref[0, :P_ROWS, cols]
            kh = mk_ref[:, h, :].astype(BF16)
            vh = mv_ref[:, h, :].astype(BF16)
            s = lax.dot_general(qh, kh, (((1,), (1,)), ((), ())), preferred_element_type=F32) * scale
            s = s - jnp.max(s, axis=-1, keepdims=True)
            e = jnp.exp(s)
            p = (e / jnp.sum(e, axis=-1, keepdims=True)).astype(BF16)
            o_ref[0, :P_ROWS, cols] = jnp.dot(p, vh, preferred_element_type=F32).astype(BF16)

    o0 = pl.multiple_of(sub * XA_SUB, XA_SUB)
    qs = qs_ref[0, pl.ds(o0, XA_SUB), :] * scale
    for h in range(XA_HEADS):
        qh = qs[:, h * XA_HEAD_DIM:(h + 1) * XA_HEAD_DIM]
        q8_ref[:, h, :] = qh
        q8_ref[:, XA_HEADS + h, :] = qh
    q8 = q8_ref[...]
    s = jnp.sum(ck_ref[...] * q8[:, None], axis=-1, keepdims=True)
    mx = jnp.max(s, axis=1, keepdims=True)
    mx = jnp.maximum(mx[:, :, :XA_HEADS], mx[:, :, XA_HEADS:])
    e = jnp.exp(s - jnp.concatenate([mx, mx], axis=2))
    den = jnp.sum(e, axis=1)
    o8 = jnp.sum(e * cv_ref[...], axis=1)
    o4_ref[...] = (o8[:, :XA_HEADS] + o8[:, XA_HEADS:]) / (den[:, :XA_HEADS] + den[:, XA_HEADS:])
    for h in range(XA_HEADS):
        os_ref[pl.ds(o0, XA_SUB), h * XA_HEAD_DIM:(h + 1) * XA_HEAD_DIM] = o4_ref[:, h, :]

    @pl.when(sub == S_ROWS // XA_SUB - 1)
    def _():
        o_ref[0, P_ROWS:, :] = os_ref[...].astype(BF16)


def _xattn(z, zs, q_blk, mem_k, mem_v, layer, cache_k, cache_v, w_out, w_layer):
    n_sub = S_ROWS // XA_SUB
    w_rows = D_MODEL // (N_TILES * n_sub)
    pair_shape = (DEPTH, DEC_BATCH, N_MEM // 2, 2 * XA_HEADS, XA_HEAD_DIM)
    cache_k, cache_v = cache_k.reshape(pair_shape), cache_v.reshape(pair_shape)
    cache_blk = (None, XA_SUB, N_MEM // 2, 2 * XA_HEADS, XA_HEAD_DIM)
    return pl.pallas_call(
        _xattn_kernel,
        grid=(N_TILES, n_sub),
        in_specs=[
            pl.BlockSpec((1, TILE, XA_WIDTH), lambda j, s: (j, 0, q_blk)),
            pl.BlockSpec((1, S_ROWS, XA_WIDTH), lambda j, s: (j, 0, q_blk)),
            pl.BlockSpec((None, None, N_MEM, XA_HEADS, XA_HEAD_DIM), lambda j, s: (layer, j // 2, 0, 0, 0)),
            pl.BlockSpec((None, None, N_MEM, XA_HEADS, XA_HEAD_DIM), lambda j, s: (layer, j // 2, 0, 0, 0)),
            pl.BlockSpec(cache_blk, lambda j, s: (layer, j * n_sub + s, 0, 0, 0)),
            pl.BlockSpec(cache_blk, lambda j, s: (layer, j * n_sub + s, 0, 0, 0)),
            pl.BlockSpec((None, w_rows, D_MODEL), lambda j, s: (w_layer, j * n_sub + s, 0)),
        ],
        out_specs=[
            pl.BlockSpec((1, TILE, XA_WIDTH), lambda j, s: (j, 0, 0)),
            pl.BlockSpec((None, w_rows, D_MODEL), lambda j, s: (0, j * n_sub + s, 0)),
        ],
        out_shape=[
            jax.ShapeDtypeStruct((N_TILES, TILE, XA_WIDTH), BF16),
            jax.ShapeDtypeStruct((1, D_MODEL, D_MODEL), BF16),
        ],
        scratch_shapes=[
            pltpu.VMEM((S_ROWS, XA_WIDTH), F32),
            pltpu.VMEM((XA_SUB, 2 * XA_HEADS, XA_HEAD_DIM), F32),
            pltpu.VMEM((XA_SUB, XA_HEADS, XA_HEAD_DIM), F32),
        ],
        compiler_params=_cparams(2, 52),
        name="xattn",
    )(z, zs, mem_k, mem_v, cache_k, cache_v, w_out)


def _s5_kernel(u_ref, bm_ref, cm_ref, lbr_ref, lbi_ref, d_ref, *rest):
    y_ref, hre_ref, him_ref, xa_ref, xb_ref, ha_ref, hb_ref, hs_ref = rest[-8:]
    tq = pl.program_id(1)

    @pl.when(tq == 0)
    def _():
        hs_ref[...] = jnp.zeros_like(hs_ref)

    bm, cm, d = bm_ref[0], cm_ref[0], d_ref[...]
    lbr = [lbr_ref[0, p:p + 1, :] for p in range(S5_SLABS)]
    lbi = [lbi_ref[0, p:p + 1, :] for p in range(S5_SLABS)]

    def seg_rows(ref, g):
        r0 = g * S5_SEG if isinstance(g, int) else pl.multiple_of(g * S5_SEG, S5_SEG)
        return r0, jnp.concatenate([ref[b, 0, pl.ds(r0, S5_SEG), :] for b in range(BATCH)], axis=0)

    def x_piece(g, dst):
        _, lhs = seg_rows(u_ref, g)
        x = jnp.dot(lhs, bm, preferred_element_type=F32)
        for b in range(BATCH):
            for s in range(2 * S5_SLABS):
                dst[s, b * S5_PITCH:b * S5_PITCH + S5_SEG, :] = x[b * S5_SEG:(b + 1) * S5_SEG, s * LANE:(s + 1) * LANE]

    def scan_seg(src, dst, carry):
        carry = list(carry)
        for tau in range(S5_SEG):
            rows = pl.ds(tau, BATCH, stride=S5_PITCH)
            for p in range(S5_SLABS):
                hr, hi = carry[2 * p], carry[2 * p + 1]
                nr = lbr[p] * hr - lbi[p] * hi + src[p, rows, :]
                ni = lbr[p] * hi + lbi[p] * hr + src[S5_SLABS + p, rows, :]
                dst[p, rows, :] = nr
                dst[S5_SLABS + p, rows, :] = ni
                carry[2 * p], carry[2 * p + 1] = nr, ni
        return tuple(carry)

    def y_piece(g, src):
        r0, u = seg_rows(u_ref, g)
        hcat = jnp.concatenate(
            [jnp.concatenate([src[s, b * S5_PITCH:b * S5_PITCH + S5_SEG, :].astype(BF16)
                              for s in range(2 * S5_SLABS)], axis=1) for b in range(BATCH)], axis=0)
        y = jax.nn.gelu(jnp.dot(hcat, cm, preferred_element_type=F32) + d * u.astype(F32))
        for b in range(BATCH):
            y_ref[b, 0, pl.ds(r0, S5_SEG), :] = y[b * S5_SEG:(b + 1) * S5_SEG]

    carry = []
    for p in range(S5_SLABS):
        carry += [hs_ref[p, :BATCH, :], hs_ref[S5_SLABS + p, :BATCH, :]]
    x_piece(0, xa_ref)
    x_piece(1, xb_ref)
    carry = scan_seg(xa_ref, ha_ref, tuple(carry))

    def body(k, carry):
        g = 2 * k + 1
        x_piece(g + 1, xa_ref)
        carry = scan_seg(xb_ref, hb_ref, carry)
        y_piece(g - 1, ha_ref)
        x_piece(g + 2, xb_ref)
        carry = scan_seg(xa_ref, ha_ref, carry)
        y_piece(g, hb_ref)
        return carry

    carry = lax.fori_loop(0, (S5_NSEG - 2) // 2, body, carry)
    carry = scan_seg(xb_ref, hb_ref, carry)
    y_piece(S5_NSEG - 2, ha_ref)
    y_piece(S5_NSEG - 1, hb_ref)
    for p in range(S5_SLABS):
        hs_ref[p, :BATCH, :] = carry[2 * p]
        hs_ref[S5_SLABS + p, :BATCH, :] = carry[2 * p + 1]
        hre_ref[:, p * LANE:(p + 1) * LANE] = carry[2 * p]
        him_ref[:, p * LANE:(p + 1) * LANE] = carry[2 * p + 1]


def _s5_prompt(z, prm, layer, prev):
    z4 = z.reshape(BATCH, 2, TILE, D_MODEL)
    blk = (BATCH, 1, S5_TQ, S5_CB)
    nst = SSM_GROUPS * SSM_STATE
    n_layers = prm["d"].shape[0]
    in_specs = [
        pl.BlockSpec(blk, lambda c, t: (0, t, 0, c)),
        pl.BlockSpec((None, 1, S5_CB, 2 * S5_ST), lambda c, t: (layer, c, 0, 0)),
        pl.BlockSpec((None, 1, 2 * S5_ST, S5_CB), lambda c, t: (layer, c, 0, 0)),
        pl.BlockSpec((None, 1, S5_SLABS, LANE), lambda c, t: (layer, c, 0, 0)),
        pl.BlockSpec((None, 1, S5_SLABS, LANE), lambda c, t: (layer, c, 0, 0)),
        pl.BlockSpec((None, 1, S5_CB), lambda c, t: (layer, 0, c)),
    ]
    args = [z4, prm["bm_hi"], prm["cm"], prm["lbr"].reshape(n_layers, S5_NCB, S5_SLABS, LANE),
            prm["lbi"].reshape(n_layers, S5_NCB, S5_SLABS, LANE), prm["d"]]
    aliases = {}
    if prev is not None:
        in_specs += [pl.BlockSpec(memory_space=pl.ANY), pl.BlockSpec(memory_space=pl.ANY)]
        aliases = {len(args): 1, len(args) + 1: 2}
        args += list(prev)
    y, hre, him = pl.pallas_call(
        _s5_kernel,
        grid=(S5_NCB, SEQ // S5_TQ),
        in_specs=in_specs,
        out_specs=[
            pl.BlockSpec(blk, lambda c, t: (0, t, 0, c)),
            pl.BlockSpec((None, BATCH, S5_ST), lambda c, t: (layer, 0, c)),
            pl.BlockSpec((None, BATCH, S5_ST), lambda c, t: (layer, 0, c)),
        ],
        out_shape=[
            jax.ShapeDtypeStruct((BATCH, 2, P_ROWS, TOK_WIDTH), F32),
            jax.ShapeDtypeStruct((n_layers, BATCH, nst), F32),
            jax.ShapeDtypeStruct((n_layers, BATCH, nst), F32),
        ],
        scratch_shapes=[pltpu.VMEM((2 * S5_SLABS, BATCH * S5_PITCH, LANE), F32)] * 4
        + [pltpu.VMEM((2 * S5_SLABS, 8, LANE), F32)],
        input_output_aliases=aliases,
        compiler_params=_cparams(2, 40),
        name="s5_prompt",
    )(*args)
    return y.reshape(N_TILES, P_ROWS, TOK_WIDTH), hre, him


def _s5_sample_kernel(u_ref, sre_ref, sim_ref, bh_ref, bl_ref, cm_ref, lbr_ref, lbi_ref, d_ref, *rest):
    y_ref, nre_ref, nim_ref = rest[-3:]
    u = u_ref[...].reshape(DEC_BATCH, S5_CB)
    uh = u.astype(BF16)
    ul = (u - uh.astype(F32)).astype(BF16)
    bh, bl = bh_ref[0], bl_ref[0]
    x = (jnp.dot(ul, bh, preferred_element_type=F32) + jnp.dot(uh, bl, preferred_element_type=F32)
         + jnp.dot(uh, bh, preferred_element_type=F32))
    lbr, lbi = lbr_ref[...], lbi_ref[...]
    sr, si = sre_ref[...], sim_ref[...]
    nr = lbr * sr - lbi * si + x[:, :S5_ST]
    ni = lbr * si + lbi * sr + x[:, S5_ST:]
    nre_ref[...] = nr
    nim_ref[...] = ni
    hcat = jnp.concatenate([nr.astype(BF16), ni.astype(BF16)], axis=1)
    y = jnp.dot(hcat, cm_ref[0], preferred_element_type=F32) + d_ref[...] * u
    y_ref[...] = jax.nn.gelu(y).reshape(N_TILES, S_ROWS, S5_CB)


def _s5_sample(zs, s_re, s_im, prm, layer, prev):
    nst = SSM_GROUPS * SSM_STATE
    n_layers = prm["d"].shape[0]
    st_spec = pl.BlockSpec((None, DEC_BATCH, S5_ST), lambda c: (layer, 0, c))
    in_specs = [
        pl.BlockSpec((N_TILES, S_ROWS, S5_CB), lambda c: (0, 0, c)),
        st_spec,
        st_spec,
        pl.BlockSpec((None, 1, S5_CB, 2 * S5_ST), lambda c: (layer, c, 0, 0)),
        pl.BlockSpec((None, 1, S5_CB, 2 * S5_ST), lambda c: (layer, c, 0, 0)),
        pl.BlockSpec((None, 1, 2 * S5_ST, S5_CB), lambda c: (layer, c, 0, 0)),
        pl.BlockSpec((None, 1, S5_ST), lambda c: (layer, 0, c)),
        pl.BlockSpec((None, 1, S5_ST), lambda c: (layer, 0, c)),
        pl.BlockSpec((None, 1, S5_CB), lambda c: (layer, 0, c)),
    ]
    args = [zs, s_re, s_im, prm["bm_hi"], prm["bm_lo"], prm["cm"], prm["lbr"].reshape(n_layers, 1, nst),
            prm["lbi"].reshape(n_layers, 1, nst), prm["d"]]
    aliases = {}
    if prev is not None:
        in_specs += [pl.BlockSpec(memory_space=pl.ANY), pl.BlockSpec(memory_space=pl.ANY)]
        aliases = {len(args): 1, len(args) + 1: 2}
        args += list(prev)
    return pl.pallas_call(
        _s5_sample_kernel,
        grid=(S5_NCB,),
        in_specs=in_specs,
        out_specs=[pl.BlockSpec((N_TILES, S_ROWS, S5_CB), lambda c: (0, 0, c)), st_spec, st_spec],
        out_shape=[
            jax.ShapeDtypeStruct((N_TILES, S_ROWS, TOK_WIDTH), F32),
            jax.ShapeDtypeStruct((n_layers, DEC_BATCH, nst), F32),
            jax.ShapeDtypeStruct((n_layers, DEC_BATCH, nst), F32),
        ],
        input_output_aliases=aliases,
        compiler_params=_cparams(1, 40),
        name="s5_sample",
    )(*args)


def _s5_params(lam_re, lam_im, log_dt, b_re, b_im, c_re, c_im, d):
    n_layers = lam_re.shape[0]
    dt = jnp.exp(log_dt)[..., None]
    ar, ai = lam_re * dt, lam_im * dt
    mag = jnp.exp(ar)
    lb_re, lb_im = mag * jnp.cos(ai), mag * jnp.sin(ai)
    nr, ni = lb_re - 1.0, lb_im
    den = lam_re * lam_re + lam_im * lam_im
    k_re = (nr * lam_re + ni * lam_im) / den
    k_im = (ni * lam_re - nr * lam_im) / den
    bb_re = k_re[..., None] * b_re - k_im[..., None] * b_im
    bb_im = k_re[..., None] * b_im + k_im[..., None] * b_re
    gpb = S5_CB // SSM_GROUP_DIM
    eye = jnp.eye(gpb, dtype=F32)

    def blockdiag_in(m):
        m = m.reshape(n_layers, S5_NCB, gpb, SSM_STATE, SSM_GROUP_DIM)
        return jnp.einsum("lngpc,gh->lngchp", m, eye).reshape(n_layers, S5_NCB, S5_CB, S5_ST)

    def blockdiag_out(m):
        m = m.reshape(n_layers, S5_NCB, gpb, SSM_GROUP_DIM, SSM_STATE)
        return jnp.einsum("lngcp,gh->lngphc", m, eye).reshape(n_layers, S5_NCB, S5_ST, S5_CB)

    bm = jnp.concatenate([blockdiag_in(bb_re), blockdiag_in(bb_im)], axis=3)
    bm_hi = bm.astype(BF16)
    bm_lo = (bm - bm_hi.astype(F32)).astype(BF16)
    cm = jnp.concatenate([blockdiag_out(c_re), -blockdiag_out(c_im)], axis=2).astype(BF16)
    return dict(bm_hi=bm_hi, bm_lo=bm_lo, cm=cm, lbr=lb_re, lbi=lb_im, d=d.reshape(n_layers, 1, TOK_WIDTH))


def _glu_kernel(yp_ref, ys_ref, w_ref, b_ref, o_ref, wbf_ref, ybf_ref, *, tn):
    @pl.when(pl.program_id(0) == 0)
    def _():
        wbf_ref[...] = w_ref[...].astype(BF16)

    ybf_ref[:P_ROWS, :] = yp_ref[0].astype(BF16)
    ybf_ref[P_ROWS:, :] = ys_ref[0].astype(BF16)
    for c in range(TOK_WIDTH // tn):
        cols = slice(c * tn, (c + 1) * tn)
        gate = jnp.dot(ybf_ref[...], wbf_ref[:, cols], preferred_element_type=F32) + b_ref[:, cols]
        sg = jax.nn.sigmoid(gate)
        o_ref[0, :P_ROWS, cols] = (yp_ref[0, :, cols] * sg[:P_ROWS]).astype(BF16)
        o_ref[0, P_ROWS:, cols] = (ys_ref[0, :, cols] * sg[P_ROWS:]).astype(BF16)


def _glu(y_p, y_s, w, b, layer):
    tn = 512
    return pl.pallas_call(
        functools.partial(_glu_kernel, tn=tn),
        grid=(N_TILES,),
        in_specs=[
            pl.BlockSpec((1, P_ROWS, TOK_WIDTH), lambda m: (m, 0, 0)),
            pl.BlockSpec((1, S_ROWS, TOK_WIDTH), lambda m: (m, 0, 0)),
            pl.BlockSpec((None, TOK_WIDTH, TOK_WIDTH), lambda m: (layer, 0, 0), pipeline_mode=pl.Buffered(1)),
            pl.BlockSpec((None, 1, TOK_WIDTH), lambda m: (layer, 0, 0)),
        ],
        out_specs=pl.BlockSpec((1, TILE, TOK_WIDTH), lambda m: (m, 0, 0)),
        out_shape=jax.ShapeDtypeStruct((N_TILES, TILE, TOK_WIDTH), BF16),
        scratch_shapes=[pltpu.VMEM((TOK_WIDTH, TOK_WIDTH), BF16), pltpu.VMEM((TILE, TOK_WIDTH), BF16)],
        compiler_params=_cparams(1, 48),
        name="glu",
    )(y_p, y_s, w, b)


PN = 256
PROJ_RESIDENT_BYTES = 12 * 1024 * 1024


def _proj_kernel(*refs, part_steps, final, n_w):
    n_parts = len(part_steps)
    parts = refs[:n_parts]
    w_refs = refs[n_parts:n_parts + n_w]
    x_ref, g_ref = refs[n_parts + n_w:n_parts + n_w + 2]
    outs = refs[n_parts + n_w + 2:]
    if final:
        yp_ref, ys_ref, acc = outs
    else:
        xn_ref, h_ref = outs
        acc = xn_ref.at[0]
    k = pl.program_id(1)
    n_k = sum(part_steps)

    def accumulate(p_ref, w_ref, first):
        lhs = p_ref[0]
        for n in range(D_MODEL // PN):
            cols = slice(n * PN, (n + 1) * PN)
            d = jnp.dot(lhs, w_ref[:, cols], preferred_element_type=F32)
            xc = x_ref[0, :, (n % (TK // PN)) * PN:(n % (TK // PN) + 1) * PN]
            d = d + jnp.where(k == n // (TK // PN), xc, 0.0)
            if first:
                acc[:, cols] = d
            else:
                acc[:, cols] += d

    spans = []
    start = 0
    for p_ref, cnt in zip(parts, part_steps):
        if n_w > 1:
            spans += [(kk, kk + 1, p_ref, w_refs[kk], kk == 0) for kk in range(start, start + cnt)]
        else:
            lo, hi = start, start + cnt
            if lo == 0:
                spans.append((0, 1, p_ref, w_refs[0], True))
                lo = 1
            if lo < hi:
                spans.append((lo, hi, p_ref, w_refs[0], False))
        start += cnt
    for lo, hi, p_ref, w_ref, first in spans:
        @pl.when((k >= lo) & (k < hi))
        def _(p_ref=p_ref, w_ref=w_ref, first=first):
            accumulate(p_ref, w_ref, first)

    @pl.when(k == n_k - 1)
    def _():
        h = _rms(acc[...], g_ref[...])
        if final:
            yp_ref[0] = h[:P_ROWS]
            ys_ref[0] = h[P_ROWS:]
        else:
            h_ref[0] = h.astype(BF16)


def _proj(parts, w, layer, x, g, g_idx, final=False):
    part_steps = tuple(p.shape[-1] // TK for p in parts)
    n_k = sum(part_steps)
    assert w.dtype == BF16 and n_k * TK == w.shape[1] and n_k >= D_MODEL // TK
    starts = [sum(part_steps[:i]) for i in range(len(parts))]
    in_specs = []
    for s0, cnt in zip(starts, part_steps):
        in_specs.append(pl.BlockSpec(
            (1, TILE, TK), lambda m, k, s0=s0, cnt=cnt: (m, 0, jnp.clip(k - s0, 0, cnt - 1))))
    resident = n_k * TK * D_MODEL * 2 <= PROJ_RESIDENT_BYTES
    if resident:
        w_specs = [pl.BlockSpec((None, TK, D_MODEL), lambda m, k, kk=kk: (layer, kk, 0), pipeline_mode=pl.Buffered(1))
                   for kk in range(n_k)]
    else:
        w_specs = [pl.BlockSpec((None, TK, D_MODEL), lambda m, k: (layer, k, 0))]
    in_specs += w_specs + [
        pl.BlockSpec((1, TILE, TK), lambda m, k: (m, 0, jnp.minimum(k, D_MODEL // TK - 1))),
        pl.BlockSpec((None, 1, D_MODEL), lambda m, k: (g_idx, 0, 0)),
    ]
    if final:
        out_specs = [
            pl.BlockSpec((1, P_ROWS, D_MODEL), lambda m, k: (m, 0, 0)),
            pl.BlockSpec((1, S_ROWS, D_MODEL), lambda m, k: (m, 0, 0)),
        ]
        out_shape = [
            jax.ShapeDtypeStruct((N_TILES, P_ROWS, D_MODEL), F32),
            jax.ShapeDtypeStruct((N_TILES, S_ROWS, D_MODEL), F32),
        ]
        scratch = [pltpu.VMEM((TILE, D_MODEL), F32)]
    else:
        out_specs = [
            pl.BlockSpec((1, TILE, D_MODEL), lambda m, k: (m, 0, 0)),
            pl.BlockSpec((1, TILE, D_MODEL), lambda m, k: (m, 0, 0)),
        ]
        out_shape = [
            jax.ShapeDtypeStruct((N_TILES, TILE, D_MODEL), F32),
            jax.ShapeDtypeStruct((N_TILES, TILE, D_MODEL), BF16),
        ]
        scratch = []
    return pl.pallas_call(
        functools.partial(_proj_kernel, part_steps=part_steps, final=final, n_w=len(w_specs)),
        grid=(N_TILES, n_k),
        in_specs=in_specs,
        out_specs=out_specs,
        out_shape=out_shape,
        scratch_shapes=scratch,
        compiler_params=_cparams(2, 52),
        name="proj_final" if final else "proj",
    )(*parts, *([w] * len(w_specs)), x, g)


def _ffn_up_kernel(h_ref, wa_ref, wg_ref, cw_ref, cb_ref, sc_ref, wd_ref, *rest):
    y_ref, cp_ref, cs_ref, wdb_ref, wbf_ref, as_ref = rest[-6:]
    f = pl.program_id(0)

    @pl.when(f == FF_BLOCKS)
    def _():
        y_ref[...] = jnp.zeros(y_ref.shape, BF16)
        wdb_ref[...] = jnp.zeros(wdb_ref.shape, BF16)

    @pl.when(f < FF_BLOCKS)
    def _():
        wdb_ref[...] = wd_ref[...].astype(BF16)
        wbf_ref[:, :LANE] = wa_ref[...].astype(BF16)
        wbf_ref[:, LANE:] = wg_ref[...].astype(BF16)
        w0, w1, w2 = cw_ref[0:1, :], cw_ref[1:2, :], cw_ref[2:3, :]
        cb = cb_ref[...]
        for j in range(N_TILES):
            r = jnp.dot(h_ref[j], wbf_ref[...], preferred_element_type=F32)
            a, g = r[:, :LANE], r[:, LANE:]
            ap = a[:P_ROWS]
            if j % 2 == 0:
                as_ref[:CONV_PAD, :] = jnp.zeros((CONV_PAD, LANE), F32)
            else:
                as_ref[CONV_PAD - 2:CONV_PAD, :] = as_ref[CONV_PAD + P_ROWS - 2:, :]
            as_ref[CONV_PAD:, :] = ap
            a1 = as_ref[CONV_PAD - 1:CONV_PAD - 1 + P_ROWS, :]
            a2 = as_ref[CONV_PAD - 2:CONV_PAD - 2 + P_ROWS, :]
            c = cb + w0 * a2 + w1 * a1 + w2 * ap
            y_ref[j, :P_ROWS, :] = (jax.nn.silu(c) * g[:P_ROWS]).astype(BF16)
            if j % 2 == 1:
                cp_ref[j // 2] = as_ref[CONV_PAD + P_ROWS - 2:, :]
            a_s = a[P_ROWS:]
            q0 = sc_ref[j * S_ROWS:(j + 1) * S_ROWS, 0, :]
            q1 = sc_ref[j * S_ROWS:(j + 1) * S_ROWS, 1, :]
            cs = cb + w0 * q0 + w1 * q1 + w2 * a_s
            y_ref[j, P_ROWS:, :] = (jax.nn.silu(cs) * g[P_ROWS:]).astype(BF16)
            cs_ref[j * S_ROWS:(j + 1) * S_ROWS, 0, :] = q1
            cs_ref[j * S_ROWS:(j + 1) * S_ROWS, 1, :] = a_s


def _ffn_up(h, w_up, conv_w, conv_b, state_conv, w_down, layer, conv_p_prev, conv_s_prev):
    last = FF_BLOCKS - 1
    fc = lambda f: jnp.minimum(f, last)
    in_specs = [
        pl.BlockSpec((N_TILES, TILE, D_MODEL), lambda f: (0, 0, 0), pipeline_mode=pl.Buffered(1)),
        pl.BlockSpec((None, D_MODEL, LANE), lambda f: (layer, 0, fc(f))),
        pl.BlockSpec((None, D_MODEL, LANE), lambda f: (layer, 0, FF_BLOCKS + fc(f))),
        pl.BlockSpec((None, 3, LANE), lambda f: (layer, 0, fc(f))),
        pl.BlockSpec((None, 1, LANE), lambda f: (layer, 0, fc(f))),
        pl.BlockSpec((None, DEC_BATCH, 2, LANE), lambda f: (layer, 0, 0, fc(f))),
        pl.BlockSpec((None, LANE, D_MODEL), lambda f: (layer, fc(f), 0)),
    ]
    args = [h, w_up, w_up, conv_w, conv_b, state_conv, w_down]
    aliases = {}
    if conv_p_prev is not None:
        in_specs += [pl.BlockSpec(memory_space=pl.ANY), pl.BlockSpec(memory_space=pl.ANY)]
        aliases = {len(args): 1, len(args) + 1: 2}
        args += [conv_p_prev, conv_s_prev]
    return pl.pallas_call(
        _ffn_up_kernel,
        grid=(FF_BLOCKS + 1,),
        in_specs=in_specs,
        out_specs=[
            pl.BlockSpec((N_TILES, TILE, LANE), lambda f: (0, 0, f)),
            pl.BlockSpec((None, BATCH, 2, LANE), lambda f: (layer, 0, 0, fc(f))),
            pl.BlockSpec((None, DEC_BATCH, 2, LANE), lambda f: (layer, 0, 0, fc(f))),
            pl.BlockSpec((None, LANE, D_MODEL), lambda f: (0, f, 0)),
        ],
        out_shape=[
            jax.ShapeDtypeStruct((N_TILES, TILE, FF_PAD), BF16),
            jax.ShapeDtypeStruct((DEPTH, BATCH, 2, D_FF), F32),
            jax.ShapeDtypeStruct((DEPTH, DEC_BATCH, 2, D_FF), F32),
            jax.ShapeDtypeStruct((1, FF_PAD, D_MODEL), BF16),
        ],
        scratch_shapes=[pltpu.VMEM((D_MODEL, 2 * LANE), BF16), pltpu.VMEM((CONV_PAD + P_ROWS, LANE), F32)],
        input_output_aliases=aliases,
        compiler_params=_cparams(1, 56),
        name="ffn_up",
    )(*args)


def kernel(x_prompt, x_sample, mem_prompt, cache_mem_k, cache_mem_v, state_ssm_re, state_ssm_im, state_conv,
           g_mix, g_ffn, g_mem, g_final, w_mem_kv, sg_w_in, sg_w_out, sg_g_v, sg_w_s, sg_b_s, ssm_w_in,
           ssm_w_out, ssm_lam_re, ssm_lam_im, ssm_log_dt, ssm_b_re, ssm_b_im, ssm_c_re, ssm_c_im, ssm_d,
           ssm_w_glu, ssm_b_glu, ffn_w_up, ffn_conv_w, ffn_conv_b, ffn_w_down):
    n_sg, n_ssm = sg_w_in.shape[0], ssm_w_in.shape[0]
    nst = SSM_GROUPS * SSM_STATE
    g_mix3, g_ffn3 = g_mix.reshape(DEPTH, 1, D_MODEL), g_ffn.reshape(DEPTH, 1, D_MODEL)
    g_fin3 = g_final.reshape(1, 1, D_MODEL)
    g_v3 = sg_g_v.reshape(n_sg, 1, TOK_WIDTH)
    sg_bias = jnp.repeat(jnp.swapaxes(sg_b_s, 1, 2), LANE, axis=2)
    sg_coef = jnp.repeat(sg_w_s[:, :, 0, 0], LANE, axis=1).reshape(n_sg, 1, TOK_WIDTH)
    b_glu3 = ssm_b_glu.reshape(n_ssm, 1, TOK_WIDTH)
    conv_b3 = ffn_conv_b.reshape(DEPTH, 1, D_FF)
    prm = _s5_params(ssm_lam_re, ssm_lam_im, ssm_log_dt, ssm_b_re, ssm_b_im, ssm_c_re, ssm_c_im, ssm_d)
    s_re, s_im = state_ssm_re.reshape(n_ssm, DEC_BATCH, nst), state_ssm_im.reshape(n_ssm, DEC_BATCH, nst)

    mem_k, mem_v = _mem_kv(mem_prompt, g_mem, w_mem_kv)
    x, h = _prep(x_prompt, x_sample, g_mix3)
    sg_v = []
    conv_p = conv_s = st_p = st_s = None
    y_prompt = y_sample = None
    for i in range(DEPTH):
        j = i // 2
        if i % 2 == 0:
            uv, uvs = _in_proj(h, sg_w_in, j, 0, 2 * TOK_WIDTH, tn=1024, act=True)
            q, qs = _in_proj(h, sg_w_in, j, 2 * TOK_WIDTH, XA_WIDTH, tn=512, act=False)
            tok, v = _sg_gate(uv, uvs, g_v3, sg_w_s, sg_bias, sg_coef, j)
            sg_v.append(v.reshape(DEC_BATCH, 1, TOK_WIDTH))
            xa, w_o = _xattn(q, qs, 0, mem_k, mem_v, i, cache_mem_k, cache_mem_v, sg_w_out, j)
            x, h = _proj([tok, xa], w_o, 0, x, g_ffn3, i)
        else:
            z, zs = _in_proj(h, ssm_w_in, j, 0, D_MODEL, tn=1024, act=False)
            y_p, *st_p = _s5_prompt(z, prm, j, st_p)
            y_s, *st_s = _s5_sample(zs, s_re, s_im, prm, j, st_s)
            yg = _glu(y_p, y_s, ssm_w_glu, b_glu3, j)
            xa, w_o = _xattn(z, zs, TOK_WIDTH // XA_WIDTH, mem_k, mem_v, i, cache_mem_k, cache_mem_v, ssm_w_out, j)
            x, h = _proj([yg, xa], w_o, 0, x, g_ffn3, i)
        yf, conv_p, conv_s, w_dn = _ffn_up(h, ffn_w_up, ffn_conv_w, conv_b3, state_conv, ffn_w_down, i,
                                           conv_p, conv_s)
        if i + 1 < DEPTH:
            x, h = _proj([yf], w_dn, 0, x, g_mix3, i + 1)
        else:
            y_prompt, y_sample = _proj([yf], w_dn, 0, x, g_fin3, 0, final=True)
    st4 = lambda a, b: a.reshape(n_ssm, b, SSM_GROUPS, SSM_STATE)
    return (y_prompt.reshape(BATCH, SEQ, D_MODEL), y_sample.reshape(DEC_BATCH, 1, D_MODEL), mem_k, mem_v,
            st4(st_p[0], BATCH), st4(st_p[1], BATCH), conv_p,
            st4(st_s[0], DEC_BATCH), st4(st_s[1], DEC_BATCH), conv_s, jnp.stack(sg_v))
```

```python
import functools
import math

import jax
import jax.numpy as jnp
from jax import lax
from jax.experimental import pallas as pl
from jax.experimental.pallas import tpu as pltpu

F32 = jnp.float32
BF16 = jnp.bfloat16

D_MODEL = 2048
BATCH = 4
SEQ = 2048
DEPTH = 4
DEC_BATCH = 128
N_MEM = 256
XA_HEADS = 4
XA_HEAD_DIM = 128
XA_WIDTH = 512
TOK_WIDTH = 1536
CHUNK = 128
SG_GROUPS = 12
SSM_GROUPS = 96
SSM_GROUP_DIM = 16
SSM_STATE = 64
D_FF = 5504
EPS = 1e-6

N_TILES = 8
P_ROWS = 1024
S_ROWS = 16
TILE = P_ROWS + S_ROWS

LANE = 128
FF_BLOCKS = D_FF // LANE
FF_PAD = (FF_BLOCKS + 1) * LANE
TK = 512
CONV_PAD = 8

S5_CB = 256
S5_NCB = TOK_WIDTH // S5_CB
S5_ST = (S5_CB // SSM_GROUP_DIM) * SSM_STATE
S5_SLABS = S5_ST // LANE
S5_TQ = P_ROWS
S5_SEG = 64
S5_NSEG = S5_TQ // S5_SEG
S5_PITCH = S5_SEG + 8
V7X_VMEM_LIMIT = 56 * 1024 * 1024


def _cparams(n_axes, vmem_mb=None):
    kw = dict(dimension_semantics=("arbitrary",) * n_axes)
    if vmem_mb is not None:
        kw["vmem_limit_bytes"] = min(int(vmem_mb * 1024 * 1024), V7X_VMEM_LIMIT)
    return pltpu.CompilerParams(**kw)


def _rms(x, g):
    return x * lax.rsqrt(jnp.mean(x * x, axis=-1, keepdims=True) + EPS) * g


def _prep_kernel(xp_ref, xs_ref, g_ref, x_ref, h_ref):
    xp = xp_ref[0]
    xs = xs_ref[0]
    g = g_ref[...]
    x_ref[0, :P_ROWS] = xp
    x_ref[0, P_ROWS:] = xs
    h_ref[0, :P_ROWS] = _rms(xp, g).astype(BF16)
    h_ref[0, P_ROWS:] = _rms(xs, g).astype(BF16)


def _prep(x_prompt, x_sample, g):
    xp = x_prompt.reshape(N_TILES, P_ROWS, D_MODEL)
    xs = x_sample.reshape(N_TILES, S_ROWS, D_MODEL)
    return pl.pallas_call(
        _prep_kernel,
        grid=(N_TILES,),
        in_specs=[
            pl.BlockSpec((1, P_ROWS, D_MODEL), lambda j: (j, 0, 0)),
            pl.BlockSpec((1, S_ROWS, D_MODEL), lambda j: (j, 0, 0)),
            pl.BlockSpec((None, 1, D_MODEL), lambda j: (0, 0, 0)),
        ],
        out_specs=[
            pl.BlockSpec((1, TILE, D_MODEL), lambda j: (j, 0, 0)),
            pl.BlockSpec((1, TILE, D_MODEL), lambda j: (j, 0, 0)),
        ],
        out_shape=[
            jax.ShapeDtypeStruct((N_TILES, TILE, D_MODEL), F32),
            jax.ShapeDtypeStruct((N_TILES, TILE, D_MODEL), BF16),
        ],
        compiler_params=_cparams(1, 52),
        name="prep",
    )(xp, xs, g)


def _mem_kv_kernel(m_ref, g_ref, w_ref, k_ref, v_ref):
    h = _rms(m_ref[...], g_ref[0]).astype(BF16)
    r = jnp.dot(h, w_ref[0].astype(BF16), preferred_element_type=F32)

    def put(o_ref):
        for b in range(BATCH):
            for hd in range(XA_HEADS):
                o_ref[b, :, hd, :] = r[b * N_MEM:(b + 1) * N_MEM, hd * XA_HEAD_DIM:(hd + 1) * XA_HEAD_DIM]

    @pl.when(pl.program_id(1) == 0)
    def _():
        put(k_ref)

    @pl.when(pl.program_id(1) == 1)
    def _():
        put(v_ref)


def _mem_kv(mem_prompt, g_mem, w_mem_kv):
    rows = BATCH * N_MEM
    mem = mem_prompt.reshape(rows, D_MODEL)
    kv_shape = (DEPTH, BATCH, N_MEM, XA_HEADS, XA_HEAD_DIM)
    kv_spec = pl.BlockSpec((None,) + kv_shape[1:], lambda i, n: (i, 0, 0, 0, 0))
    return pl.pallas_call(
        _mem_kv_kernel,
        grid=(DEPTH, 2),
        in_specs=[
            pl.BlockSpec((rows, D_MODEL), lambda i, n: (0, 0)),
            pl.BlockSpec((1, 1, D_MODEL), lambda i, n: (i, 0, 0)),
            pl.BlockSpec((1, D_MODEL, XA_WIDTH), lambda i, n: (i, 0, n)),
        ],
        out_specs=[kv_spec, kv_spec],
        out_shape=[jax.ShapeDtypeStruct(kv_shape, F32), jax.ShapeDtypeStruct(kv_shape, F32)],
        compiler_params=_cparams(2, 48),
        name="mem_kv",
    )(mem, g_mem.reshape(DEPTH, 1, D_MODEL), w_mem_kv)


def _in_proj_kernel(h_ref, w_ref, z_ref, zs_ref, wbf_ref, *, act):
    @pl.when(pl.program_id(1) == 0)
    def _():
        wbf_ref[...] = w_ref[...].astype(BF16)

    r = jnp.dot(h_ref[0], wbf_ref[...], preferred_element_type=F32)
    if act:
        r = jax.nn.gelu(r)
    z_ref[0] = r.astype(BF16)
    zs_ref[0] = r[P_ROWS:]


def _in_proj(h, w, layer, col0, width, tn, act):
    assert col0 % tn == 0 and width % tn == 0
    c0 = col0 // tn
    return pl.pallas_call(
        functools.partial(_in_proj_kernel, act=act),
        grid=(width // tn, N_TILES),
        in_specs=[
            pl.BlockSpec((1, TILE, D_MODEL), lambda n, m: (m, 0, 0)),
            pl.BlockSpec((None, D_MODEL, tn), lambda n, m: (layer, 0, c0 + n)),
        ],
        out_specs=[
            pl.BlockSpec((1, TILE, tn), lambda n, m: (m, 0, n)),
            pl.BlockSpec((1, S_ROWS, tn), lambda n, m: (m, 0, n)),
        ],
        out_shape=[
            jax.ShapeDtypeStruct((N_TILES, TILE, width), BF16),
            jax.ShapeDtypeStruct((N_TILES, S_ROWS, width), F32),
        ],
        scratch_shapes=[pltpu.VMEM((D_MODEL, tn), BF16)],
        compiler_params=_cparams(2, 48),
        name="in_proj",
    )(h, w)


def _sg_gate_kernel(u_ref, v_ref, us_ref, vs_ref, gv_ref, ws_ref, bias_ref, coef_ref, tok_ref, sgv_ref, vn_ref):
    gv = gv_ref[...]
    vn_ref[...] = _rms(v_ref[0, :P_ROWS, :].astype(F32), gv)
    row = lax.broadcasted_iota(jnp.int32, (CHUNK, CHUNK), 0)
    col = lax.broadcasted_iota(jnp.int32, (CHUNK, CHUNK), 1)
    causal = col <= row
    wms = [jnp.where(causal, ws_ref[g], 0.0).astype(BF16) for g in range(SG_GROUPS)]

    def chunk(c, carry):
        r0 = pl.multiple_of(c * CHUNK, CHUNK)
        for g in range(SG_GROUPS):
            cols = slice(g * LANE, (g + 1) * LANE)
            blk = vn_ref[pl.ds(r0, CHUNK), cols].astype(BF16)
            s = jnp.dot(wms[g], blk, preferred_element_type=F32) + bias_ref[:, cols]
            tok_ref[0, pl.ds(r0, CHUNK), cols] = (u_ref[0, pl.ds(r0, CHUNK), cols].astype(F32) * s).astype(BF16)
        return carry

    lax.fori_loop(0, P_ROWS // CHUNK, chunk, 0)
    vs = _rms(vs_ref[0], gv)
    sgv_ref[0] = vs
    s = coef_ref[...] * vs + bias_ref[0:1, :]
    tok_ref[0, P_ROWS:, :] = (us_ref[0] * s).astype(BF16)


def _sg_gate(uv, uvs, g_v, w_s, bias, coef, layer):
    par = lambda shape: pl.BlockSpec((None,) + shape, lambda j: (layer,) + (0,) * len(shape))
    return pl.pallas_call(
        _sg_gate_kernel,
        grid=(N_TILES,),
        in_specs=[
            pl.BlockSpec((1, TILE, TOK_WIDTH), lambda j: (j, 0, 0)),
            pl.BlockSpec((1, TILE, TOK_WIDTH), lambda j: (j, 0, 1)),
            pl.BlockSpec((1, S_ROWS, TOK_WIDTH), lambda j: (j, 0, 0)),
            pl.BlockSpec((1, S_ROWS, TOK_WIDTH), lambda j: (j, 0, 1)),
            par((1, TOK_WIDTH)),
            par((SG_GROUPS, CHUNK, CHUNK)),
            par((CHUNK, TOK_WIDTH)),
            par((1, TOK_WIDTH)),
        ],
        out_specs=[
            pl.BlockSpec((1, TILE, TOK_WIDTH), lambda j: (j, 0, 0)),
            pl.BlockSpec((1, S_ROWS, TOK_WIDTH), lambda j: (j, 0, 0)),
        ],
        out_shape=[
            jax.ShapeDtypeStruct((N_TILES, TILE, TOK_WIDTH), BF16),
            jax.ShapeDtypeStruct((N_TILES, S_ROWS, TOK_WIDTH), F32),
        ],
        scratch_shapes=[pltpu.VMEM((P_ROWS, TOK_WIDTH), F32)],
        compiler_params=_cparams(1, 40),
        name="sg_gate",
    )(uv, uv, uvs, uvs, g_v, w_s, bias, coef)


XA_SUB = 16


def _xattn_kernel(q_ref, qs_ref, mk_ref, mv_ref, ck_ref, cv_ref, w_ref, o_ref, wb_ref, os_ref, q8_ref, o4_ref):
    scale = XA_HEAD_DIM ** -0.5
    sub = pl.program_id(1)
    wb_ref[...] = w_ref[...].astype(BF16)

    def when(cond):
        return (lambda f: f()) if S_ROWS == XA_SUB else pl.when(cond)

    @when(sub == 0)
    def _():
        for h in range(XA_HEADS):
            cols = slice(h * XA_HEAD_DIM, (h + 1) * XA_HEAD_DIM)
            qh = q_ref[0, :P_ROWS, cols]
            kh = mk_ref[:, h, :].astype(BF16)
            vh = mv_ref[:, h, :].astype(BF16)
            s = lax.dot_general(qh, kh, (((1,), (1,)), ((), ())), preferred_element_type=F32) * scale
            s = s - jnp.max(s, axis=-1, keepdims=True)
            e = jnp.exp(s)
            p = (e / jnp.sum(e, axis=-1, keepdims=True)).astype(BF16)
            o_ref[0, :P_ROWS, cols] = jnp.dot(p, vh, preferred_element_type=F32).astype(BF16)

    o0 = pl.multiple_of(sub * XA_SUB, XA_SUB)
    qs = qs_ref[0, pl.ds(o0, XA_SUB), :] * scale
    for h in range(XA_HEADS):
        qh = qs[:, h * XA_HEAD_DIM:(h + 1) * XA_HEAD_DIM]
        q8_ref[:, h, :] = qh
        q8_ref[:, XA_HEADS + h, :] = qh
    q8 = q8_ref[...]
    s = jnp.sum(ck_ref[...] * q8[:, None], axis=-1, keepdims=True)
    mx = jnp.max(s, axis=1, keepdims=True)
    mx = jnp.maximum(mx[:, :, :XA_HEADS], mx[:, :, XA_HEADS:])
    e = jnp.exp(s - jnp.concatenate([mx, mx], axis=2))
    den = jnp.sum(e, axis=1)
    o8 = jnp.sum(e * cv_ref[...], axis=1)
    o4_ref[...] = (o8[:, :XA_HEADS] + o8[:, XA_HEADS:]) / (den[:, :XA_HEADS] + den[:, XA_HEADS:])
    for h in range(XA_HEADS):
        os_ref[pl.ds(o0, XA_SUB), h * XA_HEAD_DIM:(h + 1) * XA_HEAD_DIM] = o4_ref[:, h, :]

    @when(sub == S_ROWS // XA_SUB - 1)
    def _():
        o_ref[0, P_ROWS:, :] = os_ref[...].astype(BF16)


def _xattn(z, zs, q_blk, mem_k, mem_v, layer, cache_k, cache_v, w_out, w_layer):
    n_sub = S_ROWS // XA_SUB
    w_rows = D_MODEL // (N_TILES * n_sub)
    pair_shape = (DEPTH, DEC_BATCH, N_MEM // 2, 2 * XA_HEADS, XA_HEAD_DIM)
    cache_k, cache_v = cache_k.reshape(pair_shape), cache_v.reshape(pair_shape)
    cache_blk = (None, XA_SUB, N_MEM // 2, 2 * XA_HEADS, XA_HEAD_DIM)
    return pl.pallas_call(
        _xattn_kernel,
        grid=(N_TILES, n_sub),
        in_specs=[
            pl.BlockSpec((1, TILE, XA_WIDTH), lambda j, s: (j, 0, q_blk)),
            pl.BlockSpec((1, S_ROWS, XA_WIDTH), lambda j, s: (j, 0, q_blk)),
            pl.BlockSpec((None, None, N_MEM, XA_HEADS, XA_HEAD_DIM), lambda j, s: (layer, j // 2, 0, 0, 0)),
            pl.BlockSpec((None, None, N_MEM, XA_HEADS, XA_HEAD_DIM), lambda j, s: (layer, j // 2, 0, 0, 0)),
            pl.BlockSpec(cache_blk, lambda j, s: (layer, j * n_sub + s, 0, 0, 0)),
            pl.BlockSpec(cache_blk, lambda j, s: (layer, j * n_sub + s, 0, 0, 0)),
            pl.BlockSpec((None, w_rows, D_MODEL), lambda j, s: (w_layer, j * n_sub + s, 0)),
        ],
        out_specs=[
            pl.BlockSpec((1, TILE, XA_WIDTH), lambda j, s: (j, 0, 0)),
            pl.BlockSpec((None, w_rows, D_MODEL), lambda j, s: (0, j * n_sub + s, 0)),
        ],
        out_shape=[
            jax.ShapeDtypeStruct((N_TILES, TILE, XA_WIDTH), BF16),
            jax.ShapeDtypeStruct((1, D_MODEL, D_MODEL), BF16),
        ],
        scratch_shapes=[
            pltpu.VMEM((S_ROWS, XA_WIDTH), F32),
            pltpu.VMEM((XA_SUB, 2 * XA_HEADS, XA_HEAD_DIM), F32),
            pltpu.VMEM((XA_SUB, XA_HEADS, XA_HEAD_DIM), F32),
        ],
        compiler_params=_cparams(2, 56),
        name="xattn",
    )(z, zs, mem_k, mem_v, cache_k, cache_v, w_out)


def _s5_kernel(u_ref, bt_ref, ct_ref, mask_ref, lbr_ref, lbi_ref, d_ref, *rest):
    y_ref, hre_ref, him_ref, xa_ref, xb_ref, ha_ref, hb_ref, hs_ref = rest[-8:]
    tq = pl.program_id(1)

    @pl.when(tq == 0)
    def _():
        hs_ref[...] = jnp.zeros_like(hs_ref)

    bm = _s5_expand(bt_ref, mask_ref).astype(BF16)
    cm = _s5_expand(ct_ref, mask_ref).T.astype(BF16)
    d = d_ref[...]
    lbr = [lbr_ref[0, p:p + 1, :] for p in range(S5_SLABS)]
    lbi = [lbi_ref[0, p:p + 1, :] for p in range(S5_SLABS)]

    def seg_rows(ref, g):
        r0 = g * S5_SEG if isinstance(g, int) else pl.multiple_of(g * S5_SEG, S5_SEG)
        return r0, jnp.concatenate([ref[b, 0, pl.ds(r0, S5_SEG), :] for b in range(BATCH)], axis=0)

    def x_piece(g, dst):
        _, lhs = seg_rows(u_ref, g)
        x = jnp.dot(lhs, bm, preferred_element_type=F32)
        for b in range(BATCH):
            for s in range(2 * S5_SLABS):
                dst[s, b * S5_PITCH:b * S5_PITCH + S5_SEG, :] = x[b * S5_SEG:(b + 1) * S5_SEG, s * LANE:(s + 1) * LANE]

    def scan_seg(src, dst, carry):
        carry = list(carry)
        for tau in range(S5_SEG):
            rows = pl.ds(tau, BATCH, stride=S5_PITCH)
            for p in range(S5_SLABS):
                hr, hi = carry[2 * p], carry[2 * p + 1]
                nr = lbr[p] * hr - lbi[p] * hi + src[p, rows, :]
                ni = lbr[p] * hi + lbi[p] * hr + src[S5_SLABS + p, rows, :]
                dst[p, rows, :] = nr
                dst[S5_SLABS + p, rows, :] = ni
                carry[2 * p], carry[2 * p + 1] = nr, ni
        return tuple(carry)

    def y_piece(g, src):
        r0, u = seg_rows(u_ref, g)
        hcat = jnp.concatenate(
            [jnp.concatenate([src[s, b * S5_PITCH:b * S5_PITCH + S5_SEG, :].astype(BF16)
                              for s in range(2 * S5_SLABS)], axis=1) for b in range(BATCH)], axis=0)
        y = jax.nn.gelu(jnp.dot(hcat, cm, preferred_element_type=F32) + d * u.astype(F32))
        for b in range(BATCH):
            y_ref[b, 0, pl.ds(r0, S5_SEG), :] = y[b * S5_SEG:(b + 1) * S5_SEG]

    carry = []
    for p in range(S5_SLABS):
        carry += [hs_ref[p, :BATCH, :], hs_ref[S5_SLABS + p, :BATCH, :]]
    x_piece(0, xa_ref)
    x_piece(1, xb_ref)
    carry = scan_seg(xa_ref, ha_ref, tuple(carry))

    def body(k, carry):
        g = 2 * k + 1
        x_piece(g + 1, xa_ref)
        carry = scan_seg(xb_ref, hb_ref, carry)
        y_piece(g - 1, ha_ref)
        x_piece(g + 2, xb_ref)
        carry = scan_seg(xa_ref, ha_ref, carry)
        y_piece(g, hb_ref)
        return carry

    carry = lax.fori_loop(0, (S5_NSEG - 2) // 2, body, carry)
    carry = scan_seg(xb_ref, hb_ref, carry)
    y_piece(S5_NSEG - 2, ha_ref)
    y_piece(S5_NSEG - 1, hb_ref)
    for p in range(S5_SLABS):
        hs_ref[p, :BATCH, :] = carry[2 * p]
        hs_ref[S5_SLABS + p, :BATCH, :] = carry[2 * p + 1]
        hre_ref[:, p * LANE:(p + 1) * LANE] = carry[2 * p]
        him_ref[:, p * LANE:(p + 1) * LANE] = carry[2 * p + 1]


def _s5_prompt(z, prm, layer, prev):
    z4 = z.reshape(BATCH, 2, TILE, D_MODEL)
    blk = (BATCH, 1, S5_TQ, S5_CB)
    nst = SSM_GROUPS * SSM_STATE
    n_layers = prm["d"].shape[0]
    in_specs = [
        pl.BlockSpec(blk, lambda c, t: (0, t, 0, c)),
        pl.BlockSpec((None, 1, SSM_GROUP_DIM, 2 * S5_ST), lambda c, t: (layer, c, 0, 0)),
        pl.BlockSpec((None, 1, SSM_GROUP_DIM, 2 * S5_ST), lambda c, t: (layer, c, 0, 0)),
        pl.BlockSpec((S5_CB, 2 * S5_ST), lambda c, t: (0, 0)),
        pl.BlockSpec((None, 1, S5_SLABS, LANE), lambda c, t: (layer, c, 0, 0)),
        pl.BlockSpec((None, 1, S5_SLABS, LANE), lambda c, t: (layer, c, 0, 0)),
        pl.BlockSpec((None, 1, S5_CB), lambda c, t: (layer, 0, c)),
    ]
    args = [z4, prm["bt"], prm["ct"], prm["mask"], prm["lbr"].reshape(n_layers, S5_NCB, S5_SLABS, LANE),
            prm["lbi"].reshape(n_layers, S5_NCB, S5_SLABS, LANE), prm["d"]]
    aliases = {}
    if prev is not None:
        in_specs += [pl.BlockSpec(memory_space=pl.ANY), pl.BlockSpec(memory_space=pl.ANY)]
        aliases = {len(args): 1, len(args) + 1: 2}
        args += list(prev)
    y, hre, him = pl.pallas_call(
        _s5_kernel,
        grid=(S5_NCB, SEQ // S5_TQ),
        in_specs=in_specs,
        out_specs=[
            pl.BlockSpec(blk, lambda c, t: (0, t, 0, c)),
            pl.BlockSpec((None, BATCH, S5_ST), lambda c, t: (layer, 0, c)),
            pl.BlockSpec((None, BATCH, S5_ST), lambda c, t: (layer, 0, c)),
        ],
        out_shape=[
            jax.ShapeDtypeStruct((BATCH, 2, P_ROWS, TOK_WIDTH), F32),
            jax.ShapeDtypeStruct((n_layers, BATCH, nst), F32),
            jax.ShapeDtypeStruct((n_layers, BATCH, nst), F32),
        ],
        scratch_shapes=[pltpu.VMEM((2 * S5_SLABS, BATCH * S5_PITCH, LANE), F32)] * 4
        + [pltpu.VMEM((2 * S5_SLABS, 8, LANE), F32)],
        input_output_aliases=aliases,
        compiler_params=_cparams(2, 40),
        name="s5_prompt",
    )(*args)
    return y.reshape(N_TILES, P_ROWS, TOK_WIDTH), hre, him


def _s5_sample_kernel(u_ref, sre_ref, sim_ref, bt_ref, ct_ref, mask_ref, lbr_ref, lbi_ref, d_ref, *rest):
    y_ref, nre_ref, nim_ref = rest[-3:]
    u = u_ref[...].reshape(DEC_BATCH, S5_CB)
    uh = u.astype(BF16)
    ul = (u - uh.astype(F32)).astype(BF16)
    bm = _s5_expand(bt_ref, mask_ref)
    bh = bm.astype(BF16)
    bl = (bm - bh.astype(F32)).astype(BF16)
    cm = _s5_expand(ct_ref, mask_ref).T.astype(BF16)
    x = (jnp.dot(ul, bh, preferred_element_type=F32) + jnp.dot(uh, bl, preferred_element_type=F32)
         + jnp.dot(uh, bh, preferred_element_type=F32))
    lbr, lbi = lbr_ref[...], lbi_ref[...]
    sr, si = sre_ref[...], sim_ref[...]
    nr = lbr * sr - lbi * si + x[:, :S5_ST]
    ni = lbr * si + lbi * sr + x[:, S5_ST:]
    nre_ref[...] = nr
    nim_ref[...] = ni
    hcat = jnp.concatenate([nr.astype(BF16), ni.astype(BF16)], axis=1)
    y = jnp.dot(hcat, cm, preferred_element_type=F32) + d_ref[...] * u
    y_ref[...] = jax.nn.gelu(y).reshape(N_TILES, S_ROWS, S5_CB)


def _s5_sample(zs, s_re, s_im, prm, layer, prev):
    nst = SSM_GROUPS * SSM_STATE
    n_layers = prm["d"].shape[0]
    st_spec = pl.BlockSpec((None, DEC_BATCH, S5_ST), lambda c: (layer, 0, c))
    in_specs = [
        pl.BlockSpec((N_TILES, S_ROWS, S5_CB), lambda c: (0, 0, c)),
        st_spec,
        st_spec,
        pl.BlockSpec((None, 1, SSM_GROUP_DIM, 2 * S5_ST), lambda c: (layer, c, 0, 0)),
        pl.BlockSpec((None, 1, SSM_GROUP_DIM, 2 * S5_ST), lambda c: (layer, c, 0, 0)),
        pl.BlockSpec((S5_CB, 2 * S5_ST), lambda c: (0, 0)),
        pl.BlockSpec((None, 1, S5_ST), lambda c: (layer, 0, c)),
        pl.BlockSpec((None, 1, S5_ST), lambda c: (layer, 0, c)),
        pl.BlockSpec((None, 1, S5_CB), lambda c: (layer, 0, c)),
    ]
    args = [zs, s_re, s_im, prm["bt"], prm["ct"], prm["mask"], prm["lbr"].reshape(n_layers, 1, nst),
            prm["lbi"].reshape(n_layers, 1, nst), prm["d"]]
    aliases = {}
    if prev is not None:
        in_specs += [pl.BlockSpec(memory_space=pl.ANY), pl.BlockSpec(memory_space=pl.ANY)]
        aliases = {len(args): 1, len(args) + 1: 2}
        args += list(prev)
    return pl.pallas_call(
        _s5_sample_kernel,
        grid=(S5_NCB,),
        in_specs=in_specs,
        out_specs=[pl.BlockSpec((N_TILES, S_ROWS, S5_CB), lambda c: (0, 0, c)), st_spec, st_spec],
        out_shape=[
            jax.ShapeDtypeStruct((N_TILES, S_ROWS, TOK_WIDTH), F32),
            jax.ShapeDtypeStruct((n_layers, DEC_BATCH, nst), F32),
            jax.ShapeDtypeStruct((n_layers, DEC_BATCH, nst), F32),
        ],
        input_output_aliases=aliases,
        compiler_params=_cparams(1, 40),
        name="s5_sample",
    )(*args)


def _s5_params(lam_re, lam_im, log_dt, b_re, b_im, c_re, c_im, d):
    n_layers = lam_re.shape[0]
    dt = jnp.exp(log_dt)[..., None]
    ar, ai = lam_re * dt, lam_im * dt
    mag = jnp.exp(ar)
    lb_re, lb_im = mag * jnp.cos(ai), mag * jnp.sin(ai)
    nr, ni = lb_re - 1.0, lb_im
    den = lam_re * lam_re + lam_im * lam_im
    k_re = (nr * lam_re + ni * lam_im) / den
    k_im = (ni * lam_re - nr * lam_im) / den
    bb_re = k_re[..., None] * b_re - k_im[..., None] * b_im
    bb_im = k_re[..., None] * b_im + k_im[..., None] * b_re
    gpb = S5_CB // SSM_GROUP_DIM

    def by_channel(m, perm):
        m = m.reshape((n_layers, S5_NCB, gpb) + m.shape[2:])
        return jnp.transpose(m, perm).reshape(n_layers, S5_NCB, SSM_GROUP_DIM, S5_ST)

    b_perm, c_perm = (0, 1, 4, 2, 3), (0, 1, 3, 2, 4)
    bt = jnp.concatenate([by_channel(bb_re, b_perm), by_channel(bb_im, b_perm)], axis=3)
    ct = jnp.concatenate([by_channel(c_re, c_perm), -by_channel(c_im, c_perm)], axis=3)
    row_g = lax.broadcasted_iota(jnp.int32, (S5_CB, 2 * S5_ST), 0) // SSM_GROUP_DIM
    col_g = (lax.broadcasted_iota(jnp.int32, (S5_CB, 2 * S5_ST), 1) % S5_ST) // SSM_STATE
    mask = (row_g == col_g).astype(F32)
    return dict(bt=bt, ct=ct, mask=mask, lbr=lb_re, lbi=lb_im, d=d.reshape(n_layers, 1, TOK_WIDTH))


def _s5_expand(t_ref, mask_ref):
    t = t_ref[0]
    return jnp.concatenate([t] * (S5_CB // SSM_GROUP_DIM), axis=0) * mask_ref[...]


def _glu_kernel(yp_ref, ys_ref, w_ref, b_ref, o_ref, wbf_ref, ybf_ref, *, tn):
    @pl.when(pl.program_id(0) == 0)
    def _():
        wbf_ref[...] = w_ref[...].astype(BF16)

    ybf_ref[:P_ROWS, :] = yp_ref[0].astype(BF16)
    ybf_ref[P_ROWS:, :] = ys_ref[0].astype(BF16)
    for c in range(TOK_WIDTH // tn):
        cols = slice(c * tn, (c + 1) * tn)
        gate = jnp.dot(ybf_ref[...], wbf_ref[:, cols], preferred_element_type=F32) + b_ref[:, cols]
        sg = jax.nn.sigmoid(gate)
        o_ref[0, :P_ROWS, cols] = (yp_ref[0, :, cols] * sg[:P_ROWS]).astype(BF16)
        o_ref[0, P_ROWS:, cols] = (ys_ref[0, :, cols] * sg[P_ROWS:]).astype(BF16)


def _glu(y_p, y_s, w, b, layer):
    tn = 512
    return pl.pallas_call(
        functools.partial(_glu_kernel, tn=tn),
        grid=(N_TILES,),
        in_specs=[
            pl.BlockSpec((1, P_ROWS, TOK_WIDTH), lambda m: (m, 0, 0)),
            pl.BlockSpec((1, S_ROWS, TOK_WIDTH), lambda m: (m, 0, 0)),
            pl.BlockSpec((None, TOK_WIDTH, TOK_WIDTH), lambda m: (layer, 0, 0), pipeline_mode=pl.Buffered(1)),
            pl.BlockSpec((None, 1, TOK_WIDTH), lambda m: (layer, 0, 0)),
        ],
        out_specs=pl.BlockSpec((1, TILE, TOK_WIDTH), lambda m: (m, 0, 0)),
        out_shape=jax.ShapeDtypeStruct((N_TILES, TILE, TOK_WIDTH), BF16),
        scratch_shapes=[pltpu.VMEM((TOK_WIDTH, TOK_WIDTH), BF16), pltpu.VMEM((TILE, TOK_WIDTH), BF16)],
        compiler_params=_cparams(1, 48),
        name="glu",
    )(y_p, y_s, w, b)


PN = 256
PROJ_RESIDENT_BYTES = 12 * 1024 * 1024


def _proj_kernel(*refs, part_steps, final, n_w):
    n_parts = len(part_steps)
    parts = refs[:n_parts]
    w_refs = refs[n_parts:n_parts + n_w]
    x_ref, g_ref = refs[n_parts + n_w:n_parts + n_w + 2]
    outs = refs[n_parts + n_w + 2:]
    if final:
        yp_ref, ys_ref, acc = outs
    else:
        xn_ref, h_ref = outs
        acc = xn_ref.at[0]
    k = pl.program_id(1)
    n_k = sum(part_steps)

    def accumulate(p_ref, w_ref, first):
        lhs = p_ref[0]
        for n in range(D_MODEL // PN):
            cols = slice(n * PN, (n + 1) * PN)
            d = jnp.dot(lhs, w_ref[:, cols], preferred_element_type=F32)
            xc = x_ref[0, :, (n % (TK // PN)) * PN:(n % (TK // PN) + 1) * PN]
            d = d + jnp.where(k == n // (TK // PN), xc, 0.0)
            if first:
                acc[:, cols] = d
            else:
                acc[:, cols] += d

    spans = []
    start = 0
    for p_ref, cnt in zip(parts, part_steps):
        if n_w > 1:
            spans += [(kk, kk + 1, p_ref, w_refs[kk], kk == 0) for kk in range(start, start + cnt)]
        else:
            lo, hi = start, start + cnt
            if lo == 0:
                spans.append((0, 1, p_ref, w_refs[0], True))
                lo = 1
            if lo < hi:
                spans.append((lo, hi, p_ref, w_refs[0], False))
        start += cnt
    for lo, hi, p_ref, w_ref, first in spans:
        @pl.when((k >= lo) & (k < hi))
        def _(p_ref=p_ref, w_ref=w_ref, first=first):
            accumulate(p_ref, w_ref, first)

    @pl.when(k == n_k - 1)
    def _():
        h = _rms(acc[...], g_ref[...])
        if final:
            yp_ref[0] = h[:P_ROWS]
            ys_ref[0] = h[P_ROWS:]
        else:
            h_ref[0] = h.astype(BF16)


def _proj(parts, w, layer, x, g, g_idx, final=False):
    part_steps = tuple(p.shape[-1] // TK for p in parts)
    n_k = sum(part_steps)
    assert w.dtype == BF16 and n_k * TK == w.shape[1] and n_k >= D_MODEL // TK
    starts = [sum(part_steps[:i]) for i in range(len(parts))]
    in_specs = []
    for s0, cnt in zip(starts, part_steps):
        in_specs.append(pl.BlockSpec(
            (1, TILE, TK), lambda m, k, s0=s0, cnt=cnt: (m, 0, jnp.clip(k - s0, 0, cnt - 1))))
    resident = n_k * TK * D_MODEL * 2 <= PROJ_RESIDENT_BYTES
    if resident:
        w_specs = [pl.BlockSpec((None, TK, D_MODEL), lambda m, k, kk=kk: (layer, kk, 0), pipeline_mode=pl.Buffered(1))
                   for kk in range(n_k)]
    else:
        w_specs = [pl.BlockSpec((None, TK, D_MODEL), lambda m, k: (layer, k, 0))]
    in_specs += w_specs + [
        pl.BlockSpec((1, TILE, TK), lambda m, k: (m, 0, jnp.minimum(k, D_MODEL // TK - 1))),
        pl.BlockSpec((None, 1, D_MODEL), lambda m, k: (g_idx, 0, 0)),
    ]
    if final:
        out_specs = [
            pl.BlockSpec((1, P_ROWS, D_MODEL), lambda m, k: (m, 0, 0)),
            pl.BlockSpec((1, S_ROWS, D_MODEL), lambda m, k: (m, 0, 0)),
        ]
        out_shape = [
            jax.ShapeDtypeStruct((N_TILES, P_ROWS, D_MODEL), F32),
            jax.ShapeDtypeStruct((N_TILES, S_ROWS, D_MODEL), F32),
        ]
        scratch = [pltpu.VMEM((TILE, D_MODEL), F32)]
    else:
        out_specs = [
            pl.BlockSpec((1, TILE, D_MODEL), lambda m, k: (m, 0, 0)),
            pl.BlockSpec((1, TILE, D_MODEL), lambda m, k: (m, 0, 0)),
        ]
        out_shape = [
            jax.ShapeDtypeStruct((N_TILES, TILE, D_MODEL), F32),
            jax.ShapeDtypeStruct((N_TILES, TILE, D_MODEL), BF16),
        ]
        scratch = []
    return pl.pallas_call(
        functools.partial(_proj_kernel, part_steps=part_steps, final=final, n_w=len(w_specs)),
        grid=(N_TILES, n_k),
        in_specs=in_specs,
        out_specs=out_specs,
        out_shape=out_shape,
        scratch_shapes=scratch,
        compiler_params=_cparams(2, 52),
        name="proj_final" if final else "proj",
    )(*parts, *([w] * len(w_specs)), x, g)


def _ffn_up_kernel(h_ref, wa_ref, wg_ref, cw_ref, cb_ref, sc_ref, wd_ref, *rest):
    y_ref, cp_ref, cs_ref, wdb_ref, wbf_ref, as_ref = rest[-6:]
    f = pl.program_id(0)

    @pl.when(f == FF_BLOCKS)
    def _():
        y_ref[...] = jnp.zeros(y_ref.shape, BF16)
        wdb_ref[...] = jnp.zeros(wdb_ref.shape, BF16)

    @pl.when(f < FF_BLOCKS)
    def _():
        wdb_ref[...] = wd_ref[...].astype(BF16)
        wbf_ref[:, :LANE] = wa_ref[...].astype(BF16)
        wbf_ref[:, LANE:] = wg_ref[...].astype(BF16)
        w0, w1, w2 = cw_ref[0:1, :], cw_ref[1:2, :], cw_ref[2:3, :]
        cb = cb_ref[...]
        for j in range(N_TILES):
            r = jnp.dot(h_ref[j], wbf_ref[...], preferred_element_type=F32)
            a, g = r[:, :LANE], r[:, LANE:]
            ap = a[:P_ROWS]
            if j % 2 == 0:
                as_ref[:CONV_PAD, :] = jnp.zeros((CONV_PAD, LANE), F32)
            else:
                as_ref[CONV_PAD - 2:CONV_PAD, :] = as_ref[CONV_PAD + P_ROWS - 2:, :]
            as_ref[CONV_PAD:, :] = ap
            a1 = as_ref[CONV_PAD - 1:CONV_PAD - 1 + P_ROWS, :]
            a2 = as_ref[CONV_PAD - 2:CONV_PAD - 2 + P_ROWS, :]
            c = cb + w0 * a2 + w1 * a1 + w2 * ap
            y_ref[j, :P_ROWS, :] = (jax.nn.silu(c) * g[:P_ROWS]).astype(BF16)
            if j % 2 == 1:
                cp_ref[j // 2] = as_ref[CONV_PAD + P_ROWS - 2:, :]
            a_s = a[P_ROWS:]
            q0 = sc_ref[j * S_ROWS:(j + 1) * S_ROWS, 0, :]
            q1 = sc_ref[j * S_ROWS:(j + 1) * S_ROWS, 1, :]
            cs = cb + w0 * q0 + w1 * q1 + w2 * a_s
            y_ref[j, P_ROWS:, :] = (jax.nn.silu(cs) * g[P_ROWS:]).astype(BF16)
            cs_ref[j * S_ROWS:(j + 1) * S_ROWS, 0, :] = q1
            cs_ref[j * S_ROWS:(j + 1) * S_ROWS, 1, :] = a_s


def _ffn_up(h, w_up, conv_w, conv_b, state_conv, w_down, layer, conv_p_prev, conv_s_prev):
    last = FF_BLOCKS - 1
    fc = lambda f: jnp.minimum(f, last)
    in_specs = [
        pl.BlockSpec((N_TILES, TILE, D_MODEL), lambda f: (0, 0, 0), pipeline_mode=pl.Buffered(1)),
        pl.BlockSpec((None, D_MODEL, LANE), lambda f: (layer, 0, fc(f))),
        pl.BlockSpec((None, D_MODEL, LANE), lambda f: (layer, 0, FF_BLOCKS + fc(f))),
        pl.BlockSpec((None, 3, LANE), lambda f: (layer, 0, fc(f))),
        pl.BlockSpec((None, 1, LANE), lambda f: (layer, 0, fc(f))),
        pl.BlockSpec((None, DEC_BATCH, 2, LANE), lambda f: (layer, 0, 0, fc(f))),
        pl.BlockSpec((None, LANE, D_MODEL), lambda f: (layer, fc(f), 0)),
    ]
    args = [h, w_up, w_up, conv_w, conv_b, state_conv, w_down]
    aliases = {}
    if conv_p_prev is not None:
        in_specs += [pl.BlockSpec(memory_space=pl.ANY), pl.BlockSpec(memory_space=pl.ANY)]
        aliases = {len(args): 1, len(args) + 1: 2}
        args += [conv_p_prev, conv_s_prev]
    return pl.pallas_call(
        _ffn_up_kernel,
        grid=(FF_BLOCKS + 1,),
        in_specs=in_specs,
        out_specs=[
            pl.BlockSpec((N_TILES, TILE, LANE), lambda f: (0, 0, f)),
            pl.BlockSpec((None, BATCH, 2, LANE), lambda f: (layer, 0, 0, fc(f))),
            pl.BlockSpec((None, DEC_BATCH, 2, LANE), lambda f: (layer, 0, 0, fc(f))),
            pl.BlockSpec((None, LANE, D_MODEL), lambda f: (0, f, 0)),
        ],
        out_shape=[
            jax.ShapeDtypeStruct((N_TILES, TILE, FF_PAD), BF16),
            jax.ShapeDtypeStruct((DEPTH, BATCH, 2, D_FF), F32),
            jax.ShapeDtypeStruct((DEPTH, DEC_BATCH, 2, D_FF), F32),
            jax.ShapeDtypeStruct((1, FF_PAD, D_MODEL), BF16),
        ],
        scratch_shapes=[pltpu.VMEM((D_MODEL, 2 * LANE), BF16), pltpu.VMEM((CONV_PAD + P_ROWS, LANE), F32)],
        input_output_aliases=aliases,
        compiler_params=_cparams(1, 56),
        name="ffn_up",
    )(*args)


def kernel(x_prompt, x_sample, mem_prompt, cache_mem_k, cache_mem_v, state_ssm_re, state_ssm_im, state_conv,
           g_mix, g_ffn, g_mem, g_final, w_mem_kv, sg_w_in, sg_w_out, sg_g_v, sg_w_s, sg_b_s, ssm_w_in,
           ssm_w_out, ssm_lam_re, ssm_lam_im, ssm_log_dt, ssm_b_re, ssm_b_im, ssm_c_re, ssm_c_im, ssm_d,
           ssm_w_glu, ssm_b_glu, ffn_w_up, ffn_conv_w, ffn_conv_b, ffn_w_down):
    n_sg, n_ssm = sg_w_in.shape[0], ssm_w_in.shape[0]
    nst = SSM_GROUPS * SSM_STATE
    g_mix3, g_ffn3 = g_mix.reshape(DEPTH, 1, D_MODEL), g_ffn.reshape(DEPTH, 1, D_MODEL)
    g_fin3 = g_final.reshape(1, 1, D_MODEL)
    g_v3 = sg_g_v.reshape(n_sg, 1, TOK_WIDTH)
    sg_bias = jnp.repeat(jnp.swapaxes(sg_b_s, 1, 2), LANE, axis=2)
    sg_coef = jnp.repeat(sg_w_s[:, :, 0, 0], LANE, axis=1).reshape(n_sg, 1, TOK_WIDTH)
    b_glu3 = ssm_b_glu.reshape(n_ssm, 1, TOK_WIDTH)
    conv_b3 = ffn_conv_b.reshape(DEPTH, 1, D_FF)
    prm = _s5_params(ssm_lam_re, ssm_lam_im, ssm_log_dt, ssm_b_re, ssm_b_im, ssm_c_re, ssm_c_im, ssm_d)
    s_re, s_im = state_ssm_re.reshape(n_ssm, DEC_BATCH, nst), state_ssm_im.reshape(n_ssm, DEC_BATCH, nst)

    mem_k, mem_v = _mem_kv(mem_prompt, g_mem, w_mem_kv)
    x, h = _prep(x_prompt, x_sample, g_mix3)
    sg_v = []
    conv_p = conv_s = st_p = st_s = None
    y_prompt = y_sample = None
    for i in range(DEPTH):
        j = i // 2
        if i % 2 == 0:
            uv, uvs = _in_proj(h, sg_w_in, j, 0, 2 * TOK_WIDTH, tn=1024, act=True)
            q, qs = _in_proj(h, sg_w_in, j, 2 * TOK_WIDTH, XA_WIDTH, tn=512, act=False)
            tok, v = _sg_gate(uv, uvs, g_v3, sg_w_s, sg_bias, sg_coef, j)
            sg_v.append(v.reshape(DEC_BATCH, 1, TOK_WIDTH))
            xa, w_o = _xattn(q, qs, 0, mem_k, mem_v, i, cache_mem_k, cache_mem_v, sg_w_out, j)
            x, h = _proj([tok, xa], w_o, 0, x, g_ffn3, i)
        else:
            z, zs = _in_proj(h, ssm_w_in, j, 0, D_MODEL, tn=1024, act=False)
            y_p, *st_p = _s5_prompt(z, prm, j, st_p)
            y_s, *st_s = _s5_sample(zs, s_re, s_im, prm, j, st_s)
            yg = _glu(y_p, y_s, ssm_w_glu, b_glu3, j)
            xa, w_o = _xattn(z, zs, TOK_WIDTH // XA_WIDTH, mem_k, mem_v, i, cache_mem_k, cache_mem_v, ssm_w_out, j)
            x, h = _proj([yg, xa], w_o, 0, x, g_ffn3, i)
        yf, conv_p, conv_s, w_dn = _ffn_up(h, ffn_w_up, ffn_conv_w, conv_b3, state_conv, ffn_w_down, i,
                                           conv_p, conv_s)
        if i + 1 < DEPTH:
            x, h = _proj([yf], w_dn, 0, x, g_mix3, i + 1)
        else:
            y_prompt, y_sample = _proj([yf], w_dn, 0, x, g_fin3, 0, final=True)
    st4 = lambda a, b: a.reshape(n_ssm, b, SSM_GROUPS, SSM_STATE)
    return (y_prompt.reshape(BATCH, SEQ, D_MODEL), y_sample.reshape(DEC_BATCH, 1, D_MODEL), mem_k, mem_v,
            st4(st_p[0], BATCH), st4(st_p[1], BATCH), conv_p,
            st4(st_s[0], DEC_BATCH), st4(st_s[1], DEC_BATCH), conv_s, jnp.stack(sg_v))
```

```python
import functools
import math

import jax
import jax.numpy as jnp
from jax import lax
from jax.experimental import pallas as pl
from jax.experimental.pallas import tpu as pltpu

F32 = jnp.float32
BF16 = jnp.bfloat16

D_MODEL = 2048
BATCH = 4
SEQ = 2048
DEPTH = 4
DEC_BATCH = 128
N_MEM = 256
XA_HEADS = 4
XA_HEAD_DIM = 128
XA_WIDTH = 512
TOK_WIDTH = 1536
CHUNK = 128
SG_GROUPS = 12
SSM_GROUPS = 96
SSM_GROUP_DIM = 16
SSM_STATE = 64
D_FF = 5504
EPS = 1e-6

N_TILES = 8
P_ROWS = 1024
S_ROWS = 16
TILE = P_ROWS + S_ROWS

LANE = 128
FF_BLOCKS = D_FF // LANE
FF_PAD = (FF_BLOCKS + 1) * LANE
TK = 512
CONV_PAD = 8

S5_CB = 256
S5_NCB = TOK_WIDTH // S5_CB
S5_ST = (S5_CB // SSM_GROUP_DIM) * SSM_STATE
S5_SLABS = S5_ST // LANE
S5_TQ = P_ROWS
S5_SEG = 64
S5_NSEG = S5_TQ // S5_SEG
S5_PITCH = S5_SEG + 8
V7X_VMEM_LIMIT = 56 * 1024 * 1024


def _cparams(n_axes, vmem_mb=None):
    kw = dict(dimension_semantics=("arbitrary",) * n_axes)
    if vmem_mb is not None:
        kw["vmem_limit_bytes"] = min(int(vmem_mb * 1024 * 1024), V7X_VMEM_LIMIT)
    return pltpu.CompilerParams(**kw)


def _rms(x, g):
    return x * lax.rsqrt(jnp.mean(x * x, axis=-1, keepdims=True) + EPS) * g


def _prep_kernel(xp_ref, xs_ref, g_ref, x_ref, h_ref):
    xp = xp_ref[0]
    xs = xs_ref[0]
    g = g_ref[...]
    x_ref[0, :P_ROWS] = xp
    x_ref[0, P_ROWS:] = xs
    h_ref[0, :P_ROWS] = _rms(xp, g).astype(BF16)
    h_ref[0, P_ROWS:] = _rms(xs, g).astype(BF16)


def _prep(x_prompt, x_sample, g):
    xp = x_prompt.reshape(N_TILES, P_ROWS, D_MODEL)
    xs = x_sample.reshape(N_TILES, S_ROWS, D_MODEL)
    return pl.pallas_call(
        _prep_kernel,
        grid=(N_TILES,),
        in_specs=[
            pl.BlockSpec((1, P_ROWS, D_MODEL), lambda j: (j, 0, 0)),
            pl.BlockSpec((1, S_ROWS, D_MODEL), lambda j: (j, 0, 0)),
            pl.BlockSpec((None, 1, D_MODEL), lambda j: (0, 0, 0)),
        ],
        out_specs=[
            pl.BlockSpec((1, TILE, D_MODEL), lambda j: (j, 0, 0)),
            pl.BlockSpec((1, TILE, D_MODEL), lambda j: (j, 0, 0)),
        ],
        out_shape=[
            jax.ShapeDtypeStruct((N_TILES, TILE, D_MODEL), F32),
            jax.ShapeDtypeStruct((N_TILES, TILE, D_MODEL), BF16),
        ],
        compiler_params=_cparams(1, 52),
        name="prep",
    )(xp, xs, g)


def _mem_kv_kernel(m_ref, g_ref, w_ref, k_ref, v_ref):
    h = _rms(m_ref[...], g_ref[0]).astype(BF16)
    r = jnp.dot(h, w_ref[0].astype(BF16), preferred_element_type=F32)

    def put(o_ref):
        for b in range(BATCH):
            for hd in range(XA_HEADS):
                o_ref[b, :, hd, :] = r[b * N_MEM:(b + 1) * N_MEM, hd * XA_HEAD_DIM:(hd + 1) * XA_HEAD_DIM]

    @pl.when(pl.program_id(1) == 0)
    def _():
        put(k_ref)

    @pl.when(pl.program_id(1) == 1)
    def _():
        put(v_ref)


def _mem_kv(mem_prompt, g_mem, w_mem_kv):
    rows = BATCH * N_MEM
    mem = mem_prompt.reshape(rows, D_MODEL)
    kv_shape = (DEPTH, BATCH, N_MEM, XA_HEADS, XA_HEAD_DIM)
    kv_spec = pl.BlockSpec((None,) + kv_shape[1:], lambda i, n: (i, 0, 0, 0, 0))
    return pl.pallas_call(
        _mem_kv_kernel,
        grid=(DEPTH, 2),
        in_specs=[
            pl.BlockSpec((rows, D_MODEL), lambda i, n: (0, 0)),
            pl.BlockSpec((1, 1, D_MODEL), lambda i, n: (i, 0, 0)),
            pl.BlockSpec((1, D_MODEL, XA_WIDTH), lambda i, n: (i, 0, n)),
        ],
        out_specs=[kv_spec, kv_spec],
        out_shape=[jax.ShapeDtypeStruct(kv_shape, F32), jax.ShapeDtypeStruct(kv_shape, F32)],
        compiler_params=_cparams(2, 48),
        name="mem_kv",
    )(mem, g_mem.reshape(DEPTH, 1, D_MODEL), w_mem_kv)


def _in_proj_kernel(h_ref, w_ref, z_ref, zs_ref, wbf_ref, *, act):
    @pl.when(pl.program_id(1) == 0)
    def _():
        wbf_ref[...] = w_ref[...].astype(BF16)

    r = jnp.dot(h_ref[0], wbf_ref[...], preferred_element_type=F32)
    if act:
        r = jax.nn.gelu(r)
    z_ref[0] = r.astype(BF16)
    zs_ref[0] = r[P_ROWS:]


def _in_proj(h, w, layer, col0, width, tn, act):
    assert col0 % tn == 0 and width % tn == 0
    c0 = col0 // tn
    return pl.pallas_call(
        functools.partial(_in_proj_kernel, act=act),
        grid=(width // tn, N_TILES),
        in_specs=[
            pl.BlockSpec((1, TILE, D_MODEL), lambda n, m: (m, 0, 0)),
            pl.BlockSpec((None, D_MODEL, tn), lambda n, m: (layer, 0, c0 + n)),
        ],
        out_specs=[
            pl.BlockSpec((1, TILE, tn), lambda n, m: (m, 0, n)),
            pl.BlockSpec((1, S_ROWS, tn), lambda n, m: (m, 0, n)),
        ],
        out_shape=[
            jax.ShapeDtypeStruct((N_TILES, TILE, width), BF16),
            jax.ShapeDtypeStruct((N_TILES, S_ROWS, width), F32),
        ],
        scratch_shapes=[pltpu.VMEM((D_MODEL, tn), BF16)],
        compiler_params=_cparams(2, 48),
        name="in_proj",
    )(h, w)


def _sg_gate_kernel(u_ref, v_ref, us_ref, vs_ref, gv_ref, ws_ref, bias_ref, coef_ref, tok_ref, sgv_ref, vn_ref):
    gv = gv_ref[...]
    vn_ref[...] = _rms(v_ref[0, :P_ROWS, :].astype(F32), gv)
    row = lax.broadcasted_iota(jnp.int32, (CHUNK, CHUNK), 0)
    col = lax.broadcasted_iota(jnp.int32, (CHUNK, CHUNK), 1)
    causal = col <= row
    wms = [jnp.where(causal, ws_ref[g], 0.0).astype(BF16) for g in range(SG_GROUPS)]

    def chunk(c, carry):
        r0 = pl.multiple_of(c * CHUNK, CHUNK)
        for g in range(SG_GROUPS):
            cols = slice(g * LANE, (g + 1) * LANE)
            blk = vn_ref[pl.ds(r0, CHUNK), cols].astype(BF16)
            s = jnp.dot(wms[g], blk, preferred_element_type=F32) + bias_ref[:, cols]
            tok_ref[0, pl.ds(r0, CHUNK), cols] = (u_ref[0, pl.ds(r0, CHUNK), cols].astype(F32) * s).astype(BF16)
        return carry

    lax.fori_loop(0, P_ROWS // CHUNK, chunk, 0)
    vs = _rms(vs_ref[0], gv)
    sgv_ref[0] = vs
    s = coef_ref[...] * vs + bias_ref[0:1, :]
    tok_ref[0, P_ROWS:, :] = (us_ref[0] * s).astype(BF16)


def _sg_gate(uv, uvs, g_v, w_s, bias, coef, layer):
    par = lambda shape: pl.BlockSpec((None,) + shape, lambda j: (layer,) + (0,) * len(shape))
    return pl.pallas_call(
        _sg_gate_kernel,
        grid=(N_TILES,),
        in_specs=[
            pl.BlockSpec((1, TILE, TOK_WIDTH), lambda j: (j, 0, 0)),
            pl.BlockSpec((1, TILE, TOK_WIDTH), lambda j: (j, 0, 1)),
            pl.BlockSpec((1, S_ROWS, TOK_WIDTH), lambda j: (j, 0, 0)),
            pl.BlockSpec((1, S_ROWS, TOK_WIDTH), lambda j: (j, 0, 1)),
            par((1, TOK_WIDTH)),
            par((SG_GROUPS, CHUNK, CHUNK)),
            par((CHUNK, TOK_WIDTH)),
            par((1, TOK_WIDTH)),
        ],
        out_specs=[
            pl.BlockSpec((1, TILE, TOK_WIDTH), lambda j: (j, 0, 0)),
            pl.BlockSpec((1, S_ROWS, TOK_WIDTH), lambda j: (j, 0, 0)),
        ],
        out_shape=[
            jax.ShapeDtypeStruct((N_TILES, TILE, TOK_WIDTH), BF16),
            jax.ShapeDtypeStruct((N_TILES, S_ROWS, TOK_WIDTH), F32),
        ],
        scratch_shapes=[pltpu.VMEM((P_ROWS, TOK_WIDTH), F32)],
        compiler_params=_cparams(1, 40),
        name="sg_gate",
    )(uv, uv, uvs, uvs, g_v, w_s, bias, coef)


XA_SUB = 16


def _xattn_kernel(q_ref, qs_ref, mk_ref, mv_ref, ck_ref, cv_ref, w_ref, o_ref, wb_ref, os_ref, q8_ref, o4_ref):
    scale = XA_HEAD_DIM ** -0.5
    sub = pl.program_id(1)
    wb_ref[...] = w_ref[...].astype(BF16)

    def when(cond):
        return (lambda f: f()) if S_ROWS == XA_SUB else pl.when(cond)

    @when(sub == 0)
    def _():
        for h in range(XA_HEADS):
            cols = slice(h * XA_HEAD_DIM, (h + 1) * XA_HEAD_DIM)
            qh = q_ref[0, :P_ROWS, cols]
            kh = mk_ref[:, h, :].astype(BF16)
            vh = mv_ref[:, h, :].astype(BF16)
            s = lax.dot_general(qh, kh, (((1,), (1,)), ((), ())), preferred_element_type=F32) * scale
            s = s - jnp.max(s, axis=-1, keepdims=True)
            e = jnp.exp(s)
            p = (e / jnp.sum(e, axis=-1, keepdims=True)).astype(BF16)
            o_ref[0, :P_ROWS, cols] = jnp.dot(p, vh, preferred_element_type=F32).astype(BF16)

    o0 = pl.multiple_of(sub * XA_SUB, XA_SUB)
    qs = qs_ref[0, pl.ds(o0, XA_SUB), :] * scale
    for h in range(XA_HEADS):
        qh = qs[:, h * XA_HEAD_DIM:(h + 1) * XA_HEAD_DIM]
        q8_ref[:, h, :] = qh
        q8_ref[:, XA_HEADS + h, :] = qh
    q8 = q8_ref[...]
    s = jnp.sum(ck_ref[...] * q8[:, None], axis=-1, keepdims=True)
    mx = jnp.max(s, axis=1, keepdims=True)
    mx = jnp.maximum(mx[:, :, :XA_HEADS], mx[:, :, XA_HEADS:])
    e = jnp.exp(s - jnp.concatenate([mx, mx], axis=2))
    den = jnp.sum(e, axis=1)
    o8 = jnp.sum(e * cv_ref[...], axis=1)
    o4_ref[...] = (o8[:, :XA_HEADS] + o8[:, XA_HEADS:]) / (den[:, :XA_HEADS] + den[:, XA_HEADS:])
    for h in range(XA_HEADS):
        os_ref[pl.ds(o0, XA_SUB), h * XA_HEAD_DIM:(h + 1) * XA_HEAD_DIM] = o4_ref[:, h, :]

    @when(sub == S_ROWS // XA_SUB - 1)
    def _():
        o_ref[0, P_ROWS:, :] = os_ref[...].astype(BF16)


def _xattn(z, zs, q_blk, mem_k, mem_v, layer, cache_k, cache_v, w_out, w_layer):
    n_sub = S_ROWS // XA_SUB
    w_rows = D_MODEL // (N_TILES * n_sub)
    pair_shape = (DEPTH, DEC_BATCH, N_MEM // 2, 2 * XA_HEADS, XA_HEAD_DIM)
    cache_k, cache_v = cache_k.reshape(pair_shape), cache_v.reshape(pair_shape)
    cache_blk = (None, XA_SUB, N_MEM // 2, 2 * XA_HEADS, XA_HEAD_DIM)
    return pl.pallas_call(
        _xattn_kernel,
        grid=(N_TILES, n_sub),
        in_specs=[
            pl.BlockSpec((1, TILE, XA_WIDTH), lambda j, s: (j, 0, q_blk)),
            pl.BlockSpec((1, S_ROWS, XA_WIDTH), lambda j, s: (j, 0, q_blk)),
            pl.BlockSpec((None, None, N_MEM, XA_HEADS, XA_HEAD_DIM), lambda j, s: (layer, j // 2, 0, 0, 0)),
            pl.BlockSpec((None, None, N_MEM, XA_HEADS, XA_HEAD_DIM), lambda j, s: (layer, j // 2, 0, 0, 0)),
            pl.BlockSpec(cache_blk, lambda j, s: (layer, j * n_sub + s, 0, 0, 0)),
            pl.BlockSpec(cache_blk, lambda j, s: (layer, j * n_sub + s, 0, 0, 0)),
            pl.BlockSpec((None, w_rows, D_MODEL), lambda j, s: (w_layer, j * n_sub + s, 0)),
        ],
        out_specs=[
            pl.BlockSpec((1, TILE, XA_WIDTH), lambda j, s: (j, 0, 0)),
            pl.BlockSpec((None, w_rows, D_MODEL), lambda j, s: (0, j * n_sub + s, 0)),
        ],
        out_shape=[
            jax.ShapeDtypeStruct((N_TILES, TILE, XA_WIDTH), BF16),
            jax.ShapeDtypeStruct((1, D_MODEL, D_MODEL), BF16),
        ],
        scratch_shapes=[
            pltpu.VMEM((S_ROWS, XA_WIDTH), F32),
            pltpu.VMEM((XA_SUB, 2 * XA_HEADS, XA_HEAD_DIM), F32),
            pltpu.VMEM((XA_SUB, XA_HEADS, XA_HEAD_DIM), F32),
        ],
        compiler_params=_cparams(2, 56),
        name="xattn",
    )(z, zs, mem_k, mem_v, cache_k, cache_v, w_out)


def _s5_kernel(u_ref, bt_ref, ct_ref, mask_ref, lbr_ref, lbi_ref, d_ref, *rest):
    y_ref, hre_ref, him_ref, xa_ref, xb_ref, ha_ref, hb_ref, hs_ref = rest[-8:]
    tq = pl.program_id(1)

    @pl.when(tq == 0)
    def _():
        hs_ref[...] = jnp.zeros_like(hs_ref)

    bm = _s5_expand(bt_ref, mask_ref).astype(BF16)
    cm = _s5_expand(ct_ref, mask_ref).T.astype(BF16)
    d = d_ref[...]
    lbr = [lbr_ref[0, p:p + 1, :] for p in range(S5_SLABS)]
    lbi = [lbi_ref[0, p:p + 1, :] for p in range(S5_SLABS)]

    def seg_rows(ref, g):
        r0 = g * S5_SEG if isinstance(g, int) else pl.multiple_of(g * S5_SEG, S5_SEG)
        return r0, jnp.concatenate([ref[b, 0, pl.ds(r0, S5_SEG), :] for b in range(BATCH)], axis=0)

    def x_piece(g, dst):
        _, lhs = seg_rows(u_ref, g)
        x = jnp.dot(lhs, bm, preferred_element_type=F32)
        for b in range(BATCH):
            for s in range(2 * S5_SLABS):
                dst[s, b * S5_PITCH:b * S5_PITCH + S5_SEG, :] = x[b * S5_SEG:(b + 1) * S5_SEG, s * LANE:(s + 1) * LANE]

    def scan_seg(src, dst, carry):
        carry = list(carry)
        for tau in range(S5_SEG):
            rows = pl.ds(tau, BATCH, stride=S5_PITCH)
            for p in range(S5_SLABS):
                hr, hi = carry[2 * p], carry[2 * p + 1]
                nr = lbr[p] * hr - lbi[p] * hi + src[p, rows, :]
                ni = lbr[p] * hi + lbi[p] * hr + src[S5_SLABS + p, rows, :]
                dst[p, rows, :] = nr
                dst[S5_SLABS + p, rows, :] = ni
                carry[2 * p], carry[2 * p + 1] = nr, ni
        return tuple(carry)

    def y_piece(g, src):
        r0, u = seg_rows(u_ref, g)
        hcat = jnp.concatenate(
            [jnp.concatenate([src[s, b * S5_PITCH:b * S5_PITCH + S5_SEG, :].astype(BF16)
                              for s in range(2 * S5_SLABS)], axis=1) for b in range(BATCH)], axis=0)
        y = jax.nn.gelu(jnp.dot(hcat, cm, preferred_element_type=F32) + d * u.astype(F32))
        for b in range(BATCH):
            y_ref[b, 0, pl.ds(r0, S5_SEG), :] = y[b * S5_SEG:(b + 1) * S5_SEG]

    carry = []
    for p in range(S5_SLABS):
        carry += [hs_ref[p, :BATCH, :], hs_ref[S5_SLABS + p, :BATCH, :]]
    x_piece(0, xa_ref)
    x_piece(1, xb_ref)
    carry = scan_seg(xa_ref, ha_ref, tuple(carry))

    def body(k, carry):
        g = 2 * k + 1
        x_piece(g + 1, xa_ref)
        carry = scan_seg(xb_ref, hb_ref, carry)
        y_piece(g - 1, ha_ref)
        x_piece(g + 2, xb_ref)
        carry = scan_seg(xa_ref, ha_ref, carry)
        y_piece(g, hb_ref)
        return carry

    carry = lax.fori_loop(0, (S5_NSEG - 2) // 2, body, carry)
    carry = scan_seg(xb_ref, hb_ref, carry)
    y_piece(S5_NSEG - 2, ha_ref)
    y_piece(S5_NSEG - 1, hb_ref)
    for p in range(S5_SLABS):
        hs_ref[p, :BATCH, :] = carry[2 * p]
        hs_ref[S5_SLABS + p, :BATCH, :] = carry[2 * p + 1]
        hre_ref[:, p * LANE:(p + 1) * LANE] = carry[2 * p]
        him_ref[:, p * LANE:(p + 1) * LANE] = carry[2 * p + 1]


def _s5_prompt(z, prm, layer, prev):
    z4 = z.reshape(BATCH, 2, TILE, D_MODEL)
    blk = (BATCH, 1, S5_TQ, S5_CB)
    nst = SSM_GROUPS * SSM_STATE
    n_layers = prm["d"].shape[0]
    in_specs = [
        pl.BlockSpec(blk, lambda c, t: (0, t, 0, c)),
        pl.BlockSpec((None, 1, SSM_GROUP_DIM, 2 * S5_ST), lambda c, t: (layer, c, 0, 0)),
        pl.BlockSpec((None, 1, SSM_GROUP_DIM, 2 * S5_ST), lambda c, t: (layer, c, 0, 0)),
        pl.BlockSpec((S5_CB, 2 * S5_ST), lambda c, t: (0, 0)),
        pl.BlockSpec((None, 1, S5_SLABS, LANE), lambda c, t: (layer, c, 0, 0)),
        pl.BlockSpec((None, 1, S5_SLABS, LANE), lambda c, t: (layer, c, 0, 0)),
        pl.BlockSpec((None, 1, S5_CB), lambda c, t: (layer, 0, c)),
    ]
    args = [z4, prm["bt"], prm["ct"], prm["mask"], prm["lbr"].reshape(n_layers, S5_NCB, S5_SLABS, LANE),
            prm["lbi"].reshape(n_layers, S5_NCB, S5_SLABS, LANE), prm["d"]]
    aliases = {}
    if prev is not None:
        in_specs += [pl.BlockSpec(memory_space=pl.ANY), pl.BlockSpec(memory_space=pl.ANY)]
        aliases = {len(args): 1, len(args) + 1: 2}
        args += list(prev)
    y, hre, him = pl.pallas_call(
        _s5_kernel,
        grid=(S5_NCB, SEQ // S5_TQ),
        in_specs=in_specs,
        out_specs=[
            pl.BlockSpec(blk, lambda c, t: (0, t, 0, c)),
            pl.BlockSpec((None, BATCH, S5_ST), lambda c, t: (layer, 0, c)),
            pl.BlockSpec((None, BATCH, S5_ST), lambda c, t: (layer, 0, c)),
        ],
        out_shape=[
            jax.ShapeDtypeStruct((BATCH, 2, P_ROWS, TOK_WIDTH), F32),
            jax.ShapeDtypeStruct((n_layers, BATCH, nst), F32),
            jax.ShapeDtypeStruct((n_layers, BATCH, nst), F32),
        ],
        scratch_shapes=[pltpu.VMEM((2 * S5_SLABS, BATCH * S5_PITCH, LANE), F32)] * 4
        + [pltpu.VMEM((2 * S5_SLABS, 8, LANE), F32)],
        input_output_aliases=aliases,
        compiler_params=_cparams(2, 40),
        name="s5_prompt",
    )(*args)
    return y.reshape(N_TILES, P_ROWS, TOK_WIDTH), hre, him


def _s5_sample_kernel(u_ref, sre_ref, sim_ref, bt_ref, ct_ref, mask_ref, lbr_ref, lbi_ref, d_ref, *rest):
    y_ref, nre_ref, nim_ref = rest[-3:]
    u = u_ref[...].reshape(DEC_BATCH, S5_CB)
    uh = u.astype(BF16)
    ul = (u - uh.astype(F32)).astype(BF16)
    bm = _s5_expand(bt_ref, mask_ref)
    bh = bm.astype(BF16)
    bl = (bm - bh.astype(F32)).astype(BF16)
    cm = _s5_expand(ct_ref, mask_ref).T.astype(BF16)
    x = (jnp.dot(ul, bh, preferred_element_type=F32) + jnp.dot(uh, bl, preferred_element_type=F32)
         + jnp.dot(uh, bh, preferred_element_type=F32))
    lbr, lbi = lbr_ref[...], lbi_ref[...]
    sr, si = sre_ref[...], sim_ref[...]
    nr = lbr * sr - lbi * si + x[:, :S5_ST]
    ni = lbr * si + lbi * sr + x[:, S5_ST:]
    nre_ref[...] = nr
    nim_ref[...] = ni
    hcat = jnp.concatenate([nr.astype(BF16), ni.astype(BF16)], axis=1)
    y = jnp.dot(hcat, cm, preferred_element_type=F32) + d_ref[...] * u
    y_ref[...] = jax.nn.gelu(y).reshape(N_TILES, S_ROWS, S5_CB)


def _s5_sample(zs, s_re, s_im, prm, layer, prev):
    nst = SSM_GROUPS * SSM_STATE
    n_layers = prm["d"].shape[0]
    st_spec = pl.BlockSpec((None, DEC_BATCH, S5_ST), lambda c: (layer, 0, c))
    in_specs = [
        pl.BlockSpec((N_TILES, S_ROWS, S5_CB), lambda c: (0, 0, c)),
        st_spec,
        st_spec,
        pl.BlockSpec((None, 1, SSM_GROUP_DIM, 2 * S5_ST), lambda c: (layer, c, 0, 0)),
        pl.BlockSpec((None, 1, SSM_GROUP_DIM, 2 * S5_ST), lambda c: (layer, c, 0, 0)),
        pl.BlockSpec((S5_CB, 2 * S5_ST), lambda c: (0, 0)),
        pl.BlockSpec((None, 1, S5_ST), lambda c: (layer, 0, c)),
        pl.BlockSpec((None, 1, S5_ST), lambda c: (layer, 0, c)),
        pl.BlockSpec((None, 1, S5_CB), lambda c: (layer, 0, c)),
    ]
    args = [zs, s_re, s_im, prm["bt"], prm["ct"], prm["mask"], prm["lbr"].reshape(n_layers, 1, nst),
            prm["lbi"].reshape(n_layers, 1, nst), prm["d"]]
    aliases = {}
    if prev is not None:
        in_specs += [pl.BlockSpec(memory_space=pl.ANY), pl.BlockSpec(memory_space=pl.ANY)]
        aliases = {len(args): 1, len(args) + 1: 2}
        args += list(prev)
    return pl.pallas_call(
        _s5_sample_kernel,
        grid=(S5_NCB,),
        in_specs=in_specs,
        out_specs=[pl.BlockSpec((N_TILES, S_ROWS, S5_CB), lambda c: (0, 0, c)), st_spec, st_spec],
        out_shape=[
            jax.ShapeDtypeStruct((N_TILES, S_ROWS, TOK_WIDTH), F32),
            jax.ShapeDtypeStruct((n_layers, DEC_BATCH, nst), F32),
            jax.ShapeDtypeStruct((n_layers, DEC_BATCH, nst), F32),
        ],
        input_output_aliases=aliases,
        compiler_params=_cparams(1, 40),
        name="s5_sample",
    )(*args)


def _s5_params(lam_re, lam_im, log_dt, b_re, b_im, c_re, c_im, d):
    n_layers = lam_re.shape[0]
    dt = jnp.exp(log_dt)[..., None]
    ar, ai = lam_re * dt, lam_im * dt
    mag = jnp.exp(ar)
    lb_re, lb_im = mag * jnp.cos(ai), mag * jnp.sin(ai)
    nr, ni = lb_re - 1.0, lb_im
    den = lam_re * lam_re + lam_im * lam_im
    k_re = (nr * lam_re + ni * lam_im) / den
    k_im = (ni * lam_re - nr * lam_im) / den
    bb_re = k_re[..., None] * b_re - k_im[..., None] * b_im
    bb_im = k_re[..., None] * b_im + k_im[..., None] * b_re
    gpb = S5_CB // SSM_GROUP_DIM

    def by_channel(m, perm):
        m = m.reshape((n_layers, S5_NCB, gpb) + m.shape[2:])
        return jnp.transpose(m, perm).reshape(n_layers, S5_NCB, SSM_GROUP_DIM, S5_ST)

    b_perm, c_perm = (0, 1, 4, 2, 3), (0, 1, 3, 2, 4)
    bt = jnp.concatenate([by_channel(bb_re, b_perm), by_channel(bb_im, b_perm)], axis=3)
    ct = jnp.concatenate([by_channel(c_re, c_perm), -by_channel(c_im, c_perm)], axis=3)
    row_g = lax.broadcasted_iota(jnp.int32, (S5_CB, 2 * S5_ST), 0) // SSM_GROUP_DIM
    col_g = (lax.broadcasted_iota(jnp.int32, (S5_CB, 2 * S5_ST), 1) % S5_ST) // SSM_STATE
    mask = (row_g == col_g).astype(F32)
    return dict(bt=bt, ct=ct, mask=mask, lbr=lb_re, lbi=lb_im, d=d.reshape(n_layers, 1, TOK_WIDTH))


def _s5_expand(t_ref, mask_ref):
    t = t_ref[0]
    return jnp.concatenate([t] * (S5_CB // SSM_GROUP_DIM), axis=0) * mask_ref[...]


def _glu_kernel(yp_ref, ys_ref, w_ref, b_ref, o_ref, wbf_ref, ybf_ref, *, tn):
    @pl.when(pl.program_id(0) == 0)
    def _():
        wbf_ref[...] = w_ref[...].astype(BF16)

    ybf_ref[:P_ROWS, :] = yp_ref[0].astype(BF16)
    ybf_ref[P_ROWS:, :] = ys_ref[0].astype(BF16)
    for c in range(TOK_WIDTH // tn):
        cols = slice(c * tn, (c + 1) * tn)
        gate = jnp.dot(ybf_ref[...], wbf_ref[:, cols], preferred_element_type=F32) + b_ref[:, cols]
        sg = jax.nn.sigmoid(gate)
        o_ref[0, :P_ROWS, cols] = (yp_ref[0, :, cols] * sg[:P_ROWS]).astype(BF16)
        o_ref[0, P_ROWS:, cols] = (ys_ref[0, :, cols] * sg[P_ROWS:]).astype(BF16)


def _glu(y_p, y_s, w, b, layer):
    tn = 512
    return pl.pallas_call(
        functools.partial(_glu_kernel, tn=tn),
        grid=(N_TILES,),
        in_specs=[
            pl.BlockSpec((1, P_ROWS, TOK_WIDTH), lambda m: (m, 0, 0)),
            pl.BlockSpec((1, S_ROWS, TOK_WIDTH), lambda m: (m, 0, 0)),
            pl.BlockSpec((None, TOK_WIDTH, TOK_WIDTH), lambda m: (layer, 0, 0), pipeline_mode=pl.Buffered(1)),
            pl.BlockSpec((None, 1, TOK_WIDTH), lambda m: (layer, 0, 0)),
        ],
        out_specs=pl.BlockSpec((1, TILE, TOK_WIDTH), lambda m: (m, 0, 0)),
        out_shape=jax.ShapeDtypeStruct((N_TILES, TILE, TOK_WIDTH), BF16),
        scratch_shapes=[pltpu.VMEM((TOK_WIDTH, TOK_WIDTH), BF16), pltpu.VMEM((TILE, TOK_WIDTH), BF16)],
        compiler_params=_cparams(1, 48),
        name="glu",
    )(y_p, y_s, w, b)


PN = 256
PROJ_RESIDENT_BYTES = 12 * 1024 * 1024


def _proj_kernel(*refs, part_steps, final, n_w):
    n_parts = len(part_steps)
    parts = refs[:n_parts]
    w_refs = refs[n_parts:n_parts + n_w]
    x_ref, g_ref = refs[n_parts + n_w:n_parts + n_w + 2]
    outs = refs[n_parts + n_w + 2:]
    if final:
        yp_ref, ys_ref, acc = outs
    else:
        xn_ref, h_ref = outs
        acc = xn_ref.at[0]
    k = pl.program_id(1)
    n_k = sum(part_steps)

    def accumulate(p_ref, w_ref, first):
        lhs = p_ref[0]
        for n in range(D_MODEL // PN):
            cols = slice(n * PN, (n + 1) * PN)
            d = jnp.dot(lhs, w_ref[:, cols], preferred_element_type=F32)
            xc = x_ref[0, :, (n % (TK // PN)) * PN:(n % (TK // PN) + 1) * PN]
            d = d + jnp.where(k == n // (TK // PN), xc, 0.0)
            if first:
                acc[:, cols] = d
            else:
                acc[:, cols] += d

    spans = []
    start = 0
    for p_ref, cnt in zip(parts, part_steps):
        if n_w > 1:
            spans += [(kk, kk + 1, p_ref, w_refs[kk], kk == 0) for kk in range(start, start + cnt)]
        else:
            lo, hi = start, start + cnt
            if lo == 0:
                spans.append((0, 1, p_ref, w_refs[0], True))
                lo = 1
            if lo < hi:
                spans.append((lo, hi, p_ref, w_refs[0], False))
        start += cnt
    for lo, hi, p_ref, w_ref, first in spans:
        @pl.when((k >= lo) & (k < hi))
        def _(p_ref=p_ref, w_ref=w_ref, first=first):
            accumulate(p_ref, w_ref, first)

    @pl.when(k == n_k - 1)
    def _():
        h = _rms(acc[...], g_ref[...])
        if final:
            yp_ref[0] = h[:P_ROWS]
            ys_ref[0] = h[P_ROWS:]
        else:
            h_ref[0] = h.astype(BF16)


def _proj(parts, w, layer, x, g, g_idx, final=False):
    part_steps = tuple(p.shape[-1] // TK for p in parts)
    n_k = sum(part_steps)
    assert w.dtype == BF16 and n_k * TK == w.shape[1] and n_k >= D_MODEL // TK
    starts = [sum(part_steps[:i]) for i in range(len(parts))]
    in_specs = []
    for s0, cnt in zip(starts, part_steps):
        in_specs.append(pl.BlockSpec(
            (1, TILE, TK), lambda m, k, s0=s0, cnt=cnt: (m, 0, jnp.clip(k - s0, 0, cnt - 1))))
    resident = n_k * TK * D_MODEL * 2 <= PROJ_RESIDENT_BYTES
    if resident:
        w_specs = [pl.BlockSpec((None, TK, D_MODEL), lambda m, k, kk=kk: (layer, kk, 0), pipeline_mode=pl.Buffered(1))
                   for kk in range(n_k)]
    else:
        w_specs = [pl.BlockSpec((None, TK, D_MODEL), lambda m, k: (layer, k, 0))]
    in_specs += w_specs + [
        pl.BlockSpec((1, TILE, TK), lambda m, k: (m, 0, jnp.minimum(k, D_MODEL // TK - 1))),
        pl.BlockSpec((None, 1, D_MODEL), lambda m, k: (g_idx, 0, 0)),
    ]
    if final:
        out_specs = [
            pl.BlockSpec((1, P_ROWS, D_MODEL), lambda m, k: (m, 0, 0)),
            pl.BlockSpec((1, S_ROWS, D_MODEL), lambda m, k: (m, 0, 0)),
        ]
        out_shape = [
            jax.ShapeDtypeStruct((N_TILES, P_ROWS, D_MODEL), F32),
            jax.ShapeDtypeStruct((N_TILES, S_ROWS, D_MODEL), F32),
        ]
        scratch = [pltpu.VMEM((TILE, D_MODEL), F32)]
    else:
        out_specs = [
            pl.BlockSpec((1, TILE, D_MODEL), lambda m, k: (m, 0, 0)),
            pl.BlockSpec((1, TILE, D_MODEL), lambda m, k: (jnp.maximum(m - (k == 0).astype(jnp.int32), 0), 0, 0)),
        ]
        out_shape = [
            jax.ShapeDtypeStruct((N_TILES, TILE, D_MODEL), F32),
            jax.ShapeDtypeStruct((N_TILES, TILE, D_MODEL), BF16),
        ]
        scratch = []
    return pl.pallas_call(
        functools.partial(_proj_kernel, part_steps=part_steps, final=final, n_w=len(w_specs)),
        grid=(N_TILES, n_k),
        in_specs=in_specs,
        out_specs=out_specs,
        out_shape=out_shape,
        scratch_shapes=scratch,
        compiler_params=_cparams(2, 52),
        name="proj_final" if final else "proj",
    )(*parts, *([w] * len(w_specs)), x, g)


def _ffn_up_kernel(h_ref, wa_ref, wg_ref, cw_ref, cb_ref, sc_ref, wd_ref, *rest):
    y_ref, cp_ref, cs_ref, wdb_ref, wbf_ref, as_ref = rest[-6:]
    f = pl.program_id(0)

    @pl.when(f == FF_BLOCKS)
    def _():
        y_ref[...] = jnp.zeros(y_ref.shape, BF16)
        wdb_ref[...] = jnp.zeros(wdb_ref.shape, BF16)

    @pl.when(f < FF_BLOCKS)
    def _():
        wdb_ref[...] = wd_ref[...].astype(BF16)
        wbf_ref[:, :LANE] = wa_ref[...].astype(BF16)
        wbf_ref[:, LANE:] = wg_ref[...].astype(BF16)
        w0, w1, w2 = cw_ref[0:1, :], cw_ref[1:2, :], cw_ref[2:3, :]
        cb = cb_ref[...]
        for j in range(N_TILES):
            r = jnp.dot(h_ref[j], wbf_ref[...], preferred_element_type=F32)
            a, g = r[:, :LANE], r[:, LANE:]
            ap = a[:P_ROWS]
            if j % 2 == 0:
                as_ref[:CONV_PAD, :] = jnp.zeros((CONV_PAD, LANE), F32)
            else:
                as_ref[CONV_PAD - 2:CONV_PAD, :] = as_ref[CONV_PAD + P_ROWS - 2:, :]
            as_ref[CONV_PAD:, :] = ap
            a1 = as_ref[CONV_PAD - 1:CONV_PAD - 1 + P_ROWS, :]
            a2 = as_ref[CONV_PAD - 2:CONV_PAD - 2 + P_ROWS, :]
            c = cb + w0 * a2 + w1 * a1 + w2 * ap
            y_ref[j, :P_ROWS, :] = (jax.nn.silu(c) * g[:P_ROWS]).astype(BF16)
            if j % 2 == 1:
                cp_ref[j // 2] = as_ref[CONV_PAD + P_ROWS - 2:, :]
            a_s = a[P_ROWS:]
            q0 = sc_ref[j * S_ROWS:(j + 1) * S_ROWS, 0, :]
            q1 = sc_ref[j * S_ROWS:(j + 1) * S_ROWS, 1, :]
            cs = cb + w0 * q0 + w1 * q1 + w2 * a_s
            y_ref[j, P_ROWS:, :] = (jax.nn.silu(cs) * g[P_ROWS:]).astype(BF16)
            cs_ref[j * S_ROWS:(j + 1) * S_ROWS, 0, :] = q1
            cs_ref[j * S_ROWS:(j + 1) * S_ROWS, 1, :] = a_s


def _ffn_up(h, w_up, conv_w, conv_b, state_conv, w_down, layer, conv_p_prev, conv_s_prev):
    last = FF_BLOCKS - 1
    fc = lambda f: jnp.minimum(f, last)
    in_specs = [
        pl.BlockSpec((N_TILES, TILE, D_MODEL), lambda f: (0, 0, 0), pipeline_mode=pl.Buffered(1)),
        pl.BlockSpec((None, D_MODEL, LANE), lambda f: (layer, 0, fc(f))),
        pl.BlockSpec((None, D_MODEL, LANE), lambda f: (layer, 0, FF_BLOCKS + fc(f))),
        pl.BlockSpec((None, 3, LANE), lambda f: (layer, 0, fc(f))),
        pl.BlockSpec((None, 1, LANE), lambda f: (layer, 0, fc(f))),
        pl.BlockSpec((None, DEC_BATCH, 2, LANE), lambda f: (layer, 0, 0, fc(f))),
        pl.BlockSpec((None, LANE, D_MODEL), lambda f: (layer, fc(f), 0)),
    ]
    args = [h, w_up, w_up, conv_w, conv_b, state_conv, w_down]
    aliases = {}
    if conv_p_prev is not None:
        in_specs += [pl.BlockSpec(memory_space=pl.ANY), pl.BlockSpec(memory_space=pl.ANY)]
        aliases = {len(args): 1, len(args) + 1: 2}
        args += [conv_p_prev, conv_s_prev]
    return pl.pallas_call(
        _ffn_up_kernel,
        grid=(FF_BLOCKS + 1,),
        in_specs=in_specs,
        out_specs=[
            pl.BlockSpec((N_TILES, TILE, LANE), lambda f: (0, 0, f)),
            pl.BlockSpec((None, BATCH, 2, LANE), lambda f: (layer, 0, 0, fc(f))),
            pl.BlockSpec((None, DEC_BATCH, 2, LANE), lambda f: (layer, 0, 0, fc(f))),
            pl.BlockSpec((None, LANE, D_MODEL), lambda f: (0, f, 0)),
        ],
        out_shape=[
            jax.ShapeDtypeStruct((N_TILES, TILE, FF_PAD), BF16),
            jax.ShapeDtypeStruct((DEPTH, BATCH, 2, D_FF), F32),
            jax.ShapeDtypeStruct((DEPTH, DEC_BATCH, 2, D_FF), F32),
            jax.ShapeDtypeStruct((1, FF_PAD, D_MODEL), BF16),
        ],
        scratch_shapes=[pltpu.VMEM((D_MODEL, 2 * LANE), BF16), pltpu.VMEM((CONV_PAD + P_ROWS, LANE), F32)],
        input_output_aliases=aliases,
        compiler_params=_cparams(1, 56),
        name="ffn_up",
    )(*args)


def kernel(x_prompt, x_sample, mem_prompt, cache_mem_k, cache_mem_v, state_ssm_re, state_ssm_im, state_conv,
           g_mix, g_ffn, g_mem, g_final, w_mem_kv, sg_w_in, sg_w_out, sg_g_v, sg_w_s, sg_b_s, ssm_w_in,
           ssm_w_out, ssm_lam_re, ssm_lam_im, ssm_log_dt, ssm_b_re, ssm_b_im, ssm_c_re, ssm_c_im, ssm_d,
           ssm_w_glu, ssm_b_glu, ffn_w_up, ffn_conv_w, ffn_conv_b, ffn_w_down):
    n_sg, n_ssm = sg_w_in.shape[0], ssm_w_in.shape[0]
    nst = SSM_GROUPS * SSM_STATE
    g_mix3, g_ffn3 = g_mix.reshape(DEPTH, 1, D_MODEL), g_ffn.reshape(DEPTH, 1, D_MODEL)
    g_fin3 = g_final.reshape(1, 1, D_MODEL)
    g_v3 = sg_g_v.reshape(n_sg, 1, TOK_WIDTH)
    sg_bias = jnp.repeat(jnp.swapaxes(sg_b_s, 1, 2), LANE, axis=2)
    sg_coef = jnp.repeat(sg_w_s[:, :, 0, 0], LANE, axis=1).reshape(n_sg, 1, TOK_WIDTH)
    b_glu3 = ssm_b_glu.reshape(n_ssm, 1, TOK_WIDTH)
    conv_b3 = ffn_conv_b.reshape(DEPTH, 1, D_FF)
    prm = _s5_params(ssm_lam_re, ssm_lam_im, ssm_log_dt, ssm_b_re, ssm_b_im, ssm_c_re, ssm_c_im, ssm_d)
    s_re, s_im = state_ssm_re.reshape(n_ssm, DEC_BATCH, nst), state_ssm_im.reshape(n_ssm, DEC_BATCH, nst)

    mem_k, mem_v = _mem_kv(mem_prompt, g_mem, w_mem_kv)
    x, h = _prep(x_prompt, x_sample, g_mix3)
    sg_v = []
    conv_p = conv_s = st_p = st_s = None
    y_prompt = y_sample = None
    for i in range(DEPTH):
        j = i // 2
        if i % 2 == 0:
            uv, uvs = _in_proj(h, sg_w_in, j, 0, 2 * TOK_WIDTH, tn=1024, act=True)
            q, qs = _in_proj(h, sg_w_in, j, 2 * TOK_WIDTH, XA_WIDTH, tn=512, act=False)
            tok, v = _sg_gate(uv, uvs, g_v3, sg_w_s, sg_bias, sg_coef, j)
            sg_v.append(v.reshape(DEC_BATCH, 1, TOK_WIDTH))
            xa, w_o = _xattn(q, qs, 0, mem_k, mem_v, i, cache_mem_k, cache_mem_v, sg_w_out, j)
            x, h = _proj([tok, xa], w_o, 0, x, g_ffn3, i)
        else:
            z, zs = _in_proj(h, ssm_w_in, j, 0, D_MODEL, tn=1024, act=False)
            y_p, *st_p = _s5_prompt(z, prm, j, st_p)
            y_s, *st_s = _s5_sample(zs, s_re, s_im, prm, j, st_s)
            yg = _glu(y_p, y_s, ssm_w_glu, b_glu3, j)
            xa, w_o = _xattn(z, zs, TOK_WIDTH // XA_WIDTH, mem_k, mem_v, i, cache_mem_k, cache_mem_v, ssm_w_out, j)
            x, h = _proj([yg, xa], w_o, 0, x, g_ffn3, i)
        yf, conv_p, conv_s, w_dn = _ffn_up(h, ffn_w_up, ffn_conv_w, conv_b3, state_conv, ffn_w_down, i,
                                           conv_p, conv_s)
        if i + 1 < DEPTH:
            x, h = _proj([yf], w_dn, 0, x, g_mix3, i + 1)
        else:
            y_prompt, y_sample = _proj([yf], w_dn, 0, x, g_fin3, 0, final=True)
    st4 = lambda a, b: a.reshape(n_ssm, b, SSM_GROUPS, SSM_STATE)
    return (y_prompt.reshape(BATCH, SEQ, D_MODEL), y_sample.reshape(DEC_BATCH, 1, D_MODEL), mem_k, mem_v,
            st4(st_p[0], BATCH), st4(st_p[1], BATCH), conv_p,
            st4(st_s[0], DEC_BATCH), st4(st_s[1], DEC_BATCH), conv_s, jnp.stack(sg_v))
```
